```python
import math
import jax
import jax.numpy as jnp
from jax import lax
import numpy as np

D_MODEL = 1024
BATCH = 8
SEQ = 4096
DEPTH = 4

GRID_W = 64
CTX_LEN = 256
N_EVEN = (DEPTH + 1) // 2
N_ODD = DEPTH // 2
EPS = 1e-6
CONV_W = 4
LRU_WIDTH = D_MODEL // 2
LRU_BLOCKS = 8
LRU_BLOCK = LRU_WIDTH // LRU_BLOCKS
LRU_C = 8.0
RET_HEADS = 4
RET_DK = D_MODEL // 16
RET_DV = D_MODEL // 8
RET_CHUNK = 128
ROPE_BASE = 10000.0
HG_HEADS = 4
HG_DK = D_MODEL // 8
HG_DV = D_MODEL // 8
GDN_HEADS = 4
GDN_DK = D_MODEL // 8
GDN_DV = D_MODEL // 8
LIN_CHUNK = 64
D_FF = 4 * D_MODEL
EV_SIZES = (LRU_WIDTH, LRU_WIDTH, RET_HEADS * RET_DK, RET_HEADS * RET_DK, RET_HEADS * RET_DV, RET_HEADS * RET_DV)
OD_SIZES = (HG_HEADS * HG_DK, HG_HEADS * HG_DK, HG_HEADS * HG_DK, HG_HEADS * HG_DV, HG_HEADS * HG_DV,
            GDN_HEADS * GDN_DK, GDN_HEADS * GDN_DK, GDN_HEADS * GDN_DV, GDN_HEADS * GDN_DV, 2 * GDN_HEADS, 2 * GDN_HEADS)
EV_IN = sum(EV_SIZES)
OD_IN = sum(OD_SIZES)
MIX_WIDTH = LRU_WIDTH + RET_HEADS * RET_DV
F32 = jnp.float32

kernel_name = 'hybrid_bidir_rglru_retention_hgrn2_gdn'


def _rmsnorm(x, g):
    xf = x.astype(F32)
    y = xf * lax.rsqrt(jnp.mean(xf * xf, axis=-1, keepdims=True) + EPS)
    return (y * g.astype(F32)).astype(x.dtype)


def _modulate(h, shift, scale):
    return h * (1.0 + scale) + shift


def _split(t, sizes):
    return jnp.split(t, np.cumsum(sizes)[:-1].tolist(), axis=-1)


def _heads(t, n_heads):
    return t.reshape(t.shape[:-1] + (n_heads, t.shape[-1] // n_heads))


def _dwconv(t, w):
    left = CONV_W // 2
    return lax.conv_general_dilated(t, w.astype(t.dtype)[:, None, :], window_strides=(1,),
                                    padding=[(left, CONV_W - 1 - left)],
                                    dimension_numbers=('NWC', 'WIO', 'NWC'),
                                    feature_group_count=t.shape[-1])


def _l2norm(t):
    t = t.astype(F32)
    return t * lax.rsqrt(jnp.sum(t * t, axis=-1, keepdims=True) + EPS)


def _head_rms(o):
    o = o.astype(F32)
    o = o * lax.rsqrt(jnp.mean(o * o, axis=-1, keepdims=True) + EPS)
    return o.reshape(o.shape[0], o.shape[1], -1)


def _head_groupnorm(o):
    o = o.astype(F32)
    o = o - jnp.mean(o, axis=-1, keepdims=True)
    return _head_rms(o)


def _scan_order(ctx_part, lat_part, reverse):
    if reverse:
        ctx_part, lat_part = jnp.flip(ctx_part, 1), jnp.flip(lat_part, 1)
    return jnp.concatenate([ctx_part, lat_part], axis=1)


def _natural_order(y, n_ctx, reverse):
    if not reverse:
        return y
    return jnp.concatenate([jnp.flip(y[:, :n_ctx], 1), jnp.flip(y[:, n_ctx:], 1)], axis=1)


def _blk(t, chunk):
    b, l = t.shape[0], t.shape[1]
    t = t.astype(F32).reshape((b, l // chunk, chunk) + t.shape[2:])
    return jnp.moveaxis(t, 3, 2)


def _unblk(t):
    b, n, h, c, d = t.shape
    return jnp.moveaxis(t, 2, 3).reshape(b, n * c, h, d)


def _masked_exp(mask, logits):
    return jnp.where(mask, jnp.exp(jnp.where(mask, logits, 0.0)), 0.0)


def _linear_scan(a, b):
    def comb(l, r):
        return l[0] * r[0], r[0] * l[1] + r[1]
    _, h = lax.associative_scan(comb, (a, b), axis=1)
    return h


def _axial_rope(rows):
    n_freq = RET_DK // 4
    inv = jnp.power(ROPE_BASE, -jnp.arange(n_freq, dtype=F32) / n_freq)
    r = jnp.arange(rows, dtype=F32)
    col = jnp.arange(GRID_W, dtype=F32)
    row_ang = jnp.broadcast_to(r[:, None, None] * inv, (rows, GRID_W, n_freq))
    col_ang = jnp.broadcast_to(col[None, :, None] * inv, (rows, GRID_W, n_freq))
    ang = jnp.concatenate([row_ang, col_ang], axis=-1).reshape(rows * GRID_W, 2 * n_freq)
    return jnp.cos(ang), jnp.sin(ang)


def _rope(x, cos, sin):
    half = x.shape[-1] // 2
    xf = x.astype(F32)
    x1, x2 = xf[..., :half], xf[..., half:]
    c, s = cos[None, :, None, :], sin[None, :, None, :]
    return jnp.concatenate([x1 * c - x2 * s, x1 * s + x2 * c], axis=-1).astype(x.dtype)


def _retention_chunked(q, k, v, log_gamma, chunk):
    qb, kb, vb = _blk(q, chunk), _blk(k, chunk), _blk(v, chunk)
    b, n, h, c, dk = qb.shape
    dv = vb.shape[-1]
    pos = jnp.arange(c, dtype=F32)
    diff = pos[:, None] - pos[None, :]
    lg = log_gamma.astype(F32)
    decay_in = jnp.where(diff >= 0, jnp.exp(lg[:, None, None] * jnp.maximum(diff, 0.0)), 0.0)
    q_dec = jnp.exp(lg[:, None] * (pos + 1.0))
    k_dec = jnp.exp(lg[:, None] * (c - 1.0 - pos))
    c_dec = jnp.exp(lg * c)
    inner = jnp.einsum('bnhid,bnhjd->bnhij', qb, kb) * decay_in
    o_in = jnp.einsum('bnhij,bnhjv->bnhiv', inner, vb)

    def step(state, inp):
        q_c, kd_c, v_c = inp
        o_x = jnp.einsum('bhid,bhdv->bhiv', q_c, state) * q_dec[..., None]
        state = state * c_dec[:, None, None] + jnp.einsum('bhjd,bhjv->bhdv', kd_c, v_c)
        return state, o_x

    xs = (jnp.moveaxis(qb, 1, 0), jnp.moveaxis(kb * k_dec[..., None], 1, 0), jnp.moveaxis(vb, 1, 0))
    _, o_x = lax.scan(step, jnp.zeros((b, h, dk, dv), F32), xs)
    return _unblk(o_in + jnp.moveaxis(o_x, 0, 1))


def _gla_chunked(q, k, v, logf, chunk):
    qb, kb, vb = _blk(q, chunk), _blk(k, chunk), _blk(v, chunk)
    ab = jnp.cumsum(_blk(logf, chunk), axis=3)
    b, n, h, c, dk = qb.shape
    dv = vb.shape[-1]
    causal = jnp.tril(jnp.ones((c, c), dtype=bool))[:, :, None]

    def step(state, inp):
        q_c, k_c, v_c, a_c = inp
        dec = _masked_exp(causal, a_c[:, :, :, None, :] - a_c[:, :, None, :, :])
        scores = jnp.einsum('bhid,bhjd,bhijd->bhij', q_c, k_c, dec)
        a_last = a_c[:, :, -1, :]
        o = (jnp.einsum('bhij,bhjv->bhiv', scores, v_c)
             + jnp.einsum('bhid,bhdv->bhiv', q_c * jnp.exp(a_c), state))
        state = (state * jnp.exp(a_last)[..., None]
                 + jnp.einsum('bhjd,bhjv->bhdv', k_c * jnp.exp(a_last[:, :, None, :] - a_c), v_c))
        return state, o

    xs = tuple(jnp.moveaxis(t, 1, 0) for t in (qb, kb, vb, ab))
    _, o = lax.scan(step, jnp.zeros((b, h, dk, dv), F32), xs)
    return _unblk(jnp.moveaxis(o, 0, 1))


def _gated_delta_chunked(q, k, v, beta, g, chunk):
    qb, kb, vb = _blk(q, chunk), _blk(k, chunk), _blk(v, chunk)
    bb = _blk(beta, chunk)
    gc = jnp.cumsum(_blk(g, chunk), axis=-1)
    b, n, h, c, dk = qb.shape
    dv = vb.shape[-1]
    causal = jnp.tril(jnp.ones((c, c), dtype=bool))
    strict = jnp.tril(jnp.ones((c, c), dtype=bool), k=-1)
    gam = _masked_exp(causal, gc[..., :, None] - gc[..., None, :])
    kbeta = kb * bb[..., None]
    m = jnp.where(strict, jnp.einsum('bnhid,bnhjd->bnhij', kbeta, kb) * gam, 0.0)
    rhs = jnp.concatenate([vb * bb[..., None], kbeta * jnp.exp(gc)[..., None]], axis=-1)
    sol = lax.linalg.triangular_solve(m + jnp.eye(c, dtype=F32), rhs, left_side=True, lower=True,
                                      unit_diagonal=True)
    u, w = sol[..., :dv], sol[..., dv:]
    qk = jnp.einsum('bnhid,bnhjd->bnhij', qb, kb) * gam
    q_dec = qb * jnp.exp(gc)[..., None]
    k_dec = kb * jnp.exp(gc[..., -1:] - gc)[..., None]
    g_last = jnp.exp(gc[..., -1])

    def step(state, inp):
        u_c, w_c, qk_c, qd_c, kd_c, gl_c = inp
        v_new = u_c - jnp.einsum('bhcd,bhdv->bhcv', w_c, state)
        o = jnp.einsum('bhcd,bhdv->bhcv', qd_c, state) + jnp.einsum('bhij,bhjv->bhiv', qk_c, v_new)
        state = state * gl_c[..., None, None] + jnp.einsum('bhcd,bhcv->bhdv', kd_c, v_new)
        return state, o

    xs = tuple(jnp.moveaxis(t, 1, 0) for t in (u, w, qk, q_dec, k_dec, g_last))
    _, o = lax.scan(step, jnp.zeros((b, h, dk, dv), F32), xs)
    return _unblk(jnp.moveaxis(o, 0, 1))


def _even_mixer(hc, hl, w_in, conv_w, conv_b, wa, ba, wx, bx, lam, log_gamma, cos, sin):
    n_ctx = hc.shape[1]
    bsz = hl.shape[0]
    xc, gc, qc, kc, vc, zc = _split(hc @ w_in, EV_SIZES)
    xl, gl, ql, kl, vl, zl = _split(hl @ w_in, EV_SIZES)
    uc = _dwconv(xc, conv_w) + conv_b
    ul = _dwconv(xl, conv_w) + conv_b
    h = 0.0
    for d, rev in enumerate((False, True)):
        u = _scan_order(uc, ul, rev)
        seq_len = u.shape[1]
        ub = u.reshape(bsz, seq_len, LRU_BLOCKS, LRU_BLOCK)
        r = jax.nn.sigmoid((jnp.einsum('blki,kij->blkj', ub, wa[d]).reshape(bsz, seq_len, LRU_WIDTH) + ba[d]).astype(F32))
        i = jax.nn.sigmoid((jnp.einsum('blki,kij->blkj', ub, wx[d]).reshape(bsz, seq_len, LRU_WIDTH) + bx[d]).astype(F32))
        log_a = -LRU_C * jax.nn.softplus(-lam[d].astype(F32)) * r
        inp = jnp.sqrt(-jnp.expm1(2.0 * log_a)) * (i * u.astype(F32))
        h = h + _natural_order(_linear_scan(jnp.exp(log_a), inp), n_ctx, rev)
    y_a = h * jax.nn.gelu(jnp.concatenate([gc, gl], axis=1).astype(F32))
    qk_scale = RET_DK ** -0.5
    qc, kc, vc = _heads(qc, RET_HEADS), _heads(kc, RET_HEADS) * qk_scale, _heads(vc, RET_HEADS)
    ql = _rope(_heads(ql, RET_HEADS), cos, sin)
    kl = _rope(_heads(kl, RET_HEADS), cos, sin) * qk_scale
    vl = _heads(vl, RET_HEADS)
    o = 0.0
    for d, rev in enumerate((False, True)):
        o_d = _retention_chunked(_scan_order(qc, ql, rev), _scan_order(kc, kl, rev),
                                 _scan_order(vc, vl, rev), log_gamma[d], RET_CHUNK)
        o = o + _natural_order(o_d, n_ctx, rev)
    y_b = _head_groupnorm(o) * jax.nn.silu(jnp.concatenate([zc, zl], axis=1).astype(F32))
    return jnp.concatenate([y_a, y_b], axis=-1).astype(hl.dtype)


def _odd_mixer(hc, hl, w_in, lb, conv_w, a_log, dt_bias):
    n_ctx = hc.shape[1]
    (hq_c, ff_c, fb_c, hi_c, hz_c, gq_c, gk_c, gv_c, gz_c, gb_c, ga_c) = _split(hc @ w_in, OD_SIZES)
    (hq_l, ff_l, fb_l, hi_l, hz_l, gq_l, gk_l, gv_l, gz_l, gb_l, ga_l) = _split(hl @ w_in, OD_SIZES)
    q_c, q_l = _heads(jax.nn.silu(hq_c), HG_HEADS), _heads(jax.nn.silu(hq_l), HG_HEADS)
    i_c, i_l = _heads(hi_c, HG_HEADS), _heads(hi_l, HG_HEADS)
    o = 0.0
    for d, rev in enumerate((False, True)):
        fpre = _scan_order(ff_c, ff_l, rev) if d == 0 else _scan_order(fb_c, fb_l, rev)
        lbd = lb[d]
        f = lbd + (1.0 - lbd) * jax.nn.sigmoid(fpre.astype(F32))
        logf = _heads(jnp.log(f), HG_HEADS)
        o_d = _gla_chunked(_scan_order(q_c, q_l, rev), _heads(1.0 - f, HG_HEADS), _scan_order(i_c, i_l, rev),
                           logf, LIN_CHUNK)
        o = o + _natural_order(o_d, n_ctx, rev)
    y_c = _head_rms(o) * jax.nn.silu(jnp.concatenate([hz_c, hz_l], axis=1).astype(F32))
    qkv_c = jax.nn.silu(_dwconv(jnp.concatenate([gq_c, gk_c, gv_c], axis=-1), conv_w))
    qkv_l = jax.nn.silu(_dwconv(jnp.concatenate([gq_l, gk_l, gv_l], axis=-1), conv_w))
    dq_c, dk_c, dv_c = _split(qkv_c, (GDN_HEADS * GDN_DK, GDN_HEADS * GDN_DK, GDN_HEADS * GDN_DV))
    dq_l, dk_l, dv_l = _split(qkv_l, (GDN_HEADS * GDN_DK, GDN_HEADS * GDN_DK, GDN_HEADS * GDN_DV))
    q_scale = GDN_DK ** -0.5
    dq_c, dq_l = _l2norm(_heads(dq_c, GDN_HEADS)) * q_scale, _l2norm(_heads(dq_l, GDN_HEADS)) * q_scale
    dk_c, dk_l = _l2norm(_heads(dk_c, GDN_HEADS)), _l2norm(_heads(dk_l, GDN_HEADS))
    dv_c, dv_l = _heads(dv_c, GDN_HEADS), _heads(dv_l, GDN_HEADS)
    o = 0.0
    for d, rev in enumerate((False, True)):
        sl = slice(d * GDN_HEADS, (d + 1) * GDN_HEADS)
        beta = jax.nn.sigmoid(_scan_order(gb_c[..., sl], gb_l[..., sl], rev).astype(F32))
        a_pre = _scan_order(ga_c[..., sl], ga_l[..., sl], rev).astype(F32)
        g = -jnp.exp(a_log[d].astype(F32)) * jax.nn.softplus(a_pre + dt_bias[d].astype(F32))
        o_d = _gated_delta_chunked(_scan_order(dq_c, dq_l, rev), _scan_order(dk_c, dk_l, rev),
                                   _scan_order(dv_c, dv_l, rev), beta, g, LIN_CHUNK)
        o = o + _natural_order(o_d, n_ctx, rev)
    y_d = _head_rms(o) * jax.nn.silu(jnp.concatenate([gz_c, gz_l], axis=1).astype(F32))
    return jnp.concatenate([y_c, y_d], axis=-1).astype(hl.dtype)


def _sqrelu_mlp(h, w1, w2):
    return jnp.square(jax.nn.relu(h @ w1)) @ w2


def setup_inputs(seed: int = 0) -> dict:
    key = jax.random.key(seed)
    keys = iter(jax.random.split(key, 40))

    def nrm(shape, std):
        return jax.random.normal(next(keys), shape, F32) * std

    def unif(shape, lo, hi):
        return jax.random.uniform(next(keys), shape, F32, lo, hi)

    x = nrm((BATCH, SEQ, D_MODEL), 1.0)
    c = nrm((BATCH, D_MODEL), 1.0)
    ctx = nrm((BATCH, CTX_LEN, D_MODEL), 1.0)
    c_ctx = nrm((D_MODEL,), 1.0)
    ada_w = nrm((DEPTH, D_MODEL, 6 * D_MODEL), 0.5 * D_MODEL ** -0.5)
    ada_b = nrm((DEPTH, 6 * D_MODEL), 0.02)
    norm1_g = 1.0 + nrm((DEPTH, D_MODEL), 0.02)
    norm2_g = 1.0 + nrm((DEPTH, D_MODEL), 0.02)
    mix_w_out = nrm((DEPTH, MIX_WIDTH, D_MODEL), MIX_WIDTH ** -0.5)
    mlp_w1 = nrm((DEPTH, D_MODEL, D_FF), D_MODEL ** -0.5)
    mlp_w2 = nrm((DEPTH, D_FF, D_MODEL), D_FF ** -0.5)
    ev_w_in = nrm((N_EVEN, D_MODEL, EV_IN), D_MODEL ** -0.5)
    lru_conv_w = nrm((N_EVEN, CONV_W, LRU_WIDTH), CONV_W ** -0.5)
    lru_conv_b = nrm((N_EVEN, LRU_WIDTH), 0.02)
    lru_wa = nrm((N_EVEN, 2, LRU_BLOCKS, LRU_BLOCK, LRU_BLOCK), LRU_BLOCK ** -0.5)
    lru_ba = nrm((N_EVEN, 2, LRU_WIDTH), 0.02)
    lru_wx = nrm((N_EVEN, 2, LRU_BLOCKS, LRU_BLOCK, LRU_BLOCK), LRU_BLOCK ** -0.5)
    lru_bx = nrm((N_EVEN, 2, LRU_WIDTH), 0.02)
    a_c = unif((N_EVEN, 2, LRU_WIDTH), 0.9, 0.999) ** (1.0 / LRU_C)
    lru_lambda = jnp.log(a_c) - jnp.log1p(-a_c)
    ret_base = jnp.log1p(-jnp.exp2(-5.0 - jnp.arange(RET_HEADS, dtype=F32)))
    ret_log_gamma = ret_base * jnp.exp(nrm((N_EVEN, 2, RET_HEADS), 0.1))
    od_w_in = nrm((N_ODD, D_MODEL, OD_IN), D_MODEL ** -0.5)
    hg_lb_logits = nrm((2, N_ODD, HG_HEADS * HG_DK), 0.1)
    gdn_conv_w = nrm((N_ODD, CONV_W, GDN_HEADS * (2 * GDN_DK + GDN_DV)), CONV_W ** -0.5)
    gdn_a_log = jnp.log(unif((N_ODD, 2, GDN_HEADS), 1.0, 16.0))
    dt = jnp.exp(unif((N_ODD, 2, GDN_HEADS), math.log(1e-3), math.log(1e-1)))
    gdn_dt_bias = dt + jnp.log(-jnp.expm1(-dt))
    final_g = 1.0 + nrm((D_MODEL,), 0.02)
    return {'x': x, 'c': c, 'ctx': ctx, 'c_ctx': c_ctx, 'ada_w': ada_w, 'ada_b': ada_b,
            'norm1_g': norm1_g, 'norm2_g': norm2_g, 'mix_w_out': mix_w_out, 'mlp_w1': mlp_w1,
            'mlp_w2': mlp_w2, 'ev_w_in': ev_w_in, 'lru_conv_w': lru_conv_w, 'lru_conv_b': lru_conv_b,
            'lru_wa': lru_wa, 'lru_ba': lru_ba, 'lru_wx': lru_wx, 'lru_bx': lru_bx,
            'lru_lambda': lru_lambda, 'ret_log_gamma': ret_log_gamma, 'od_w_in': od_w_in,
            'hg_lb_logits': hg_lb_logits, 'gdn_conv_w': gdn_conv_w, 'gdn_a_log': gdn_a_log,
            'gdn_dt_bias': gdn_dt_bias, 'final_g': final_g}


def reference(x, c, ctx, c_ctx, ada_w, ada_b, norm1_g, norm2_g, mix_w_out, mlp_w1, mlp_w2,
              ev_w_in, lru_conv_w, lru_conv_b, lru_wa, lru_ba, lru_wx, lru_bx, lru_lambda,
              ret_log_gamma, od_w_in, hg_lb_logits, gdn_conv_w, gdn_a_log, gdn_dt_bias, final_g):
    n_ctx = ctx.shape[1]
    rows = x.shape[1] // GRID_W
    cos, sin = _axial_rope(rows)
    sm = jax.nn.softmax(hg_lb_logits.astype(F32), axis=1)
    hg_lb = jnp.cumsum(sm, axis=1) - sm[:, :1]
    s_lat = jax.nn.silu(c)
    s_ctx = jax.nn.silu(c_ctx)[None, :]
    xl, xc = x, ctx
    for layer in range(DEPTH):
        mod_l = jnp.split((s_lat @ ada_w[layer] + ada_b[layer])[:, None, :], 6, axis=-1)
        mod_c = jnp.split((s_ctx @ ada_w[layer] + ada_b[layer])[:, None, :], 6, axis=-1)
        hl = _modulate(_rmsnorm(xl, norm1_g[layer]), mod_l[0], mod_l[1])
        hc = _modulate(_rmsnorm(xc, norm1_g[layer]), mod_c[0], mod_c[1])
        if layer % 2 == 0:
            e = layer // 2
            z = _even_mixer(hc, hl, ev_w_in[e], lru_conv_w[e], lru_conv_b[e], lru_wa[e], lru_ba[e],
                            lru_wx[e], lru_bx[e], lru_lambda[e], ret_log_gamma[e], cos, sin)
        else:
            o = layer // 2
            z = _odd_mixer(hc, hl, od_w_in[o], hg_lb[:, o], gdn_conv_w[o], gdn_a_log[o], gdn_dt_bias[o])
        xl = xl + mod_l[2] * (z[:, n_ctx:] @ mix_w_out[layer])
        hl2 = _modulate(_rmsnorm(xl, norm2_g[layer]), mod_l[3], mod_l[4])
        xl = xl + mod_l[5] * _sqrelu_mlp(hl2, mlp_w1[layer], mlp_w2[layer])
        if layer < DEPTH - 1:
            xc = xc + mod_c[2] * (z[:, :n_ctx] @ mix_w_out[layer])
            hc2 = _modulate(_rmsnorm(xc, norm2_g[layer]), mod_c[3], mod_c[4])
            xc = xc + mod_c[5] * _sqrelu_mlp(hc2, mlp_w1[layer], mlp_w2[layer])
    return _rmsnorm(xl, final_g)
```

```python
import functools

import numpy as np
import jax
import jax.numpy as jnp
from jax import lax
from jax.experimental import pallas as pl
from jax.experimental.pallas import tpu as pltpu

F32 = jnp.float32
BF16 = jnp.bfloat16

EPS = 1e-6
GRID_W = 64
CONV_W = 4
CONV_LEFT = CONV_W // 2
LRU_BLOCKS = 8
LRU_C = 8.0
RET_HEADS = 4
RET_CHUNK = 128
ROPE_BASE = 10000.0
LIN_HEADS = 4
LIN_CHUNK = 64
HEAD_DV = 128

LANES = 128
SUBLANES = 8
V7X_VMEM_BYTES = 64 * 1024 * 1024
VMEM_LIMIT = (V7X_VMEM_BYTES * 3) // 4
GROUP = 512

HALO = SUBLANES


def _cparams(*sem):
    return pltpu.CompilerParams(dimension_semantics=sem, vmem_limit_bytes=VMEM_LIMIT)


def _sigmoid(x):
    return 1.0 / (1.0 + jnp.exp(-x))


def _silu(x):
    return x * _sigmoid(x)


def _softplus(x):
    return jnp.maximum(x, 0.0) + jnp.log(1.0 + jnp.exp(-jnp.abs(x)))


def _gelu_tanh(x):
    return 0.5 * x * (1.0 + jnp.tanh(0.7978845608028654 * (x + 0.044715 * x * x * x)))


def _dot(a, b):
    return jnp.dot(a.astype(BF16), b.astype(BF16), preferred_element_type=F32)


def _dot_nt(a, b):
    return lax.dot_general(a.astype(BF16), b.astype(BF16), (((1,), (1,)), ((), ())),
                           preferred_element_type=F32)


def _split_bf16(x):
    hi = x.astype(BF16)
    lo = (x - hi.astype(F32)).astype(BF16)
    return hi, lo


def _dot_sel(m, x):
    hi, lo = _split_bf16(x)
    return (jnp.dot(m, hi, preferred_element_type=F32) + jnp.dot(m, lo, preferred_element_type=F32))


def _dot3(a, b):
    ah, al = _split_bf16(a)
    bh, bl = _split_bf16(b)
    return (jnp.dot(ah, bh, preferred_element_type=F32) + jnp.dot(ah, bl, preferred_element_type=F32)
            + jnp.dot(al, bh, preferred_element_type=F32))


def _row_select(t, tm, n_ctx, mc_ref, ml_ref, idx):
    rows = t * tm + lax.broadcasted_iota(jnp.int32, (tm, 1), 0)
    return jnp.where(rows < n_ctx, mc_ref[idx:idx + 1, :], ml_ref[idx:idx + 1, :])


def _rms(x, g):
    return x * lax.rsqrt(jnp.mean(x * x, axis=-1, keepdims=True) + EPS) * g


def _head_norm(o, center):
    outs = []
    for h in range(o.shape[-1] // HEAD_DV):
        oh = o[:, h * HEAD_DV:(h + 1) * HEAD_DV]
        if center:
            oh = oh - jnp.mean(oh, axis=-1, keepdims=True)
        outs.append(oh * lax.rsqrt(jnp.mean(oh * oh, axis=-1, keepdims=True) + EPS))
    return jnp.concatenate(outs, axis=-1)


def _scan_tile(step, n_first, n_total, reverse):
    if not reverse:
        return step
    return jnp.where(step < n_first, n_first - 1 - step, n_total - 1 - (step - n_first))


def _mods_body(s_ref, w_ref, b_ref, o_ref):
    s = _silu(s_ref[...])
    o_ref[...] = _dot3(s, w_ref[...]) + b_ref[...]


def _mods(c, c_ctx, ada_w, ada_b):
    depth, d, n6 = ada_w.shape
    bsz = c.shape[0]
    rows = -(-(bsz + 1) // SUBLANES) * SUBLANES
    s = jnp.zeros((rows, d), F32).at[:bsz].set(c).at[bsz].set(c_ctx)
    tn = 1024
    out = pl.pallas_call(
        _mods_body,
        grid=(depth, n6 // tn),
        in_specs=[pl.BlockSpec((rows, d), lambda l, n: (0, 0)),
                  pl.BlockSpec((None, d, tn), lambda l, n: (l, 0, n)),
                  pl.BlockSpec((None, 1, tn), lambda l, n: (l, 0, n))],
        out_specs=pl.BlockSpec((None, rows, tn), lambda l, n: (l, 0, n)),
        out_shape=jax.ShapeDtypeStruct((depth, rows, n6), F32),
        compiler_params=_cparams("parallel", "parallel"),
        name="mods",
    )(s, ada_w, ada_b.reshape(depth, 1, n6))
    return out.reshape(depth, rows, 6, d)


def _proj_body(*refs, tm, n_ctx, has_tail):
    if has_tail:
        x_ref, ml_ref, mc_ref, g_ref, w_ref, wt_ref, o_ref, ot_ref, h_scr = refs
    else:
        x_ref, ml_ref, mc_ref, g_ref, w_ref, o_ref, h_scr = refs
    t = pl.program_id(1)

    @pl.when(pl.program_id(2) == 0)
    def _():
        y = _rms(x_ref[...], g_ref[...])
        shift = _row_select(t, tm, n_ctx, mc_ref, ml_ref, 0)
        scale = _row_select(t, tm, n_ctx, mc_ref, ml_ref, 1)
        h_scr[...] = (y * (1.0 + scale) + shift).astype(BF16)
        if has_tail:
            ot_ref[...] = jnp.dot(h_scr[...], wt_ref[...], preferred_element_type=F32)

    o_ref[...] = jnp.dot(h_scr[...], w_ref[...], preferred_element_type=F32)


def _proj(x, mod, g, w, w_tail, *, tm, n_ctx):
    bsz, l, d = x.shape
    ng = w.shape[1] // GROUP
    ctx_row = bsz
    in_specs = [pl.BlockSpec((None, tm, d), lambda b, t, n: (b, t, 0)),
                pl.BlockSpec((None, 6, d), lambda b, t, n: (b, 0, 0)),
                pl.BlockSpec((None, 6, d), lambda b, t, n: (ctx_row, 0, 0)),
                pl.BlockSpec((1, d), lambda b, t, n: (0, 0)),
                pl.BlockSpec((d, GROUP), lambda b, t, n: (0, n))]
    out_specs = [pl.BlockSpec((None, tm, GROUP), lambda b, t, n: (b, t, n))]
    out_shape = [jax.ShapeDtypeStruct((bsz, l, ng * GROUP), F32)]
    args = [x, mod, mod, g.reshape(1, d), w]
    if w_tail is not None:
        in_specs.append(pl.BlockSpec((d, LANES), lambda b, t, n: (0, 0)))
        out_specs.append(pl.BlockSpec((None, tm, LANES), lambda b, t, n: (b, t, 0)))
        out_shape.append(jax.ShapeDtypeStruct((bsz, l, LANES), F32))
        args.append(w_tail)
    outs = pl.pallas_call(
        functools.partial(_proj_body, tm=tm, n_ctx=n_ctx, has_tail=w_tail is not None),
        grid=(bsz, l // tm, ng),
        in_specs=in_specs, out_specs=out_specs, out_shape=out_shape,
        scratch_shapes=[pltpu.VMEM((tm, d), BF16)],
        compiler_params=_cparams("parallel", "parallel", "arbitrary"),
        name="proj",
    )(*args)
    return outs


def _lru_body(x_ref, xp_ref, xn_ref, cw_ref, cb_ref, wa_ref, ba_ref, wx_ref, bx_ref, lam_ref,
              o_ref, ext_scr, a_scr, b_scr, h_scr, *, tl, n_ctx_tiles, n_tiles, reverse):
    step = pl.program_id(0)
    tile = _scan_tile(step, n_ctx_tiles, n_tiles, reverse)
    bsz = x_ref.shape[0]
    width = x_ref.shape[2]
    half = width // 2
    seg_first = jnp.logical_or(tile == 0, tile == n_ctx_tiles)
    seg_last = jnp.logical_or(tile == n_ctx_tiles - 1, tile == n_tiles - 1)
    keep_prev = jnp.where(seg_first, 0.0, 1.0)
    keep_next = jnp.where(seg_last, 0.0, 1.0)
    neg_c_sp = -LRU_C * _softplus(-lam_ref[...])

    @pl.when(step == 0)
    def _():
        h_scr[...] = jnp.zeros_like(h_scr)

    def gates(b, carry):
        ext_scr[0:HALO, :] = xp_ref[b] * keep_prev
        ext_scr[HALO:HALO + tl, :] = x_ref[b]
        ext_scr[HALO + tl:, :] = xn_ref[b] * keep_next
        u = cb_ref[...] + cw_ref[0:1, :] * ext_scr[pl.ds(HALO - CONV_LEFT, tl), :]
        for k in range(1, CONV_W):
            u = u + cw_ref[k:k + 1, :] * ext_scr[pl.ds(HALO - CONV_LEFT + k, tl), :]
        ub = u.astype(BF16)
        r_pre = jnp.concatenate(
            [jnp.dot(ub[:, :half], wa_ref[0], preferred_element_type=F32),
             jnp.dot(ub[:, half:], wa_ref[1], preferred_element_type=F32)], axis=-1) + ba_ref[...]
        i_pre = jnp.concatenate(
            [jnp.dot(ub[:, :half], wx_ref[0], preferred_element_type=F32),
             jnp.dot(ub[:, half:], wx_ref[1], preferred_element_type=F32)], axis=-1) + bx_ref[...]
        a = jnp.exp(neg_c_sp * _sigmoid(r_pre))
        a_scr[b] = a
        b_scr[b] = jnp.sqrt(1.0 - a * a) * (_sigmoid(i_pre) * u)
        return carry

    lax.fori_loop(0, bsz, gates, 0)

    def scan_rows(j, hs):
        idx = (tl - 1 - j) if reverse else j
        new = []
        for b in range(bsz):
            h = a_scr[b, pl.ds(idx, 1), :] * hs[b] + b_scr[b, pl.ds(idx, 1), :]
            o_ref[b, pl.ds(idx, 1), :] = h
            new.append(h)
        return tuple(new)

    hs = tuple(h_scr[b, 0:1, :] for b in range(bsz))
    hs = lax.fori_loop(0, tl, scan_rows, hs)
    for b in range(bsz):
        h_scr[b, 0:1, :] = hs[b]


def _lru(p, conv_w, conv_b, wa, ba, wx, bx, lam, *, n_ctx, reverse, tl=256):
    bsz, l, _ = p.shape
    width = GROUP
    n_tiles = l // tl
    n_ctx_tiles = n_ctx // tl
    hb = tl // HALO
    n_hblocks = l // HALO

    def tile_of(s):
        return _scan_tile(s, n_ctx_tiles, n_tiles, reverse)

    vec = pl.BlockSpec((1, width), lambda s: (0, 0))
    mat = pl.BlockSpec((2, width // 2, width // 2), lambda s: (0, 0, 0))
    return pl.pallas_call(
        functools.partial(_lru_body, tl=tl, n_ctx_tiles=n_ctx_tiles, n_tiles=n_tiles, reverse=reverse),
        grid=(n_tiles,),
        in_specs=[pl.BlockSpec((bsz, tl, width), lambda s: (0, tile_of(s), 0)),
                  pl.BlockSpec((bsz, HALO, width), lambda s: (0, jnp.maximum(tile_of(s) * hb - 1, 0), 0)),
                  pl.BlockSpec((bsz, HALO, width),
                               lambda s: (0, jnp.minimum((tile_of(s) + 1) * hb, n_hblocks - 1), 0)),
                  pl.BlockSpec((CONV_W, width), lambda s: (0, 0)), vec, mat, vec, mat, vec, vec],
        out_specs=pl.BlockSpec((bsz, tl, width), lambda s: (0, tile_of(s), 0)),
        out_shape=jax.ShapeDtypeStruct((bsz, l, width), F32),
        scratch_shapes=[pltpu.VMEM((tl + 2 * HALO, width), F32),
                        pltpu.VMEM((bsz, tl, width), F32),
                        pltpu.VMEM((bsz, tl, width), F32),
                        pltpu.VMEM((bsz, SUBLANES, width), F32)],
        compiler_params=_cparams("arbitrary"),
        name="lru_rev" if reverse else "lru_fwd",
    )(p, p, p, conv_w, conv_b.reshape(1, width), wa, ba.reshape(1, width), wx, bx.reshape(1, width),
      lam.reshape(1, width))


def _blockdiag_halves(w):
    nb, n, _ = w.shape
    per = nb // 2
    w = w.reshape(2, per, n, n)
    eye = jnp.eye(per, dtype=w.dtype)
    out = w[:, :, :, None, :] * eye[None, :, None, :, None]
    return out.reshape(2, per * n, per * n)


def _ret_body(lg_ref, q_ref, k_ref, v_ref, cos_ref, sin_ref, o_ref, dec_scr, qd_scr, kd_scr, cd_scr, st_scr,
              *, reverse, dk):
    c = RET_CHUNK
    step = pl.program_id(1)

    @pl.when(step == 0)
    def _():
        st_scr[...] = jnp.zeros_like(st_scr)
        i = lax.broadcasted_iota(jnp.int32, (c, c), 0)
        j = lax.broadcasted_iota(jnp.int32, (c, c), 1)
        diff = (j - i) if reverse else (i - j)
        pos = ((c - 1 - i) if reverse else i).astype(F32)
        dpos = jnp.maximum(diff, 0).astype(F32)
        for h in range(RET_HEADS):
            lg = lg_ref[h]
            dec_scr[h] = jnp.where(diff >= 0, jnp.exp(lg * dpos), 0.0)
            qd_scr[h] = jnp.exp(lg * (pos + 1.0))
            kd_scr[h] = jnp.exp(lg * (c - 1.0 - pos))
            cd_scr[h] = jnp.exp(jnp.full((SUBLANES, LANES), c, F32) * lg)

    def rope(x):
        width = x.shape[-1]
        lane = lax.broadcasted_iota(jnp.int32, x.shape, 1)
        first = (lane & (dk - 1)) < dk // 2
        rot = jnp.where(first, pltpu.roll(x, width - dk // 2, axis=1), pltpu.roll(x, dk // 2, axis=1))
        return x * cos_ref[...] + rot * sin_ref[...]

    q = rope(q_ref[...])
    k = rope(k_ref[...]) * (dk ** -0.5)
    v = v_ref[...]
    for h in range(RET_HEADS):
        qh = q[:, h * dk:(h + 1) * dk]
        kh = k[:, h * dk:(h + 1) * dk]
        vh = v[:, h * HEAD_DV:(h + 1) * HEAD_DV]
        s = _dot_nt(qh, kh) * dec_scr[h]
        state = st_scr[h]
        o = _dot(s, vh) + _dot(qh, state) * qd_scr[h]
        kdh = kh * kd_scr[h][:, :dk]
        st_scr[h] = state * cd_scr[h][0:1, :] + _dot(kdh.T, vh)
        o_ref[:, h * HEAD_DV:(h + 1) * HEAD_DV] = o


def _retention(p, cos_t, sin_t, log_gamma, *, n_ctx, reverse):
    bsz, l, _ = p.shape
    c = RET_CHUNK
    qk_w = cos_t.shape[1]
    dk = qk_w // RET_HEADS
    n_chunks = l // c
    n_ctx_chunks = n_ctx // c

    def ch(s):
        return _scan_tile(s, n_ctx_chunks, n_chunks, reverse)

    q_col = (2 * GROUP) // qk_w
    return pl.pallas_call(
        functools.partial(_ret_body, reverse=reverse, dk=dk),
        grid=(bsz, n_chunks),
        in_specs=[pl.BlockSpec(memory_space=pltpu.SMEM),
                  pl.BlockSpec((None, c, qk_w), lambda b, s: (b, ch(s), q_col)),
                  pl.BlockSpec((None, c, qk_w), lambda b, s: (b, ch(s), q_col + 1)),
                  pl.BlockSpec((None, c, GROUP), lambda b, s: (b, ch(s), 3)),
                  pl.BlockSpec((c, qk_w), lambda b, s: (ch(s), 0)),
                  pl.BlockSpec((c, qk_w), lambda b, s: (ch(s), 0))],
        out_specs=pl.BlockSpec((None, c, GROUP), lambda b, s: (b, ch(s), 0)),
        out_shape=jax.ShapeDtypeStruct((bsz, l, GROUP), F32),
        scratch_shapes=[pltpu.VMEM((RET_HEADS, c, c), F32),
                        pltpu.VMEM((RET_HEADS, c, HEAD_DV), F32),
                        pltpu.VMEM((RET_HEADS, c, HEAD_DV), F32),
                        pltpu.VMEM((RET_HEADS, SUBLANES, LANES), F32),
                        pltpu.VMEM((RET_HEADS, dk, HEAD_DV), F32)],
        compiler_params=_cparams("parallel", "arbitrary"),
        name="ret_rev" if reverse else "ret_fwd",
    )(log_gamma, p, p, p, cos_t, sin_t)


def _rope_tables(n_ctx, seq, dk):
    n_freq = dk // 4
    inv = jnp.power(ROPE_BASE, -jnp.arange(n_freq, dtype=F32) / n_freq)
    rows = seq // GRID_W
    r = jnp.arange(rows, dtype=F32)
    col = jnp.arange(GRID_W, dtype=F32)
    row_ang = jnp.broadcast_to(r[:, None, None] * inv, (rows, GRID_W, n_freq))
    col_ang = jnp.broadcast_to(col[None, :, None] * inv, (rows, GRID_W, n_freq))
    ang = jnp.concatenate([row_ang, col_ang], axis=-1).reshape(rows * GRID_W, 2 * n_freq)
    cos, sin = jnp.cos(ang), jnp.sin(ang)
    cos_h = jnp.concatenate([cos, cos], axis=-1)
    sin_h = jnp.concatenate([-sin, sin], axis=-1)
    cos_t = jnp.concatenate([jnp.ones((n_ctx, dk), F32), cos_h], axis=0)
    sin_t = jnp.concatenate([jnp.zeros((n_ctx, dk), F32), sin_h], axis=0)
    return jnp.tile(cos_t, (1, RET_HEADS)), jnp.tile(sin_t, (1, RET_HEADS))


_GLA_LEVELS = (32, 16, 8, 4, 2, 1)


def _gla_matrices(reverse):
    c = LIN_CHUNK
    p = np.arange(c)
    mats = [(p[None, :] <= p[:, None]), (p[None, :] > p[:, None])]
    for s in _GLA_LEVELS:
        bs = (p // s) * s
        right = ((p // s) % 2 == 1)
        m_r = (p[None, :] > bs[:, None]) & (p[None, :] <= p[:, None])
        m_l = (p[None, :] > p[:, None]) & (p[None, :] <= (bs + s)[:, None])
        mats.append(np.where(right[:, None], m_r, m_l))
    m = np.stack(mats).astype(np.float32)
    if reverse:
        m = m[:, ::-1, ::-1]
    return m.reshape(len(mats) * c, c)


def _gla_body(m_ref, lb_ref, q_ref, f_ref, v_ref, o_ref, st_scr, *, reverse, layer):
    c = LIN_CHUNK
    step = pl.program_id(1)

    @pl.when(step == 0)
    def _():
        st_scr[...] = jnp.zeros_like(st_scr)

    lg = lb_ref[...]
    e = jnp.exp(lg - jnp.max(lg, axis=0, keepdims=True))
    sm = e / jnp.sum(e, axis=0, keepdims=True)
    lb = jnp.sum(sm[0:layer + 1, :], axis=0, keepdims=True) - sm[0:1, :]

    f = lb + (1.0 - lb) * _sigmoid(f_ref[...])
    k = 1.0 - f
    q = _silu(q_ref[...])
    v = v_ref[...]
    ex = _dot_sel(m_ref[...], jnp.log(f))

    last = 0 if reverse else c - 1
    a_last = jnp.exp(ex[last:last + 1, :])
    qa = q * jnp.exp(ex[0:c, :])
    kl = k * jnp.exp(ex[c:2 * c, :])

    ri = lax.broadcasted_iota(jnp.int32, (c, 1), 0)
    i2 = lax.broadcasted_iota(jnp.int32, (c, c), 0)
    j2 = lax.broadcasted_iota(jnp.int32, (c, c), 1)
    if reverse:
        ri, i2, j2 = c - 1 - ri, c - 1 - i2, c - 1 - j2

    nh = q.shape[-1] // HEAD_DV
    scores = [jnp.where(i2 == j2, _dot_nt(q[:, h * HEAD_DV:(h + 1) * HEAD_DV], k[:, h * HEAD_DV:(h + 1) * HEAD_DV]), 0.0)
              for h in range(nh)]
    for lvl, s in enumerate(_GLA_LEVELS):
        sh = s.bit_length() - 1
        right = ((ri >> sh) & 1) == 1
        x = (jnp.where(right, q, k) * jnp.exp(ex[(2 + lvl) * c:(3 + lvl) * c, :])).astype(BF16)
        bi, bj = i2 >> sh, j2 >> sh
        valid = bj == jnp.where((bi & 1) == 1, bi - 1, -1)
        for h in range(nh):
            xh = x[:, h * HEAD_DV:(h + 1) * HEAD_DV]
            scores[h] = scores[h] + jnp.where(valid, _dot_nt(xh, xh), 0.0)

    for h in range(nh):
        sl = slice(h * HEAD_DV, (h + 1) * HEAD_DV)
        state_t = st_scr[h]
        o_ref[:, sl] = _dot(scores[h], v[:, sl]) + _dot_nt(qa[:, sl], state_t)
        st_scr[h] = state_t * a_last[:, sl] + _dot(v[:, sl].T, kl[:, sl])


def _gla(p, lb_logits, *, n_ctx, reverse, layer):
    bsz, l, _ = p.shape
    c = LIN_CHUNK
    n_chunks = l // c
    n_ctx_chunks = n_ctx // c
    m = jnp.asarray(_gla_matrices(reverse), BF16)

    def ch(s):
        return _scan_tile(s, n_ctx_chunks, n_chunks, reverse)

    f_col = 2 if reverse else 1
    return pl.pallas_call(
        functools.partial(_gla_body, reverse=reverse, layer=layer),
        grid=(bsz, n_chunks),
        in_specs=[pl.BlockSpec(m.shape, lambda b, s: (0, 0)),
                  pl.BlockSpec(lb_logits.shape, lambda b, s: (0, 0)),
                  pl.BlockSpec((None, c, GROUP), lambda b, s: (b, ch(s), 0)),
                  pl.BlockSpec((None, c, GROUP), lambda b, s: (b, ch(s), f_col)),
                  pl.BlockSpec((None, c, GROUP), lambda b, s: (b, ch(s), 3))],
        out_specs=pl.BlockSpec((None, c, GROUP), lambda b, s: (b, ch(s), 0)),
        out_shape=jax.ShapeDtypeStruct((bsz, l, GROUP), F32),
        scratch_shapes=[pltpu.VMEM((LIN_HEADS, HEAD_DV, HEAD_DV), F32)],
        compiler_params=_cparams("parallel", "arbitrary"),
        name="gla_rev" if reverse else "gla_fwd",
    )(m, lb_logits, p, p, p)


def _gdn_body(cs_ref, cw_ref, arow_ref, dtrow_ref, t_ref,
              q_ref, qp_ref, qn_ref, k_ref, kp_ref, kn_ref, v_ref, vp_ref, vn_ref,
              o_ref, ext_scr, st_scr, *, reverse, n_ctx_chunks, n_chunks, beta_col, g_col):
    c = LIN_CHUNK
    nh = LIN_HEADS
    n = nh * c
    step = pl.program_id(1)
    chunk = _scan_tile(step, n_ctx_chunks, n_chunks, reverse)

    @pl.when(step == 0)
    def _():
        st_scr[...] = jnp.zeros_like(st_scr)

    seg_first = jnp.logical_or(chunk == 0, chunk == n_ctx_chunks)
    seg_last = jnp.logical_or(chunk == n_ctx_chunks - 1, chunk == n_chunks - 1)
    keep_prev = jnp.where(seg_first, 0.0, 1.0)
    keep_next = jnp.where(seg_last, 0.0, 1.0)

    def conv_silu(x_ref, xp_ref, xn_ref, col):
        ext_scr[0:HALO, :] = xp_ref[...] * keep_prev
        ext_scr[HALO:HALO + c, :] = x_ref[...]
        ext_scr[HALO + c:, :] = xn_ref[...] * keep_next
        u = cw_ref[0:1, col * GROUP:(col + 1) * GROUP] * ext_scr[pl.ds(HALO - CONV_LEFT, c), :]
        for t in range(1, CONV_W):
            u = u + cw_ref[t:t + 1, col * GROUP:(col + 1) * GROUP] * ext_scr[pl.ds(HALO - CONV_LEFT + t, c), :]
        return _silu(u)

    def stack(x):
        return jnp.concatenate([x[:, h * HEAD_DV:(h + 1) * HEAD_DV] for h in range(nh)], axis=0)

    def l2n(x):
        return x * lax.rsqrt(jnp.sum(x * x, axis=-1, keepdims=True) + EPS)

    q = l2n(stack(conv_silu(q_ref, qp_ref, qn_ref, 0))) * (HEAD_DV ** -0.5)
    k = l2n(stack(conv_silu(k_ref, kp_ref, kn_ref, 1)))
    v = stack(conv_silu(v_ref, vp_ref, vn_ref, 2))

    tail = t_ref[...]
    beta_all = _sigmoid(tail)
    g_all = -jnp.exp(arow_ref[...]) * _softplus(tail + dtrow_ref[...])
    beta = jnp.concatenate([jnp.broadcast_to(beta_all[:, beta_col + h:beta_col + h + 1], (c, HEAD_DV))
                            for h in range(nh)], axis=0)
    g = jnp.concatenate([jnp.broadcast_to(g_all[:, g_col + h:g_col + h + 1], (c, HEAD_DV))
                         for h in range(nh)], axis=0)
    gc = _dot_sel(cs_ref[...], g)
    last = 0 if reverse else c - 1
    gc_last = jnp.concatenate([jnp.broadcast_to(gc[h * c + last:h * c + last + 1, :], (c, HEAD_DV))
                               for h in range(nh)], axis=0)

    i2 = lax.broadcasted_iota(jnp.int32, (n, n), 0)
    j2 = lax.broadcasted_iota(jnp.int32, (n, n), 1)
    head_lo = (i2 >> 6) << 6
    if reverse:
        mask = jnp.where(j2 < head_lo + c, j2, -1) >= i2
    else:
        mask = jnp.where(j2 >= head_lo, j2, n) <= i2
    same32 = (i2 >> 5) == (j2 >> 5)
    same16 = (i2 >> 4) == (j2 >> 4)

    gc_row = jnp.transpose(gc)[0:1, :]
    diff = jnp.concatenate([gc, gc], axis=1) - gc_row
    gam = jnp.where(mask, jnp.exp(jnp.where(mask, diff, 0.0)), 0.0)

    kb = k * beta
    mm = jnp.where(i2 == j2, 0.0, _dot_nt(kb, k) * gam)
    qk = _dot_nt(q, k) * gam
    egc = jnp.exp(gc)
    rhs = jnp.concatenate([v * beta, kb * egc], axis=1)

    eye = jnp.where(i2 == j2, 1.0, 0.0)
    md = jnp.where(same16, mm, 0.0)
    m2 = _dot3(md, md)
    m4 = _dot3(m2, m2)
    m8 = _dot3(m4, m4)
    t = eye - md
    t = t + _dot3(t, m2)
    t = t + _dot3(t, m4)
    t = t + _dot3(t, m8)
    c32 = jnp.where(same32, mm - md, 0.0)
    t = t - _dot3(t, _dot3(c32, t))
    c64 = jnp.where(same32, 0.0, mm)
    t = t - _dot3(t, _dot3(c64, t))
    sol = _dot3(t, rhs)
    u = sol[:, :HEAD_DV]
    w = sol[:, HEAD_DV:]

    qd = q * egc
    kd = k * jnp.exp(gc_last - gc)
    g_last = jnp.exp(gc_last)
    v_new = jnp.concatenate([u[h * c:(h + 1) * c] - _dot(w[h * c:(h + 1) * c], st_scr[h]) for h in range(nh)],
                            axis=0)
    o_intra = _dot(qk, v_new)
    for h in range(nh):
        rows = slice(h * c, (h + 1) * c)
        state = st_scr[h]
        o_ref[:, h * HEAD_DV:(h + 1) * HEAD_DV] = o_intra[rows] + _dot(qd[rows], state)
        st_scr[h] = state * g_last[h * c:h * c + 1, :] + _dot(kd[rows].T, v_new[rows])


def _gdn_cumsum_matrix(reverse):
    c, nh = LIN_CHUNK, LIN_HEADS
    p = np.arange(c)
    tri = (p[None, :] >= p[:, None]) if reverse else (p[None, :] <= p[:, None])
    return np.kron(np.eye(nh), tri.astype(np.float32))


def _gdn(p, tail, conv_w, a_row, dt_row, *, n_ctx, reverse, direction):
    bsz, l, _ = p.shape
    c = LIN_CHUNK
    n_chunks = l // c
    n_ctx_chunks = n_ctx // c
    hb = c // HALO
    n_hblocks = l // HALO
    cs = jnp.asarray(_gdn_cumsum_matrix(reverse), BF16)

    def ch(s):
        return _scan_tile(s, n_ctx_chunks, n_chunks, reverse)

    def triple(col):
        return [pl.BlockSpec((None, c, GROUP), lambda b, s: (b, ch(s), col)),
                pl.BlockSpec((None, HALO, GROUP), lambda b, s: (b, jnp.maximum(ch(s) * hb - 1, 0), col)),
                pl.BlockSpec((None, HALO, GROUP),
                             lambda b, s: (b, jnp.minimum((ch(s) + 1) * hb, n_hblocks - 1), col))]

    const2 = lambda b, s: (0, 0)
    return pl.pallas_call(
        functools.partial(_gdn_body, reverse=reverse, n_ctx_chunks=n_ctx_chunks, n_chunks=n_chunks,
                          beta_col=direction * LIN_HEADS, g_col=(2 + direction) * LIN_HEADS),
        grid=(bsz, n_chunks),
        in_specs=[pl.BlockSpec(cs.shape, const2), pl.BlockSpec(conv_w.shape, const2),
                  pl.BlockSpec((1, LANES), const2), pl.BlockSpec((1, LANES), const2),
                  pl.BlockSpec((None, c, LANES), lambda b, s: (b, ch(s), 0))]
                 + triple(5) + triple(6) + triple(7),
        out_specs=pl.BlockSpec((None, c, GROUP), lambda b, s: (b, ch(s), 0)),
        out_shape=jax.ShapeDtypeStruct((bsz, l, GROUP), F32),
        scratch_shapes=[pltpu.VMEM((c + 2 * HALO, GROUP), F32),
                        pltpu.VMEM((LIN_HEADS, HEAD_DV, HEAD_DV), F32)],
        compiler_params=_cparams("parallel", "arbitrary"),
        name="gdn_rev" if reverse else "gdn_fwd",
    )(cs, conv_w, a_row, dt_row, tail, *([p] * 9))


def _post_body(x_ref, af_ref, ar_ref, ga_ref, bf_ref, br_ref, gb_ref, w_ref, ml_ref, mc_ref, o_ref,
               *, even, tm, n_ctx):
    t = pl.program_id(1)
    a = af_ref[...] + ar_ref[...]
    bsum = bf_ref[...] + br_ref[...]
    if even:
        ya = a * _gelu_tanh(ga_ref[...])
        yb = _head_norm(bsum, True) * _silu(gb_ref[...])
    else:
        ya = _head_norm(a, False) * _silu(ga_ref[...])
        yb = _head_norm(bsum, False) * _silu(gb_ref[...])
    half = w_ref.shape[0] // 2
    mix = jnp.dot(ya.astype(BF16), w_ref[0:half, :], preferred_element_type=F32)
    mix = mix + jnp.dot(yb.astype(BF16), w_ref[half:, :], preferred_element_type=F32)
    o_ref[...] = x_ref[...] + _row_select(t, tm, n_ctx, mc_ref, ml_ref, 2) * mix


def _post(x, mod, a_f, a_r, b_f, b_r, p, gate_a_col, gate_b_col, w_out, *, even, tm, n_ctx):
    bsz, l, d = x.shape
    ctx_row = bsz
    tok = lambda b, t: (b, t, 0)
    grp = pl.BlockSpec((None, tm, GROUP), tok)
    return pl.pallas_call(
        functools.partial(_post_body, even=even, tm=tm, n_ctx=n_ctx),
        grid=(bsz, l // tm),
        in_specs=[pl.BlockSpec((None, tm, d), tok), grp, grp,
                  pl.BlockSpec((None, tm, GROUP), lambda b, t: (b, t, gate_a_col)), grp, grp,
                  pl.BlockSpec((None, tm, GROUP), lambda b, t: (b, t, gate_b_col)),
                  pl.BlockSpec(w_out.shape, lambda b, t: (0, 0)),
                  pl.BlockSpec((None, 6, d), lambda b, t: (b, 0, 0)),
                  pl.BlockSpec((None, 6, d), lambda b, t: (ctx_row, 0, 0))],
        out_specs=pl.BlockSpec((None, tm, d), tok),
        out_shape=jax.ShapeDtypeStruct((bsz, l, d), F32),
        compiler_params=_cparams("parallel", "parallel"),
        name="post",
    )(x, a_f, a_r, p, b_f, b_r, p, w_out, mod, mod)


def _mlp_body(x_ref, g_ref, ml_ref, mc_ref, w1_ref, w2_ref, fg_ref, o_ref, h_scr, acc_scr, *, tm, n_ctx, final):
    t = pl.program_id(1)
    f = pl.program_id(2)

    @pl.when(f == 0)
    def _():
        y = _rms(x_ref[...], g_ref[...])
        shift = _row_select(t, tm, n_ctx, mc_ref, ml_ref, 3)
        scale = _row_select(t, tm, n_ctx, mc_ref, ml_ref, 4)
        h_scr[...] = (y * (1.0 + scale) + shift).astype(BF16)
        acc_scr[...] = jnp.zeros_like(acc_scr)

    hid = jnp.maximum(jnp.dot(h_scr[...], w1_ref[...], preferred_element_type=F32), 0.0)
    acc_scr[...] += jnp.dot((hid * hid).astype(BF16), w2_ref[...], preferred_element_type=F32)

    @pl.when(f == pl.num_programs(2) - 1)
    def _():
        y = x_ref[...] + _row_select(t, tm, n_ctx, mc_ref, ml_ref, 5) * acc_scr[...]
        if final:
            y = _rms(y, fg_ref[...])
        o_ref[...] = y


def _mlp(x, mod, g, w1, w2, final_g, *, tm, tf, n_ctx, final):
    bsz, l, d = x.shape
    dff = w1.shape[1]
    ctx_row = bsz
    tok = lambda b, t, f: (b, t, 0)
    return pl.pallas_call(
        functools.partial(_mlp_body, tm=tm, n_ctx=n_ctx, final=final),
        grid=(bsz, l // tm, dff // tf),
        in_specs=[pl.BlockSpec((None, tm, d), tok),
                  pl.BlockSpec((1, d), lambda b, t, f: (0, 0)),
                  pl.BlockSpec((None, 6, d), lambda b, t, f: (b, 0, 0)),
                  pl.BlockSpec((None, 6, d), lambda b, t, f: (ctx_row, 0, 0)),
                  pl.BlockSpec((d, tf), lambda b, t, f: (0, f)),
                  pl.BlockSpec((tf, d), lambda b, t, f: (f, 0)),
                  pl.BlockSpec((1, d), lambda b, t, f: (0, 0))],
        out_specs=pl.BlockSpec((None, tm, d), tok),
        out_shape=jax.ShapeDtypeStruct((bsz, l, d), F32),
        scratch_shapes=[pltpu.VMEM((tm, d), BF16), pltpu.VMEM((tm, d), F32)],
        compiler_params=_cparams("parallel", "parallel", "arbitrary"),
        name="mlp",
    )(x, g.reshape(1, d), mod, mod, w1, w2, final_g.reshape(1, d))


def _token_tile(l):
    for div in (4, 8, 16, 17, 34):
        if l % div == 0 and (l // div) % 16 == 0 and l // div <= 1088:
            return l // div
    return l


def kernel(x, c, ctx, c_ctx, ada_w, ada_b, norm1_g, norm2_g, mix_w_out, mlp_w1, mlp_w2, ev_w_in, lru_conv_w,
           lru_conv_b, lru_wa, lru_ba, lru_wx, lru_bx, lru_lambda, ret_log_gamma, od_w_in, hg_lb_logits,
           gdn_conv_w, gdn_a_log, gdn_dt_bias, final_g):
    bsz, seq, d = x.shape
    n_ctx = ctx.shape[1]
    depth = ada_w.shape[0]
    l = n_ctx + seq
    tm = _token_tile(l)
    tf = 512

    mods = _mods(c, c_ctx, ada_w, ada_b)
    ret_dk = (ev_w_in.shape[2] - 4 * GROUP) // (2 * RET_HEADS)
    cos_t, sin_t = _rope_tables(n_ctx, seq, ret_dk)

    xs = jnp.concatenate([ctx, x], axis=1)
    for layer in range(depth):
        mod = mods[layer]
        w_out = mix_w_out[layer].astype(BF16)
        if layer % 2 == 0:
            e = layer // 2
            (p,) = _proj(xs, mod, norm1_g[layer], ev_w_in[e].astype(BF16), None, tm=tm, n_ctx=n_ctx)
            mix = []
            for dr, rev in enumerate((False, True)):
                mix.append((
                    _lru(p, lru_conv_w[e], lru_conv_b[e], _blockdiag_halves(lru_wa[e, dr]).astype(BF16),
                         lru_ba[e, dr], _blockdiag_halves(lru_wx[e, dr]).astype(BF16), lru_bx[e, dr],
                         lru_lambda[e, dr], n_ctx=n_ctx, reverse=rev),
                    _retention(p, cos_t, sin_t, ret_log_gamma[e, dr], n_ctx=n_ctx, reverse=rev)))
            gate_cols = (1, 4)
        else:
            o = layer // 2
            n_main = 9 * GROUP
            w_in = od_w_in[o]
            w_tail = jnp.pad(w_in[:, n_main:], ((0, 0), (0, LANES - (w_in.shape[1] - n_main)))).astype(BF16)
            p, tail = _proj(xs, mod, norm1_g[layer], w_in[:, :n_main].astype(BF16), w_tail, tm=tm, n_ctx=n_ctx)
            mix = []
            for dr, rev in enumerate((False, True)):
                g_col = (2 + dr) * LIN_HEADS
                a_row = jnp.zeros((1, LANES), F32).at[0, g_col:g_col + LIN_HEADS].set(gdn_a_log[o, dr])
                dt_row = jnp.zeros((1, LANES), F32).at[0, g_col:g_col + LIN_HEADS].set(gdn_dt_bias[o, dr])
                mix.append((
                    _gla(p, hg_lb_logits[dr], n_ctx=n_ctx, reverse=rev, layer=o),
                    _gdn(p, tail, gdn_conv_w[o], a_row, dt_row, n_ctx=n_ctx, reverse=rev, direction=dr)))
            gate_cols = (4, 8)
        xs = _post(xs, mod, mix[0][0], mix[1][0], mix[0][1], mix[1][1], p, gate_cols[0], gate_cols[1], w_out,
                   even=layer % 2 == 0, tm=tm // 2, n_ctx=n_ctx)
        xs = _mlp(xs, mod, norm2_g[layer], mlp_w1[layer].astype(BF16), mlp_w2[layer].astype(BF16), final_g,
                  tm=tm, tf=tf, n_ctx=n_ctx, final=layer == depth - 1)
    return xs[:, n_ctx:]
```

```python
import functools

import numpy as np
import jax
import jax.numpy as jnp
from jax import lax
from jax.experimental import pallas as pl
from jax.experimental.pallas import tpu as pltpu

F32 = jnp.float32
BF16 = jnp.bfloat16

EPS = 1e-6
GRID_W = 64
CONV_W = 4
CONV_LEFT = CONV_W // 2
LRU_BLOCKS = 8
LRU_C = 8.0
RET_HEADS = 4
RET_CHUNK = 256
ROPE_BASE = 10000.0
LIN_HEADS = 4
GLA_CHUNK = 128
LIN_CHUNK = 64
HEAD_DV = 128

LANES = 128
SUBLANES = 8
V7X_VMEM_BYTES = 64 * 1024 * 1024
VMEM_LIMIT = (V7X_VMEM_BYTES * 3) // 4
GROUP = 512

HALO = SUBLANES


def _cparams(*sem):
    return pltpu.CompilerParams(dimension_semantics=sem, vmem_limit_bytes=VMEM_LIMIT)


def _sigmoid(x):
    return 1.0 / (1.0 + jnp.exp(-x))


def _silu(x):
    return x * _sigmoid(x)


def _softplus(x):
    return jnp.maximum(x, 0.0) + jnp.log(1.0 + jnp.exp(-jnp.abs(x)))


def _gelu_tanh(x):
    return 0.5 * x * (1.0 + jnp.tanh(0.7978845608028654 * (x + 0.044715 * x * x * x)))


def _dot(a, b):
    return jnp.dot(a.astype(BF16), b.astype(BF16), preferred_element_type=F32)


def _dot_nt(a, b):
    return lax.dot_general(a.astype(BF16), b.astype(BF16), (((1,), (1,)), ((), ())),
                           preferred_element_type=F32)


def _split_bf16(x):
    hi = x.astype(BF16)
    lo = (x - hi.astype(F32)).astype(BF16)
    return hi, lo


def _dot_sel(m2, x):
    hi, lo = _split_bf16(x)
    return jnp.dot(m2, jnp.concatenate([hi, lo], axis=0), preferred_element_type=F32)


def _doubled(m):
    return jnp.asarray(np.concatenate([m, m], axis=1), BF16)


def _dot3(a, b):
    ah, al = _split_bf16(a)
    bh, bl = _split_bf16(b)
    return (jnp.dot(ah, bh, preferred_element_type=F32) + jnp.dot(ah, bl, preferred_element_type=F32)
            + jnp.dot(al, bh, preferred_element_type=F32))


def _row_select(t, tm, n_ctx, mc_ref, ml_ref, idx):
    rows = t * tm + lax.broadcasted_iota(jnp.int32, (tm, 1), 0)
    return jnp.where(rows < n_ctx, mc_ref[idx:idx + 1, :], ml_ref[idx:idx + 1, :])


def _rms(x, g):
    return x * lax.rsqrt(jnp.mean(x * x, axis=-1, keepdims=True) + EPS) * g


def _head_norm(o, center):
    outs = []
    for h in range(o.shape[-1] // HEAD_DV):
        oh = o[:, h * HEAD_DV:(h + 1) * HEAD_DV]
        if center:
            oh = oh - jnp.mean(oh, axis=-1, keepdims=True)
        outs.append(oh * lax.rsqrt(jnp.mean(oh * oh, axis=-1, keepdims=True) + EPS))
    return jnp.concatenate(outs, axis=-1)


def _scan_tile(step, n_first, n_total, reverse):
    if not reverse:
        return step
    return jnp.where(step < n_first, n_first - 1 - step, n_total - 1 - (step - n_first))


def _mods_body(s_ref, w_ref, b_ref, o_ref):
    s = _silu(s_ref[...])
    o_ref[...] = _dot3(s, w_ref[...]) + b_ref[...]


def _mods(c, c_ctx, ada_w, ada_b):
    depth, d, n6 = ada_w.shape
    bsz = c.shape[0]
    rows = -(-(bsz + 1) // SUBLANES) * SUBLANES
    s = jnp.zeros((rows, d), F32).at[:bsz].set(c).at[bsz].set(c_ctx)
    tn = 1024
    out = pl.pallas_call(
        _mods_body,
        grid=(depth, n6 // tn),
        in_specs=[pl.BlockSpec((rows, d), lambda l, n: (0, 0)),
                  pl.BlockSpec((None, d, tn), lambda l, n: (l, 0, n)),
                  pl.BlockSpec((None, 1, tn), lambda l, n: (l, 0, n))],
        out_specs=pl.BlockSpec((None, rows, tn), lambda l, n: (l, 0, n)),
        out_shape=jax.ShapeDtypeStruct((depth, rows, n6), F32),
        compiler_params=_cparams("parallel", "parallel"),
        name="mods",
    )(s, ada_w, ada_b.reshape(depth, 1, n6))
    return out.reshape(depth, rows, 6, d)


def _proj_body(*refs, tm, n_ctx, has_tail):
    if has_tail:
        x_ref, ml_ref, mc_ref, g_ref, w_ref, wt_ref, o_ref, ot_ref, h_scr = refs
    else:
        x_ref, ml_ref, mc_ref, g_ref, w_ref, o_ref, h_scr = refs
    t = pl.program_id(1)

    @pl.when(pl.program_id(2) == 0)
    def _():
        y = _rms(x_ref[...], g_ref[...])
        shift = _row_select(t, tm, n_ctx, mc_ref, ml_ref, 0)
        scale = _row_select(t, tm, n_ctx, mc_ref, ml_ref, 1)
        h_scr[...] = (y * (1.0 + scale) + shift).astype(BF16)
        if has_tail:
            ot_ref[...] = jnp.dot(h_scr[...], wt_ref[...], preferred_element_type=F32)

    o_ref[...] = jnp.dot(h_scr[...], w_ref[...], preferred_element_type=F32)


def _proj(x, mod, g, w, w_tail, *, tm, n_ctx):
    bsz, l, d = x.shape
    ng = w.shape[1] // GROUP
    ctx_row = bsz
    in_specs = [pl.BlockSpec((None, tm, d), lambda b, t, n: (b, t, 0)),
                pl.BlockSpec((None, 6, d), lambda b, t, n: (b, 0, 0)),
                pl.BlockSpec((None, 6, d), lambda b, t, n: (ctx_row, 0, 0)),
                pl.BlockSpec((1, d), lambda b, t, n: (0, 0)),
                pl.BlockSpec((d, GROUP), lambda b, t, n: (0, n))]
    out_specs = [pl.BlockSpec((None, tm, GROUP), lambda b, t, n: (b, t, n))]
    out_shape = [jax.ShapeDtypeStruct((bsz, l, ng * GROUP), F32)]
    args = [x, mod, mod, g.reshape(1, d), w]
    if w_tail is not None:
        in_specs.append(pl.BlockSpec((d, LANES), lambda b, t, n: (0, 0)))
        out_specs.append(pl.BlockSpec((None, tm, LANES), lambda b, t, n: (b, t, 0)))
        out_shape.append(jax.ShapeDtypeStruct((bsz, l, LANES), F32))
        args.append(w_tail)
    outs = pl.pallas_call(
        functools.partial(_proj_body, tm=tm, n_ctx=n_ctx, has_tail=w_tail is not None),
        grid=(bsz, l // tm, ng),
        in_specs=in_specs, out_specs=out_specs, out_shape=out_shape,
        scratch_shapes=[pltpu.VMEM((tm, d), BF16)],
        compiler_params=_cparams("parallel", "parallel", "arbitrary"),
        name="proj",
    )(*args)
    return outs


def _lru_body(x_ref, xp_ref, xn_ref, cw_ref, cb_ref, wa_ref, ba_ref, wx_ref, bx_ref, lam_ref,
              o_ref, ext_scr, a_scr, b_scr, h_scr, *, tl, n_ctx_tiles, n_tiles, reverse):
    step = pl.program_id(0)
    tile = _scan_tile(step, n_ctx_tiles, n_tiles, reverse)
    bsz = x_ref.shape[0]
    width = x_ref.shape[2]
    half = width // 2
    seg_first = jnp.logical_or(tile == 0, tile == n_ctx_tiles)
    seg_last = jnp.logical_or(tile == n_ctx_tiles - 1, tile == n_tiles - 1)
    keep_prev = jnp.where(seg_first, 0.0, 1.0)
    keep_next = jnp.where(seg_last, 0.0, 1.0)
    neg_c_sp = -LRU_C * _softplus(-lam_ref[...])

    @pl.when(step == 0)
    def _():
        h_scr[...] = jnp.zeros_like(h_scr)

    def gates(b, carry):
        ext_scr[0:HALO, :] = xp_ref[b] * keep_prev
        ext_scr[HALO:HALO + tl, :] = x_ref[b]
        ext_scr[HALO + tl:, :] = xn_ref[b] * keep_next
        u = cb_ref[...] + cw_ref[0:1, :] * ext_scr[pl.ds(HALO - CONV_LEFT, tl), :]
        for k in range(1, CONV_W):
            u = u + cw_ref[k:k + 1, :] * ext_scr[pl.ds(HALO - CONV_LEFT + k, tl), :]
        ub = u.astype(BF16)
        r_pre = jnp.concatenate(
            [jnp.dot(ub[:, :half], wa_ref[0], preferred_element_type=F32),
             jnp.dot(ub[:, half:], wa_ref[1], preferred_element_type=F32)], axis=-1) + ba_ref[...]
        i_pre = jnp.concatenate(
            [jnp.dot(ub[:, :half], wx_ref[0], preferred_element_type=F32),
             jnp.dot(ub[:, half:], wx_ref[1], preferred_element_type=F32)], axis=-1) + bx_ref[...]
        a = jnp.exp(neg_c_sp * _sigmoid(r_pre))
        a_scr[b] = a
        b_scr[b] = jnp.sqrt(1.0 - a * a) * (_sigmoid(i_pre) * u)
        return carry

    lax.fori_loop(0, bsz, gates, 0)

    def scan_rows(j, hs):
        idx = (tl - 1 - j) if reverse else j
        new = []
        for b in range(bsz):
            h = a_scr[b, pl.ds(idx, 1), :] * hs[b] + b_scr[b, pl.ds(idx, 1), :]
            o_ref[b, pl.ds(idx, 1), :] = h
            new.append(h)
        return tuple(new)

    hs = tuple(h_scr[b, 0:1, :] for b in range(bsz))
    hs = lax.fori_loop(0, tl, scan_rows, hs)
    for b in range(bsz):
        h_scr[b, 0:1, :] = hs[b]


def _lru(p, conv_w, conv_b, wa, ba, wx, bx, lam, *, n_ctx, reverse, tl=256):
    bsz, l, _ = p.shape
    width = GROUP
    n_tiles = l // tl
    n_ctx_tiles = n_ctx // tl
    hb = tl // HALO
    n_hblocks = l // HALO

    def tile_of(s):
        return _scan_tile(s, n_ctx_tiles, n_tiles, reverse)

    vec = pl.BlockSpec((1, width), lambda s: (0, 0))
    mat = pl.BlockSpec((2, width // 2, width // 2), lambda s: (0, 0, 0))
    return pl.pallas_call(
        functools.partial(_lru_body, tl=tl, n_ctx_tiles=n_ctx_tiles, n_tiles=n_tiles, reverse=reverse),
        grid=(n_tiles,),
        in_specs=[pl.BlockSpec((bsz, tl, width), lambda s: (0, tile_of(s), 0)),
                  pl.BlockSpec((bsz, HALO, width), lambda s: (0, jnp.maximum(tile_of(s) * hb - 1, 0), 0)),
                  pl.BlockSpec((bsz, HALO, width),
                               lambda s: (0, jnp.minimum((tile_of(s) + 1) * hb, n_hblocks - 1), 0)),
                  pl.BlockSpec((CONV_W, width), lambda s: (0, 0)), vec, mat, vec, mat, vec, vec],
        out_specs=pl.BlockSpec((bsz, tl, width), lambda s: (0, tile_of(s), 0)),
        out_shape=jax.ShapeDtypeStruct((bsz, l, width), F32),
        scratch_shapes=[pltpu.VMEM((tl + 2 * HALO, width), F32),
                        pltpu.VMEM((bsz, tl, width), F32),
                        pltpu.VMEM((bsz, tl, width), F32),
                        pltpu.VMEM((bsz, SUBLANES, width), F32)],
        compiler_params=_cparams("arbitrary"),
        name="lru_rev" if reverse else "lru_fwd",
    )(p, p, p, conv_w, conv_b.reshape(1, width), wa, ba.reshape(1, width), wx, bx.reshape(1, width),
      lam.reshape(1, width))


def _blockdiag_halves(w):
    nb, n, _ = w.shape
    per = nb // 2
    w = w.reshape(2, per, n, n)
    eye = jnp.eye(per, dtype=w.dtype)
    out = w[:, :, :, None, :] * eye[None, :, None, :, None]
    return out.reshape(2, per * n, per * n)


def _ret_body(lg_ref, q_ref, k_ref, v_ref, cos_ref, sin_ref, o_ref, dec_scr, qd_scr, kd_scr, cd_scr, st_scr,
              *, reverse, dk, nb):
    c = RET_CHUNK
    step = pl.program_id(1)

    @pl.when(step == 0)
    def _():
        st_scr[...] = jnp.zeros_like(st_scr)
        i = lax.broadcasted_iota(jnp.int32, (c, c), 0)
        j = lax.broadcasted_iota(jnp.int32, (c, c), 1)
        diff = (j - i) if reverse else (i - j)
        row = lax.broadcasted_iota(jnp.int32, (c, HEAD_DV), 0)
        pos = ((c - 1 - row) if reverse else row).astype(F32)
        dpos = jnp.maximum(diff, 0).astype(F32)
        for h in range(RET_HEADS):
            lg = lg_ref[h]
            dec_scr[h] = jnp.where(diff >= 0, jnp.exp(lg * dpos), 0.0)
            qd_scr[h] = jnp.exp(lg * (pos + 1.0))
            kd_scr[h] = jnp.exp(lg * (c - 1.0 - pos))
            cd_scr[h] = jnp.exp(jnp.full((SUBLANES, LANES), c, F32) * lg)

    def rope(x):
        width = x.shape[-1]
        lane = lax.broadcasted_iota(jnp.int32, x.shape, 1)
        first = (lane & (dk - 1)) < dk // 2
        rot = jnp.where(first, pltpu.roll(x, width - dk // 2, axis=1), pltpu.roll(x, dk // 2, axis=1))
        return x * cos_ref[...] + rot * sin_ref[...]

    q = [rope(q_ref[bb]) for bb in range(nb)]
    k = [rope(k_ref[bb]) * (dk ** -0.5) for bb in range(nb)]
    pairs = [(bb, h) for bb in range(nb) for h in range(RET_HEADS)]
    qh = [q[bb][:, h * dk:(h + 1) * dk] for bb, h in pairs]
    kh = [k[bb][:, h * dk:(h + 1) * dk] for bb, h in pairs]
    vh = [v_ref[bb, :, h * HEAD_DV:(h + 1) * HEAD_DV] for bb, h in pairs]
    s = [_dot_nt(qh[i], kh[i]) * dec_scr[h] for i, (bb, h) in enumerate(pairs)]
    state = [st_scr[bb, h] for bb, h in pairs]
    o_x = [_dot(qh[i], state[i]) * qd_scr[h] for i, (bb, h) in enumerate(pairs)]
    o_in = [_dot(s[i], vh[i]) for i in range(len(pairs))]
    kv = [_dot((kh[i] * kd_scr[h][:, :dk]).T, vh[i]) for i, (bb, h) in enumerate(pairs)]
    for i, (bb, h) in enumerate(pairs):
        st_scr[bb, h] = state[i] * cd_scr[h][0:1, :] + kv[i]
        o_ref[bb, :, h * HEAD_DV:(h + 1) * HEAD_DV] = o_in[i] + o_x[i]


def _retention(p, cos_t, sin_t, log_gamma, *, n_ctx, reverse, nb):
    bsz, l, _ = p.shape
    c = RET_CHUNK
    qk_w = cos_t.shape[1]
    dk = qk_w // RET_HEADS
    n_chunks = l // c
    n_ctx_chunks = n_ctx // c

    def ch(s):
        return _scan_tile(s, n_ctx_chunks, n_chunks, reverse)

    q_col = (2 * GROUP) // qk_w
    return pl.pallas_call(
        functools.partial(_ret_body, reverse=reverse, dk=dk, nb=nb),
        grid=(bsz // nb, n_chunks),
        in_specs=[pl.BlockSpec(memory_space=pltpu.SMEM),
                  pl.BlockSpec((nb, c, qk_w), lambda b, s: (b, ch(s), q_col)),
                  pl.BlockSpec((nb, c, qk_w), lambda b, s: (b, ch(s), q_col + 1)),
                  pl.BlockSpec((nb, c, GROUP), lambda b, s: (b, ch(s), 3)),
                  pl.BlockSpec((c, qk_w), lambda b, s: (ch(s), 0)),
                  pl.BlockSpec((c, qk_w), lambda b, s: (ch(s), 0))],
        out_specs=pl.BlockSpec((nb, c, GROUP), lambda b, s: (b, ch(s), 0)),
        out_shape=jax.ShapeDtypeStruct((bsz, l, GROUP), F32),
        scratch_shapes=[pltpu.VMEM((RET_HEADS, c, c), F32),
                        pltpu.VMEM((RET_HEADS, c, HEAD_DV), F32),
                        pltpu.VMEM((RET_HEADS, c, HEAD_DV), F32),
                        pltpu.VMEM((RET_HEADS, SUBLANES, LANES), F32),
                        pltpu.VMEM((nb, RET_HEADS, dk, HEAD_DV), F32)],
        compiler_params=_cparams("parallel", "arbitrary"),
        name="ret_rev" if reverse else "ret_fwd",
    )(log_gamma, p, p, p, cos_t, sin_t)


def _rope_tables(n_ctx, seq, dk):
    n_freq = dk // 4
    inv = jnp.power(ROPE_BASE, -jnp.arange(n_freq, dtype=F32) / n_freq)
    rows = seq // GRID_W
    r = jnp.arange(rows, dtype=F32)
    col = jnp.arange(GRID_W, dtype=F32)
    row_ang = jnp.broadcast_to(r[:, None, None] * inv, (rows, GRID_W, n_freq))
    col_ang = jnp.broadcast_to(col[None, :, None] * inv, (rows, GRID_W, n_freq))
    ang = jnp.concatenate([row_ang, col_ang], axis=-1).reshape(rows * GRID_W, 2 * n_freq)
    cos, sin = jnp.cos(ang), jnp.sin(ang)
    cos_h = jnp.concatenate([cos, cos], axis=-1)
    sin_h = jnp.concatenate([-sin, sin], axis=-1)
    cos_t = jnp.concatenate([jnp.ones((n_ctx, dk), F32), cos_h], axis=0)
    sin_t = jnp.concatenate([jnp.zeros((n_ctx, dk), F32), sin_h], axis=0)
    return jnp.tile(cos_t, (1, RET_HEADS)), jnp.tile(sin_t, (1, RET_HEADS))


_GLA_LEVELS = tuple(GLA_CHUNK >> (i + 1) for i in range(GLA_CHUNK.bit_length() - 1))


def _gla_matrices(reverse):
    c = GLA_CHUNK
    p = np.arange(c)
    mats = [(p[None, :] <= p[:, None]), (p[None, :] > p[:, None])]
    for s in _GLA_LEVELS:
        bs = (p // s) * s
        right = ((p // s) % 2 == 1)
        m_r = (p[None, :] > bs[:, None]) & (p[None, :] <= p[:, None])
        m_l = (p[None, :] > p[:, None]) & (p[None, :] <= (bs + s)[:, None])
        mats.append(np.where(right[:, None], m_r, m_l))
    m = np.stack(mats).astype(np.float32)
    if reverse:
        m = m[:, ::-1, ::-1]
    return m.reshape(len(mats) * c, c)


def _gla_body(m_ref, lb_ref, q_ref, f_ref, v_ref, o_ref, st_scr, *, reverse, layer, nb):
    c = GLA_CHUNK
    step = pl.program_id(1)

    @pl.when(step == 0)
    def _():
        st_scr[...] = jnp.zeros_like(st_scr)

    lg = lb_ref[...]
    e = jnp.exp(lg - jnp.max(lg, axis=0, keepdims=True))
    sm = e / jnp.sum(e, axis=0, keepdims=True)
    lb = jnp.sum(sm[0:layer + 1, :], axis=0, keepdims=True) - sm[0:1, :]

    ri = lax.broadcasted_iota(jnp.int32, (c, 1), 0)
    i2 = lax.broadcasted_iota(jnp.int32, (c, c), 0)
    j2 = lax.broadcasted_iota(jnp.int32, (c, c), 1)
    if reverse:
        ri, i2, j2 = c - 1 - ri, c - 1 - i2, c - 1 - j2
    last = 0 if reverse else c - 1
    width = q_ref.shape[-1]
    nh = width // HEAD_DV

    rows_bb = list(range(nb))
    pairs = [(bb, h) for bb in rows_bb for h in range(nh)]

    def head(x, h):
        return x[:, h * HEAD_DV:(h + 1) * HEAD_DV]

    f = [lb + (1.0 - lb) * _sigmoid(f_ref[bb]) for bb in rows_bb]
    k = [1.0 - x for x in f]
    q = [_silu(q_ref[bb]) for bb in rows_bb]
    ex = [_dot_sel(m_ref[...], jnp.log(x)) for x in f]
    a_last = [jnp.exp(x[last:last + 1, :]) for x in ex]
    qa = [q[bb] * jnp.exp(ex[bb][0:c, :]) for bb in rows_bb]
    kl = [k[bb] * jnp.exp(ex[bb][c:2 * c, :]) for bb in rows_bb]

    qb = [x.astype(BF16) for x in q]
    kb = [x.astype(BF16) for x in k]
    scores = [jnp.where(i2 == j2, _dot_nt(head(qb[bb], h), head(kb[bb], h)), 0.0) for bb, h in pairs]
    for lvl, s in enumerate(_GLA_LEVELS):
        sh = s.bit_length() - 1
        right = jnp.broadcast_to(((ri >> sh) & 1).astype(F32), (c, width)).astype(BF16) > 0
        x = [jnp.where(right, qb[bb], kb[bb]) * jnp.exp(ex[bb][(2 + lvl) * c:(3 + lvl) * c, :].astype(BF16))
             for bb in rows_bb]
        bi, bj = i2 >> sh, j2 >> sh
        valid = bj == jnp.where((bi & 1) == 1, bi - 1, -1)
        gram = [_dot_nt(head(x[bb], h), head(x[bb], h)) for bb, h in pairs]
        scores = [sc + jnp.where(valid, gm, 0.0) for sc, gm in zip(scores, gram)]

    state_t = [st_scr[bb, h] for bb, h in pairs]
    o_intra = [_dot(scores[i], head(v_ref[bb], h)) for i, (bb, h) in enumerate(pairs)]
    o_inter = [_dot_nt(head(qa[bb], h), state_t[i]) for i, (bb, h) in enumerate(pairs)]
    kv = [_dot(head(v_ref[bb], h).T, head(kl[bb], h)) for bb, h in pairs]
    for i, (bb, h) in enumerate(pairs):
        o_ref[bb, :, h * HEAD_DV:(h + 1) * HEAD_DV] = o_intra[i] + o_inter[i]
        st_scr[bb, h] = state_t[i] * head(a_last[bb], h) + kv[i]


def _gla(p, lb_logits, *, n_ctx, reverse, layer, nb):
    bsz, l, _ = p.shape
    c = GLA_CHUNK
    n_chunks = l // c
    n_ctx_chunks = n_ctx // c
    m = _doubled(_gla_matrices(reverse))

    def ch(s):
        return _scan_tile(s, n_ctx_chunks, n_chunks, reverse)

    f_col = 2 if reverse else 1
    return pl.pallas_call(
        functools.partial(_gla_body, reverse=reverse, layer=layer, nb=nb),
        grid=(bsz // nb, n_chunks),
        in_specs=[pl.BlockSpec(m.shape, lambda b, s: (0, 0)),
                  pl.BlockSpec(lb_logits.shape, lambda b, s: (0, 0)),
                  pl.BlockSpec((nb, c, GROUP), lambda b, s: (b, ch(s), 0)),
                  pl.BlockSpec((nb, c, GROUP), lambda b, s: (b, ch(s), f_col)),
                  pl.BlockSpec((nb, c, GROUP), lambda b, s: (b, ch(s), 3))],
        out_specs=pl.BlockSpec((nb, c, GROUP), lambda b, s: (b, ch(s), 0)),
        out_shape=jax.ShapeDtypeStruct((bsz, l, GROUP), F32),
        scratch_shapes=[pltpu.VMEM((nb, LIN_HEADS, HEAD_DV, HEAD_DV), F32)],
        compiler_params=_cparams("parallel", "arbitrary"),
        name="gla_rev" if reverse else "gla_fwd",
    )(m, lb_logits, p, p, p)


def _gdn_body(cs_ref, cw_ref, arow_ref, dtrow_ref, t_ref,
              q_ref, qp_ref, qn_ref, k_ref, kp_ref, kn_ref, v_ref, vp_ref, vn_ref,
              o_ref, ext_scr, st_scr, *, reverse, n_ctx_chunks, n_chunks, beta_col, g_col, nb):
    c = LIN_CHUNK
    nh = LIN_HEADS
    n = nh * c
    step = pl.program_id(1)
    chunk = _scan_tile(step, n_ctx_chunks, n_chunks, reverse)

    @pl.when(step == 0)
    def _():
        st_scr[...] = jnp.zeros_like(st_scr)

    seg_first = jnp.logical_or(chunk == 0, chunk == n_ctx_chunks)
    seg_last = jnp.logical_or(chunk == n_ctx_chunks - 1, chunk == n_chunks - 1)
    keep_prev = jnp.where(seg_first, 0.0, 1.0)
    keep_next = jnp.where(seg_last, 0.0, 1.0)

    def conv_silu(bb, x_ref, xp_ref, xn_ref, col):
        ext = ext_scr.at[bb * 3 + col]
        ext[0:HALO, :] = xp_ref[bb] * keep_prev
        ext[HALO:HALO + c, :] = x_ref[bb]
        ext[HALO + c:, :] = xn_ref[bb] * keep_next
        u = cw_ref[0:1, col * GROUP:(col + 1) * GROUP] * ext[pl.ds(HALO - CONV_LEFT, c), :]
        for t in range(1, CONV_W):
            u = u + cw_ref[t:t + 1, col * GROUP:(col + 1) * GROUP] * ext[pl.ds(HALO - CONV_LEFT + t, c), :]
        return _silu(u)

    def stack(x):
        return jnp.concatenate([x[:, h * HEAD_DV:(h + 1) * HEAD_DV] for h in range(nh)], axis=0)

    def l2n(x):
        return x * lax.rsqrt(jnp.sum(x * x, axis=-1, keepdims=True) + EPS)

    i2 = lax.broadcasted_iota(jnp.int32, (n, n), 0)
    j2 = lax.broadcasted_iota(jnp.int32, (n, n), 1)
    head_lo = (i2 >> 6) << 6
    if reverse:
        mask = jnp.where(j2 < head_lo + c, j2, -1) >= i2
    else:
        mask = jnp.where(j2 >= head_lo, j2, n) <= i2
    same32 = (i2 >> 5) == (j2 >> 5)
    same16 = (i2 >> 4) == (j2 >> 4)
    eye = jnp.where(i2 == j2, 1.0, 0.0)
    last = 0 if reverse else c - 1

    def each(fn, *lists):
        return [fn(*args) for args in zip(*lists)]

    def head_bcast(x, col):
        return jnp.concatenate([jnp.broadcast_to(x[:, col + h:col + h + 1], (c, HEAD_DV)) for h in range(nh)], axis=0)

    rows_bb = list(range(nb))
    q = [l2n(stack(conv_silu(bb, q_ref, qp_ref, qn_ref, 0))) * (HEAD_DV ** -0.5) for bb in rows_bb]
    k = [l2n(stack(conv_silu(bb, k_ref, kp_ref, kn_ref, 1))) for bb in rows_bb]
    v = [stack(conv_silu(bb, v_ref, vp_ref, vn_ref, 2)) for bb in rows_bb]

    tail = [t_ref[bb] for bb in rows_bb]
    beta = [head_bcast(_sigmoid(x), beta_col) for x in tail]
    g = [head_bcast(-jnp.exp(arow_ref[...]) * _softplus(x + dtrow_ref[...]), g_col) for x in tail]
    gc = [_dot_sel(cs_ref[...], x) for x in g]
    gc_last = [jnp.concatenate([jnp.broadcast_to(x[h * c + last:h * c + last + 1, :], (c, HEAD_DV))
                                for h in range(nh)], axis=0) for x in gc]
    diff = [jnp.concatenate([x, x], axis=1) - jnp.transpose(x)[0:1, :] for x in gc]
    gam = [jnp.where(mask, jnp.exp(jnp.where(mask, x, 0.0)), 0.0) for x in diff]

    kb = each(lambda a, b: a * b, k, beta)
    mm = each(lambda a, b, gm: jnp.where(i2 == j2, 0.0, _dot_nt(a, b) * gm), kb, k, gam)
    qk = each(lambda a, b, gm: _dot_nt(a, b) * gm, q, k, gam)
    egc = [jnp.exp(x) for x in gc]
    rhs = each(lambda a, b, kbx, e: jnp.concatenate([a * b, kbx * e], axis=1), v, beta, kb, egc)

    md = [jnp.where(same16, x, 0.0) for x in mm]
    m2 = each(_dot, md, md)
    t = [eye - x for x in md]
    t = each(lambda a, b: a + _dot(a, b), t, m2)
    m4 = each(_dot, m2, m2)
    t = each(lambda a, b: a + _dot(a, b), t, m4)
    m8 = each(_dot, m4, m4)
    t = each(lambda a, b: a + _dot(a, b), t, m8)
    c32 = each(lambda a, b: jnp.where(same32, a - b, 0.0), mm, md)
    ct = each(_dot, c32, t)
    t = each(lambda a, b: a - _dot(a, b), t, ct)
    c64 = [jnp.where(same32, 0.0, x) for x in mm]
    ct = each(_dot, c64, t)
    t = each(lambda a, b: a - _dot(a, b), t, ct)
    sol = each(_dot, t, rhs)

    qd = each(lambda a, e: a * e, q, egc)
    kd = each(lambda a, gl, x: a * jnp.exp(gl - x), k, gc_last, gc)
    g_last = [jnp.exp(x) for x in gc_last]
    v_new = [jnp.concatenate([sol[bb][h * c:(h + 1) * c, :HEAD_DV]
                              - _dot(sol[bb][h * c:(h + 1) * c, HEAD_DV:], st_scr[bb, h]) for h in range(nh)], axis=0)
             for bb in rows_bb]
    o_intra = each(_dot, qk, v_new)
    for h in range(nh):
        rows = slice(h * c, (h + 1) * c)
        for bb in rows_bb:
            state = st_scr[bb, h]
            o_ref[bb, :, h * HEAD_DV:(h + 1) * HEAD_DV] = o_intra[bb][rows] + _dot(qd[bb][rows], state)
            st_scr[bb, h] = state * g_last[bb][h * c:h * c + 1, :] + _dot(kd[bb][rows].T, v_new[bb][rows])


def _gdn_cumsum_matrix(reverse):
    c, nh = LIN_CHUNK, LIN_HEADS
    p = np.arange(c)
    tri = (p[None, :] >= p[:, None]) if reverse else (p[None, :] <= p[:, None])
    return np.kron(np.eye(nh), tri.astype(np.float32))


def _gdn(p, tail, conv_w, a_row, dt_row, *, n_ctx, reverse, direction, nb):
    bsz, l, _ = p.shape
    c = LIN_CHUNK
    n_chunks = l // c
    n_ctx_chunks = n_ctx // c
    hb = c // HALO
    n_hblocks = l // HALO
    cs = _doubled(_gdn_cumsum_matrix(reverse))

    def ch(s):
        return _scan_tile(s, n_ctx_chunks, n_chunks, reverse)

    def triple(col):
        return [pl.BlockSpec((nb, c, GROUP), lambda b, s: (b, ch(s), col)),
                pl.BlockSpec((nb, HALO, GROUP), lambda b, s: (b, jnp.maximum(ch(s) * hb - 1, 0), col)),
                pl.BlockSpec((nb, HALO, GROUP),
                             lambda b, s: (b, jnp.minimum((ch(s) + 1) * hb, n_hblocks - 1), col))]

    const2 = lambda b, s: (0, 0)
    return pl.pallas_call(
        functools.partial(_gdn_body, reverse=reverse, n_ctx_chunks=n_ctx_chunks, n_chunks=n_chunks,
                          beta_col=direction * LIN_HEADS, g_col=(2 + direction) * LIN_HEADS, nb=nb),
        grid=(bsz // nb, n_chunks),
        in_specs=[pl.BlockSpec(cs.shape, const2), pl.BlockSpec(conv_w.shape, const2),
                  pl.BlockSpec((1, LANES), const2), pl.BlockSpec((1, LANES), const2),
                  pl.BlockSpec((nb, c, LANES), lambda b, s: (b, ch(s), 0))]
                 + triple(5) + triple(6) + triple(7),
        out_specs=pl.BlockSpec((nb, c, GROUP), lambda b, s: (b, ch(s), 0)),
        out_shape=jax.ShapeDtypeStruct((bsz, l, GROUP), F32),
        scratch_shapes=[pltpu.VMEM((3 * nb, c + 2 * HALO, GROUP), F32),
                        pltpu.VMEM((nb, LIN_HEADS, HEAD_DV, HEAD_DV), F32)],
        compiler_params=_cparams("parallel", "arbitrary"),
        name="gdn_rev" if reverse else "gdn_fwd",
    )(cs, conv_w, a_row, dt_row, tail, *([p] * 9))


def _post_body(x_ref, af_ref, ar_ref, ga_ref, bf_ref, br_ref, gb_ref, w_ref, ml_ref, mc_ref, o_ref,
               *, even, tm, n_ctx):
    t = pl.program_id(1)
    a = af_ref[...] + ar_ref[...]
    bsum = bf_ref[...] + br_ref[...]
    if even:
        ya = a * _gelu_tanh(ga_ref[...])
        yb = _head_norm(bsum, True) * _silu(gb_ref[...])
    else:
        ya = _head_norm(a, False) * _silu(ga_ref[...])
        yb = _head_norm(bsum, False) * _silu(gb_ref[...])
    half = w_ref.shape[0] // 2
    mix = jnp.dot(ya.astype(BF16), w_ref[0:half, :], preferred_element_type=F32)
    mix = mix + jnp.dot(yb.astype(BF16), w_ref[half:, :], preferred_element_type=F32)
    o_ref[...] = x_ref[...] + _row_select(t, tm, n_ctx, mc_ref, ml_ref, 2) * mix


def _post(x, mod, a_f, a_r, b_f, b_r, p, gate_a_col, gate_b_col, w_out, *, even, tm, n_ctx):
    bsz, l, d = x.shape
    ctx_row = bsz
    tok = lambda b, t: (b, t, 0)
    grp = pl.BlockSpec((None, tm, GROUP), tok)
    return pl.pallas_call(
        functools.partial(_post_body, even=even, tm=tm, n_ctx=n_ctx),
        grid=(bsz, l // tm),
        in_specs=[pl.BlockSpec((None, tm, d), tok), grp, grp,
                  pl.BlockSpec((None, tm, GROUP), lambda b, t: (b, t, gate_a_col)), grp, grp,
                  pl.BlockSpec((None, tm, GROUP), lambda b, t: (b, t, gate_b_col)),
                  pl.BlockSpec(w_out.shape, lambda b, t: (0, 0)),
                  pl.BlockSpec((None, 6, d), lambda b, t: (b, 0, 0)),
                  pl.BlockSpec((None, 6, d), lambda b, t: (ctx_row, 0, 0))],
        out_specs=pl.BlockSpec((None, tm, d), tok),
        out_shape=jax.ShapeDtypeStruct((bsz, l, d), F32),
        compiler_params=_cparams("parallel", "parallel"),
        name="post",
    )(x, a_f, a_r, p, b_f, b_r, p, w_out, mod, mod)


def _mlp_body(x_ref, g_ref, ml_ref, mc_ref, w1_ref, w2_ref, fg_ref, o_ref, h_scr, acc_scr, *, tm, n_ctx, final):
    t = pl.program_id(1)
    f = pl.program_id(2)

    @pl.when(f == 0)
    def _():
        y = _rms(x_ref[...], g_ref[...])
        shift = _row_select(t, tm, n_ctx, mc_ref, ml_ref, 3)
        scale = _row_select(t, tm, n_ctx, mc_ref, ml_ref, 4)
        h_scr[...] = (y * (1.0 + scale) + shift).astype(BF16)
        acc_scr[...] = jnp.zeros_like(acc_scr)

    hid = jnp.maximum(jnp.dot(h_scr[...], w1_ref[...], preferred_element_type=F32), 0.0)
    acc_scr[...] += jnp.dot((hid * hid).astype(BF16), w2_ref[...], preferred_element_type=F32)

    @pl.when(f == pl.num_programs(2) - 1)
    def _():
        y = x_ref[...] + _row_select(t, tm, n_ctx, mc_ref, ml_ref, 5) * acc_scr[...]
        if final:
            y = _rms(y, fg_ref[...])
        o_ref[...] = y


def _mlp(x, mod, g, w1, w2, final_g, *, tm, tf, n_ctx, final):
    bsz, l, d = x.shape
    dff = w1.shape[1]
    ctx_row = bsz
    tok = lambda b, t, f: (b, t, 0)
    return pl.pallas_call(
        functools.partial(_mlp_body, tm=tm, n_ctx=n_ctx, final=final),
        grid=(bsz, l // tm, dff // tf),
        in_specs=[pl.BlockSpec((None, tm, d), tok),
                  pl.BlockSpec((1, d), lambda b, t, f: (0, 0)),
                  pl.BlockSpec((None, 6, d), lambda b, t, f: (b, 0, 0)),
                  pl.BlockSpec((None, 6, d), lambda b, t, f: (ctx_row, 0, 0)),
                  pl.BlockSpec((d, tf), lambda b, t, f: (0, f)),
                  pl.BlockSpec((tf, d), lambda b, t, f: (f, 0)),
                  pl.BlockSpec((1, d), lambda b, t, f: (0, 0))],
        out_specs=pl.BlockSpec((None, tm, d), tok),
        out_shape=jax.ShapeDtypeStruct((bsz, l, d), F32),
        scratch_shapes=[pltpu.VMEM((tm, d), BF16), pltpu.VMEM((tm, d), F32)],
        compiler_params=_cparams("parallel", "parallel", "arbitrary"),
        name="mlp",
    )(x, g.reshape(1, d), mod, mod, w1, w2, final_g.reshape(1, d))


def _token_tile(l):
    for div in (4, 8, 16, 17, 34):
        if l % div == 0 and (l // div) % 16 == 0 and l // div <= 1088:
            return l // div
    return l


def kernel(x, c, ctx, c_ctx, ada_w, ada_b, norm1_g, norm2_g, mix_w_out, mlp_w1, mlp_w2, ev_w_in, lru_conv_w,
           lru_conv_b, lru_wa, lru_ba, lru_wx, lru_bx, lru_lambda, ret_log_gamma, od_w_in, hg_lb_logits,
           gdn_conv_w, gdn_a_log, gdn_dt_bias, final_g):
    bsz, seq, d = x.shape
    n_ctx = ctx.shape[1]
    depth = ada_w.shape[0]
    l = n_ctx + seq
    tm = _token_tile(l)
    tf = 512
    nb = 4 if bsz % 4 == 0 else 1

    mods = _mods(c, c_ctx, ada_w, ada_b)
    ret_dk = (ev_w_in.shape[2] - 4 * GROUP) // (2 * RET_HEADS)
    cos_t, sin_t = _rope_tables(n_ctx, seq, ret_dk)

    xs = jnp.concatenate([ctx, x], axis=1)
    for layer in range(depth):
        mod = mods[layer]
        w_out = mix_w_out[layer].astype(BF16)
        if layer % 2 == 0:
            e = layer // 2
            (p,) = _proj(xs, mod, norm1_g[layer], ev_w_in[e].astype(BF16), None, tm=tm, n_ctx=n_ctx)
            mix = []
            for dr, rev in enumerate((False, True)):
                mix.append((
                    _lru(p, lru_conv_w[e], lru_conv_b[e], _blockdiag_halves(lru_wa[e, dr]).astype(BF16),
                         lru_ba[e, dr], _blockdiag_halves(lru_wx[e, dr]).astype(BF16), lru_bx[e, dr],
                         lru_lambda[e, dr], n_ctx=n_ctx, reverse=rev),
                    _retention(p, cos_t, sin_t, ret_log_gamma[e, dr], n_ctx=n_ctx, reverse=rev, nb=nb)))
            gate_cols = (1, 4)
        else:
            o = layer // 2
            n_main = 9 * GROUP
            w_in = od_w_in[o]
            w_tail = jnp.pad(w_in[:, n_main:], ((0, 0), (0, LANES - (w_in.shape[1] - n_main)))).astype(BF16)
            p, tail = _proj(xs, mod, norm1_g[layer], w_in[:, :n_main].astype(BF16), w_tail, tm=tm, n_ctx=n_ctx)
            mix = []
            for dr, rev in enumerate((False, True)):
                g_col = (2 + dr) * LIN_HEADS
                a_row = jnp.zeros((1, LANES), F32).at[0, g_col:g_col + LIN_HEADS].set(gdn_a_log[o, dr])
                dt_row = jnp.zeros((1, LANES), F32).at[0, g_col:g_col + LIN_HEADS].set(gdn_dt_bias[o, dr])
                mix.append((
                    _gla(p, hg_lb_logits[dr], n_ctx=n_ctx, reverse=rev, layer=o, nb=nb),
                    _gdn(p, tail, gdn_conv_w[o], a_row, dt_row, n_ctx=n_ctx, reverse=rev, direction=dr, nb=nb)))
            gate_cols = (4, 8)
        xs = _post(xs, mod, mix[0][0], mix[1][0], mix[0][1], mix[1][1], p, gate_cols[0], gate_cols[1], w_out,
                   even=layer % 2 == 0, tm=tm // 2, n_ctx=n_ctx)
        xs = _mlp(xs, mod, norm2_g[layer], mlp_w1[layer].astype(BF16), mlp_w2[layer].astype(BF16), final_g,
                  tm=tm, tf=tf, n_ctx=n_ctx, final=layer == depth - 1)
    return xs[:, n_ctx:]
```

```python
import functools

import numpy as np
import jax
import jax.numpy as jnp
from jax import lax
from jax.experimental import pallas as pl
from jax.experimental.pallas import tpu as pltpu

F32 = jnp.float32
BF16 = jnp.bfloat16

EPS = 1e-6
GRID_W = 64
CONV_W = 4
CONV_LEFT = CONV_W // 2
LRU_BLOCKS = 8
LRU_C = 8.0
RET_HEADS = 4
RET_CHUNK = 256
ROPE_BASE = 10000.0
LIN_HEADS = 4
GLA_CHUNK = 128
LIN_CHUNK = 64
HEAD_DV = 128

LANES = 128
SUBLANES = 8
V7X_VMEM_BYTES = 64 * 1024 * 1024
VMEM_LIMIT = (V7X_VMEM_BYTES * 3) // 4
GROUP = 512
PROJ_MAX_TN = 2560

HALO = 2 * SUBLANES
ACT = jnp.bfloat16


def _cparams(*sem):
    return pltpu.CompilerParams(dimension_semantics=sem, vmem_limit_bytes=VMEM_LIMIT)


def _sigmoid(x):
    return 1.0 / (1.0 + jnp.exp(-x))


def _silu(x):
    return x * _sigmoid(x)


def _softplus(x):
    return jnp.maximum(x, 0.0) + jnp.log(1.0 + jnp.exp(-jnp.abs(x)))


def _gelu_tanh(x):
    return 0.5 * x * (1.0 + jnp.tanh(0.7978845608028654 * (x + 0.044715 * x * x * x)))


def _dot(a, b):
    return jnp.dot(a.astype(BF16), b.astype(BF16), preferred_element_type=F32)


def _dot_nt(a, b):
    return lax.dot_general(a.astype(BF16), b.astype(BF16), (((1,), (1,)), ((), ())),
                           preferred_element_type=F32)


def _split_bf16(x):
    hi = x.astype(BF16)
    lo = (x - hi.astype(F32)).astype(BF16)
    return hi, lo


def _dot_sel(m2, x):
    hi, lo = _split_bf16(x)
    return jnp.dot(m2, jnp.concatenate([hi, lo], axis=0), preferred_element_type=F32)


def _doubled(m):
    return jnp.asarray(np.concatenate([m, m], axis=1), BF16)


def _dot3(a, b):
    ah, al = _split_bf16(a)
    bh, bl = _split_bf16(b)
    return (jnp.dot(ah, bh, preferred_element_type=F32) + jnp.dot(ah, bl, preferred_element_type=F32)
            + jnp.dot(al, bh, preferred_element_type=F32))


def _row_select(t, tm, n_ctx, mc_ref, ml_ref, idx):
    rows = t * tm + lax.broadcasted_iota(jnp.int32, (tm, 1), 0)
    return jnp.where(rows < n_ctx, mc_ref[idx:idx + 1, :], ml_ref[idx:idx + 1, :])


def _rms(x, g):
    return x * lax.rsqrt(jnp.mean(x * x, axis=-1, keepdims=True) + EPS) * g


def _head_norm(o, center):
    outs = []
    for h in range(o.shape[-1] // HEAD_DV):
        oh = o[:, h * HEAD_DV:(h + 1) * HEAD_DV]
        if center:
            oh = oh - jnp.mean(oh, axis=-1, keepdims=True)
        outs.append(oh * lax.rsqrt(jnp.mean(oh * oh, axis=-1, keepdims=True) + EPS))
    return jnp.concatenate(outs, axis=-1)


def _scan_tile(step, n_first, n_total, reverse):
    if not reverse:
        return step
    return jnp.where(step < n_first, n_first - 1 - step, n_total - 1 - (step - n_first))


def _mods_body(s_ref, w_ref, b_ref, o_ref):
    s = _silu(s_ref[...])
    o_ref[...] = _dot3(s, w_ref[...]) + b_ref[...]


def _mods(c, c_ctx, ada_w, ada_b):
    depth, d, n6 = ada_w.shape
    bsz = c.shape[0]
    rows = -(-(bsz + 1) // SUBLANES) * SUBLANES
    s = jnp.zeros((rows, d), F32).at[:bsz].set(c).at[bsz].set(c_ctx)
    tn = 1024
    out = pl.pallas_call(
        _mods_body,
        grid=(depth, n6 // tn),
        in_specs=[pl.BlockSpec((rows, d), lambda l, n: (0, 0)),
                  pl.BlockSpec((None, d, tn), lambda l, n: (l, 0, n)),
                  pl.BlockSpec((None, 1, tn), lambda l, n: (l, 0, n))],
        out_specs=pl.BlockSpec((None, rows, tn), lambda l, n: (l, 0, n)),
        out_shape=jax.ShapeDtypeStruct((depth, rows, n6), F32),
        compiler_params=_cparams("parallel", "parallel"),
        name="mods",
    )(s, ada_w, ada_b.reshape(depth, 1, n6))
    return out.reshape(depth, rows, 6, d)


def _proj_body(*refs, tm, n_ctx, has_tail):
    if has_tail:
        x_ref, ml_ref, mc_ref, g_ref, w_ref, wt_ref, o_ref, ot_ref, h_scr = refs
    else:
        x_ref, ml_ref, mc_ref, g_ref, w_ref, o_ref, h_scr = refs
    t = pl.program_id(1)

    @pl.when(pl.program_id(2) == 0)
    def _():
        y = _rms(x_ref[...], g_ref[...])
        shift = _row_select(t, tm, n_ctx, mc_ref, ml_ref, 0)
        scale = _row_select(t, tm, n_ctx, mc_ref, ml_ref, 1)
        h_scr[...] = (y * (1.0 + scale) + shift).astype(BF16)
        if has_tail:
            ot_ref[...] = jnp.dot(h_scr[...], wt_ref[...], preferred_element_type=F32)

    o_ref[...] = jnp.dot(h_scr[...], w_ref[...], preferred_element_type=F32).astype(o_ref.dtype)


def _proj(x, mod, g, w, w_tail, *, tm, n_ctx):
    bsz, l, d = x.shape
    n_out = w.shape[1]
    tn = max(t for t in range(LANES, PROJ_MAX_TN + 1, LANES) if n_out % t == 0)
    ng = n_out // tn
    ctx_row = bsz
    in_specs = [pl.BlockSpec((None, tm, d), lambda b, t, n: (b, t, 0)),
                pl.BlockSpec((None, 6, d), lambda b, t, n: (b, 0, 0)),
                pl.BlockSpec((None, 6, d), lambda b, t, n: (ctx_row, 0, 0)),
                pl.BlockSpec((1, d), lambda b, t, n: (0, 0)),
                pl.BlockSpec((d, tn), lambda b, t, n: (0, n))]
    out_specs = [pl.BlockSpec((None, tm, tn), lambda b, t, n: (b, t, n))]
    out_shape = [jax.ShapeDtypeStruct((bsz, l, n_out), ACT)]
    args = [x, mod, mod, g.reshape(1, d), w]
    if w_tail is not None:
        in_specs.append(pl.BlockSpec((d, LANES), lambda b, t, n: (0, 0)))
        out_specs.append(pl.BlockSpec((None, tm, LANES), lambda b, t, n: (b, t, 0)))
        out_shape.append(jax.ShapeDtypeStruct((bsz, l, LANES), F32))
        args.append(w_tail)
    outs = pl.pallas_call(
        functools.partial(_proj_body, tm=tm, n_ctx=n_ctx, has_tail=w_tail is not None),
        grid=(bsz, l // tm, ng),
        in_specs=in_specs, out_specs=out_specs, out_shape=out_shape,
        scratch_shapes=[pltpu.VMEM((tm, d), BF16)],
        compiler_params=_cparams("parallel", "parallel", "arbitrary"),
        name="proj",
    )(*args)
    return outs


def _lru_body(x_ref, xp_ref, xn_ref, cw_ref, cb_ref, wa_ref, ba_ref, wx_ref, bx_ref, lam_ref,
              o_ref, ext_scr, a_scr, b_scr, h_scr, *, tl, n_ctx_tiles, n_tiles, reverse):
    step = pl.program_id(0)
    tile = _scan_tile(step, n_ctx_tiles, n_tiles, reverse)
    bsz = x_ref.shape[0]
    width = x_ref.shape[2]
    half = width // 2
    seg_first = jnp.logical_or(tile == 0, tile == n_ctx_tiles)
    seg_last = jnp.logical_or(tile == n_ctx_tiles - 1, tile == n_tiles - 1)
    keep_prev = jnp.where(seg_first, 0.0, 1.0)
    keep_next = jnp.where(seg_last, 0.0, 1.0)
    neg_c_sp = -LRU_C * _softplus(-lam_ref[...])

    @pl.when(step == 0)
    def _():
        h_scr[...] = jnp.zeros_like(h_scr)

    def gates(b, carry):
        ext_scr[0:HALO, :] = xp_ref[b].astype(F32) * keep_prev
        ext_scr[HALO:HALO + tl, :] = x_ref[b].astype(F32)
        ext_scr[HALO + tl:, :] = xn_ref[b].astype(F32) * keep_next
        u = cb_ref[...] + cw_ref[0:1, :] * ext_scr[pl.ds(HALO - CONV_LEFT, tl), :]
        for k in range(1, CONV_W):
            u = u + cw_ref[k:k + 1, :] * ext_scr[pl.ds(HALO - CONV_LEFT + k, tl), :]
        ub = u.astype(BF16)
        r_pre = jnp.concatenate(
            [jnp.dot(ub[:, :half], wa_ref[0], preferred_element_type=F32),
             jnp.dot(ub[:, half:], wa_ref[1], preferred_element_type=F32)], axis=-1) + ba_ref[...]
        i_pre = jnp.concatenate(
            [jnp.dot(ub[:, :half], wx_ref[0], preferred_element_type=F32),
             jnp.dot(ub[:, half:], wx_ref[1], preferred_element_type=F32)], axis=-1) + bx_ref[...]
        a = jnp.exp(neg_c_sp * _sigmoid(r_pre))
        a_scr[b] = a
        b_scr[b] = jnp.sqrt(1.0 - a * a) * (_sigmoid(i_pre) * u)
        return carry

    lax.fori_loop(0, bsz, gates, 0)

    def scan_rows(j, hs):
        idx = (tl - 1 - j) if reverse else j
        new = []
        for b in range(bsz):
            h = a_scr[b, pl.ds(idx, 1), :] * hs[b] + b_scr[b, pl.ds(idx, 1), :]
            b_scr[b, pl.ds(idx, 1), :] = h
            new.append(h)
        return tuple(new)

    hs = tuple(h_scr[b, 0:1, :] for b in range(bsz))
    hs = lax.fori_loop(0, tl, scan_rows, hs)
    for b in range(bsz):
        h_scr[b, 0:1, :] = hs[b]
    o_ref[...] = b_scr[...].astype(o_ref.dtype)


def _lru(p, conv_w, conv_b, wa, ba, wx, bx, lam, *, n_ctx, reverse, tl=256):
    bsz, l, _ = p.shape
    width = GROUP
    n_tiles = l // tl
    n_ctx_tiles = n_ctx // tl
    hb = tl // HALO
    n_hblocks = l // HALO

    def tile_of(s):
        return _scan_tile(s, n_ctx_tiles, n_tiles, reverse)

    vec = pl.BlockSpec((1, width), lambda s: (0, 0))
    mat = pl.BlockSpec((2, width // 2, width // 2), lambda s: (0, 0, 0))
    return pl.pallas_call(
        functools.partial(_lru_body, tl=tl, n_ctx_tiles=n_ctx_tiles, n_tiles=n_tiles, reverse=reverse),
        grid=(n_tiles,),
        in_specs=[pl.BlockSpec((bsz, tl, width), lambda s: (0, tile_of(s), 0)),
                  pl.BlockSpec((bsz, HALO, width), lambda s: (0, jnp.maximum(tile_of(s) * hb - 1, 0), 0)),
                  pl.BlockSpec((bsz, HALO, width),
                               lambda s: (0, jnp.minimum((tile_of(s) + 1) * hb, n_hblocks - 1), 0)),
                  pl.BlockSpec((CONV_W, width), lambda s: (0, 0)), vec, mat, vec, mat, vec, vec],
        out_specs=pl.BlockSpec((bsz, tl, width), lambda s: (0, tile_of(s), 0)),
        out_shape=jax.ShapeDtypeStruct((bsz, l, width), ACT),
        scratch_shapes=[pltpu.VMEM((tl + 2 * HALO, width), F32),
                        pltpu.VMEM((bsz, tl, width), F32),
                        pltpu.VMEM((bsz, tl, width), F32),
                        pltpu.VMEM((bsz, SUBLANES, width), F32)],
        compiler_params=_cparams("arbitrary"),
        name="lru_rev" if reverse else "lru_fwd",
    )(p, p, p, conv_w, conv_b.reshape(1, width), wa, ba.reshape(1, width), wx, bx.reshape(1, width),
      lam.reshape(1, width))


def _blockdiag_halves(w):
    nb, n, _ = w.shape
    per = nb // 2
    w = w.reshape(2, per, n, n)
    eye = jnp.eye(per, dtype=w.dtype)
    out = w[:, :, :, None, :] * eye[None, :, None, :, None]
    return out.reshape(2, per * n, per * n)


def _ret_body(lg_ref, q_ref, k_ref, v_ref, cos_ref, sin_ref, o_ref, dec_scr, qd_scr, kd_scr, cd_scr, st_scr,
              *, reverse, dk, nb):
    c = RET_CHUNK
    step = pl.program_id(1)

    @pl.when(step == 0)
    def _():
        st_scr[...] = jnp.zeros_like(st_scr)
        i = lax.broadcasted_iota(jnp.int32, (c, c), 0)
        j = lax.broadcasted_iota(jnp.int32, (c, c), 1)
        diff = (j - i) if reverse else (i - j)
        row = lax.broadcasted_iota(jnp.int32, (c, HEAD_DV), 0)
        pos = ((c - 1 - row) if reverse else row).astype(F32)
        dpos = jnp.maximum(diff, 0).astype(F32)
        for h in range(RET_HEADS):
            lg = lg_ref[h]
            dec_scr[h] = jnp.where(diff >= 0, jnp.exp(lg * dpos), 0.0)
            qd_scr[h] = jnp.exp(lg * (pos + 1.0))
            kd_scr[h] = jnp.exp(lg * (c - 1.0 - pos))
            cd_scr[h] = jnp.exp(jnp.full((SUBLANES, LANES), c, F32) * lg)

    def rope(x):
        width = x.shape[-1]
        lane = lax.broadcasted_iota(jnp.int32, x.shape, 1)
        first = (lane & (dk - 1)) < dk // 2
        rot = jnp.where(first, pltpu.roll(x, width - dk // 2, axis=1), pltpu.roll(x, dk // 2, axis=1))
        return x * cos_ref[...] + rot * sin_ref[...]

    q = [rope(q_ref[bb].astype(F32)) for bb in range(nb)]
    k = [rope(k_ref[bb].astype(F32)) * (dk ** -0.5) for bb in range(nb)]
    pairs = [(bb, h) for bb in range(nb) for h in range(RET_HEADS)]
    qh = [q[bb][:, h * dk:(h + 1) * dk] for bb, h in pairs]
    kh = [k[bb][:, h * dk:(h + 1) * dk] for bb, h in pairs]
    vh = [v_ref[bb, :, h * HEAD_DV:(h + 1) * HEAD_DV] for bb, h in pairs]
    s = [_dot_nt(qh[i], kh[i]) * dec_scr[h] for i, (bb, h) in enumerate(pairs)]
    state = [st_scr[bb, h] for bb, h in pairs]
    o_x = [_dot(qh[i], state[i]) * qd_scr[h] for i, (bb, h) in enumerate(pairs)]
    o_in = [_dot(s[i], vh[i]) for i in range(len(pairs))]
    kv = [_dot((kh[i] * kd_scr[h][:, :dk]).T, vh[i]) for i, (bb, h) in enumerate(pairs)]
    for i, (bb, h) in enumerate(pairs):
        st_scr[bb, h] = state[i] * cd_scr[h][0:1, :] + kv[i]
        o_ref[bb, :, h * HEAD_DV:(h + 1) * HEAD_DV] = (o_in[i] + o_x[i]).astype(o_ref.dtype)


def _retention(p, cos_t, sin_t, log_gamma, *, n_ctx, reverse, nb):
    bsz, l, _ = p.shape
    c = RET_CHUNK
    qk_w = cos_t.shape[1]
    dk = qk_w // RET_HEADS
    n_chunks = l // c
    n_ctx_chunks = n_ctx // c

    def ch(s):
        return _scan_tile(s, n_ctx_chunks, n_chunks, reverse)

    q_col = (2 * GROUP) // qk_w
    return pl.pallas_call(
        functools.partial(_ret_body, reverse=reverse, dk=dk, nb=nb),
        grid=(bsz // nb, n_chunks),
        in_specs=[pl.BlockSpec(memory_space=pltpu.SMEM),
                  pl.BlockSpec((nb, c, qk_w), lambda b, s: (b, ch(s), q_col)),
                  pl.BlockSpec((nb, c, qk_w), lambda b, s: (b, ch(s), q_col + 1)),
                  pl.BlockSpec((nb, c, GROUP), lambda b, s: (b, ch(s), 3)),
                  pl.BlockSpec((c, qk_w), lambda b, s: (ch(s), 0)),
                  pl.BlockSpec((c, qk_w), lambda b, s: (ch(s), 0))],
        out_specs=pl.BlockSpec((nb, c, GROUP), lambda b, s: (b, ch(s), 0)),
        out_shape=jax.ShapeDtypeStruct((bsz, l, GROUP), ACT),
        scratch_shapes=[pltpu.VMEM((RET_HEADS, c, c), F32),
                        pltpu.VMEM((RET_HEADS, c, HEAD_DV), F32),
                        pltpu.VMEM((RET_HEADS, c, HEAD_DV), F32),
                        pltpu.VMEM((RET_HEADS, SUBLANES, LANES), F32),
                        pltpu.VMEM((nb, RET_HEADS, dk, HEAD_DV), F32)],
        compiler_params=_cparams("parallel", "arbitrary"),
        name="ret_rev" if reverse else "ret_fwd",
    )(log_gamma, p, p, p, cos_t, sin_t)


def _rope_tables(n_ctx, seq, dk):
    n_freq = dk // 4
    inv = jnp.power(ROPE_BASE, -jnp.arange(n_freq, dtype=F32) / n_freq)
    rows = seq // GRID_W
    r = jnp.arange(rows, dtype=F32)
    col = jnp.arange(GRID_W, dtype=F32)
    row_ang = jnp.broadcast_to(r[:, None, None] * inv, (rows, GRID_W, n_freq))
    col_ang = jnp.broadcast_to(col[None, :, None] * inv, (rows, GRID_W, n_freq))
    ang = jnp.concatenate([row_ang, col_ang], axis=-1).reshape(rows * GRID_W, 2 * n_freq)
    cos, sin = jnp.cos(ang), jnp.sin(ang)
    cos_h = jnp.concatenate([cos, cos], axis=-1)
    sin_h = jnp.concatenate([-sin, sin], axis=-1)
    cos_t = jnp.concatenate([jnp.ones((n_ctx, dk), F32), cos_h], axis=0)
    sin_t = jnp.concatenate([jnp.zeros((n_ctx, dk), F32), sin_h], axis=0)
    return jnp.tile(cos_t, (1, RET_HEADS)), jnp.tile(sin_t, (1, RET_HEADS))


_GLA_LEVELS = tuple(GLA_CHUNK >> (i + 1) for i in range(GLA_CHUNK.bit_length() - 1))


def _gla_matrices(reverse):
    c = GLA_CHUNK
    p = np.arange(c)
    mats = [(p[None, :] <= p[:, None]), (p[None, :] > p[:, None])]
    for s in _GLA_LEVELS:
        bs = (p // s) * s
        right = ((p // s) % 2 == 1)
        m_r = (p[None, :] > bs[:, None]) & (p[None, :] <= p[:, None])
        m_l = (p[None, :] > p[:, None]) & (p[None, :] <= (bs + s)[:, None])
        mats.append(np.where(right[:, None], m_r, m_l))
    m = np.stack(mats).astype(np.float32)
    if reverse:
        m = m[:, ::-1, ::-1]
    return m.reshape(len(mats) * c, c)


def _gla_body(m_ref, lb_ref, q_ref, f_ref, v_ref, o_ref, st_scr, *, reverse, layer, nb):
    c = GLA_CHUNK
    step = pl.program_id(1)

    @pl.when(step == 0)
    def _():
        st_scr[...] = jnp.zeros_like(st_scr)

    lg = lb_ref[...]
    e = jnp.exp(lg - jnp.max(lg, axis=0, keepdims=True))
    sm = e / jnp.sum(e, axis=0, keepdims=True)
    lb = jnp.sum(sm[0:layer + 1, :], axis=0, keepdims=True) - sm[0:1, :]

    ri = lax.broadcasted_iota(jnp.int32, (c, 1), 0)
    i2 = lax.broadcasted_iota(jnp.int32, (c, c), 0)
    j2 = lax.broadcasted_iota(jnp.int32, (c, c), 1)
    if reverse:
        ri, i2, j2 = c - 1 - ri, c - 1 - i2, c - 1 - j2
    last = 0 if reverse else c - 1
    width = q_ref.shape[-1]
    nh = width // HEAD_DV

    rows_bb = list(range(nb))
    pairs = [(bb, h) for bb in rows_bb for h in range(nh)]

    def head(x, h):
        return x[:, h * HEAD_DV:(h + 1) * HEAD_DV]

    f = [lb + (1.0 - lb) * _sigmoid(f_ref[bb].astype(F32)) for bb in rows_bb]
    k = [1.0 - x for x in f]
    q = [_silu(q_ref[bb].astype(F32)) for bb in rows_bb]
    ex = [_dot_sel(m_ref[...], jnp.log(x)) for x in f]
    a_last = [jnp.exp(x[last:last + 1, :]) for x in ex]
    qa = [q[bb] * jnp.exp(ex[bb][0:c, :]) for bb in rows_bb]
    kl = [k[bb] * jnp.exp(ex[bb][c:2 * c, :]) for bb in rows_bb]

    qb = [x.astype(BF16) for x in q]
    kb = [x.astype(BF16) for x in k]
    scores = [jnp.where(i2 == j2, _dot_nt(head(qb[bb], h), head(kb[bb], h)), 0.0) for bb, h in pairs]
    for lvl, s in enumerate(_GLA_LEVELS):
        sh = s.bit_length() - 1
        right = jnp.broadcast_to(((ri >> sh) & 1).astype(F32), (c, width)).astype(BF16) > 0
        x = [jnp.where(right, qb[bb], kb[bb]) * jnp.exp(ex[bb][(2 + lvl) * c:(3 + lvl) * c, :].astype(BF16))
             for bb in rows_bb]
        bi, bj = i2 >> sh, j2 >> sh
        valid = bj == jnp.where((bi & 1) == 1, bi - 1, -1)
        gram = [_dot_nt(head(x[bb], h), head(x[bb], h)) for bb, h in pairs]
        scores = [sc + jnp.where(valid, gm, 0.0) for sc, gm in zip(scores, gram)]

    state_t = [st_scr[bb, h] for bb, h in pairs]
    o_intra = [_dot(scores[i], head(v_ref[bb], h)) for i, (bb, h) in enumerate(pairs)]
    o_inter = [_dot_nt(head(qa[bb], h), state_t[i]) for i, (bb, h) in enumerate(pairs)]
    kv = [_dot(head(v_ref[bb], h).astype(F32).T, head(kl[bb], h)) for bb, h in pairs]
    for i, (bb, h) in enumerate(pairs):
        o_ref[bb, :, h * HEAD_DV:(h + 1) * HEAD_DV] = (o_intra[i] + o_inter[i]).astype(o_ref.dtype)
        st_scr[bb, h] = state_t[i] * head(a_last[bb], h) + kv[i]


def _gla(p, lb_logits, *, n_ctx, reverse, layer, nb):
    bsz, l, _ = p.shape
    c = GLA_CHUNK
    n_chunks = l // c
    n_ctx_chunks = n_ctx // c
    m = _doubled(_gla_matrices(reverse))

    def ch(s):
        return _scan_tile(s, n_ctx_chunks, n_chunks, reverse)

    f_col = 2 if reverse else 1
    return pl.pallas_call(
        functools.partial(_gla_body, reverse=reverse, layer=layer, nb=nb),
        grid=(bsz // nb, n_chunks),
        in_specs=[pl.BlockSpec(m.shape, lambda b, s: (0, 0)),
                  pl.BlockSpec(lb_logits.shape, lambda b, s: (0, 0)),
                  pl.BlockSpec((nb, c, GROUP), lambda b, s: (b, ch(s), 0)),
                  pl.BlockSpec((nb, c, GROUP), lambda b, s: (b, ch(s), f_col)),
                  pl.BlockSpec((nb, c, GROUP), lambda b, s: (b, ch(s), 3))],
        out_specs=pl.BlockSpec((nb, c, GROUP), lambda b, s: (b, ch(s), 0)),
        out_shape=jax.ShapeDtypeStruct((bsz, l, GROUP), ACT),
        scratch_shapes=[pltpu.VMEM((nb, LIN_HEADS, HEAD_DV, HEAD_DV), F32)],
        compiler_params=_cparams("parallel", "arbitrary"),
        name="gla_rev" if reverse else "gla_fwd",
    )(m, lb_logits, p, p, p)


def _gdn_body(cs_ref, cw_ref, arow_ref, dtrow_ref, t_ref,
              q_ref, qp_ref, qn_ref, k_ref, kp_ref, kn_ref, v_ref, vp_ref, vn_ref,
              o_ref, ext_scr, st_scr, *, reverse, n_ctx_chunks, n_chunks, beta_col, g_col, nb):
    c = LIN_CHUNK
    nh = LIN_HEADS
    n = nh * c
    step = pl.program_id(1)
    chunk = _scan_tile(step, n_ctx_chunks, n_chunks, reverse)

    @pl.when(step == 0)
    def _():
        st_scr[...] = jnp.zeros_like(st_scr)

    seg_first = jnp.logical_or(chunk == 0, chunk == n_ctx_chunks)
    seg_last = jnp.logical_or(chunk == n_ctx_chunks - 1, chunk == n_chunks - 1)
    keep_prev = jnp.where(seg_first, 0.0, 1.0)
    keep_next = jnp.where(seg_last, 0.0, 1.0)

    def conv_silu(bb, x_ref, xp_ref, xn_ref, col):
        ext = ext_scr.at[bb * 3 + col]
        ext[0:HALO, :] = xp_ref[bb].astype(F32) * keep_prev
        ext[HALO:HALO + c, :] = x_ref[bb].astype(F32)
        ext[HALO + c:, :] = xn_ref[bb].astype(F32) * keep_next
        u = cw_ref[0:1, col * GROUP:(col + 1) * GROUP] * ext[pl.ds(HALO - CONV_LEFT, c), :]
        for t in range(1, CONV_W):
            u = u + cw_ref[t:t + 1, col * GROUP:(col + 1) * GROUP] * ext[pl.ds(HALO - CONV_LEFT + t, c), :]
        return _silu(u)

    def stack(x):
        return jnp.concatenate([x[:, h * HEAD_DV:(h + 1) * HEAD_DV] for h in range(nh)], axis=0)

    def l2n(x):
        return x * lax.rsqrt(jnp.sum(x * x, axis=-1, keepdims=True) + EPS)

    i2 = lax.broadcasted_iota(jnp.int32, (n, n), 0)
    j2 = lax.broadcasted_iota(jnp.int32, (n, n), 1)
    head_lo = (i2 >> 6) << 6
    if reverse:
        mask = jnp.where(j2 < head_lo + c, j2, -1) >= i2
    else:
        mask = jnp.where(j2 >= head_lo, j2, n) <= i2
    same32 = (i2 >> 5) == (j2 >> 5)
    same16 = (i2 >> 4) == (j2 >> 4)
    eye = jnp.where(i2 == j2, 1.0, 0.0)
    last = 0 if reverse else c - 1

    def each(fn, *lists):
        return [fn(*args) for args in zip(*lists)]

    def head_bcast(x, col):
        return jnp.concatenate([jnp.broadcast_to(x[:, col + h:col + h + 1], (c, HEAD_DV)) for h in range(nh)], axis=0)

    rows_bb = list(range(nb))
    q = [l2n(stack(conv_silu(bb, q_ref, qp_ref, qn_ref, 0))) * (HEAD_DV ** -0.5) for bb in rows_bb]
    k = [l2n(stack(conv_silu(bb, k_ref, kp_ref, kn_ref, 1))) for bb in rows_bb]
    v = [stack(conv_silu(bb, v_ref, vp_ref, vn_ref, 2)) for bb in rows_bb]

    tail = [t_ref[bb] for bb in rows_bb]
    beta = [head_bcast(_sigmoid(x), beta_col) for x in tail]
    g = [head_bcast(-jnp.exp(arow_ref[...]) * _softplus(x + dtrow_ref[...]), g_col) for x in tail]
    gc = [_dot_sel(cs_ref[...], x) for x in g]
    gc_last = [jnp.concatenate([jnp.broadcast_to(x[h * c + last:h * c + last + 1, :], (c, HEAD_DV))
                                for h in range(nh)], axis=0) for x in gc]
    diff = [jnp.concatenate([x, x], axis=1) - jnp.transpose(x)[0:1, :] for x in gc]
    gam = [jnp.where(mask, jnp.exp(jnp.where(mask, x, 0.0)), 0.0) for x in diff]

    kb = each(lambda a, b: a * b, k, beta)
    mm = each(lambda a, b, gm: jnp.where(i2 == j2, 0.0, _dot_nt(a, b) * gm), kb, k, gam)
    qk = each(lambda a, b, gm: _dot_nt(a, b) * gm, q, k, gam)
    egc = [jnp.exp(x) for x in gc]
    rhs = each(lambda a, b, kbx, e: jnp.concatenate([a * b, kbx * e], axis=1), v, beta, kb, egc)

    md = [jnp.where(same16, x, 0.0) for x in mm]
    m2 = each(_dot, md, md)
    t = [eye - x for x in md]
    t = each(lambda a, b: a + _dot(a, b), t, m2)
    m4 = each(_dot, m2, m2)
    t = each(lambda a, b: a + _dot(a, b), t, m4)
    m8 = each(_dot, m4, m4)
    t = each(lambda a, b: a + _dot(a, b), t, m8)
    c32 = each(lambda a, b: jnp.where(same32, a - b, 0.0), mm, md)
    ct = each(_dot, c32, t)
    t = each(lambda a, b: a - _dot(a, b), t, ct)
    c64 = [jnp.where(same32, 0.0, x) for x in mm]
    ct = each(_dot, c64, t)
    t = each(lambda a, b: a - _dot(a, b), t, ct)
    sol = each(_dot, t, rhs)

    qd = each(lambda a, e: a * e, q, egc)
    kd = each(lambda a, gl, x: a * jnp.exp(gl - x), k, gc_last, gc)
    g_last = [jnp.exp(x) for x in gc_last]
    v_new = [jnp.concatenate([sol[bb][h * c:(h + 1) * c, :HEAD_DV]
                              - _dot(sol[bb][h * c:(h + 1) * c, HEAD_DV:], st_scr[bb, h]) for h in range(nh)], axis=0)
             for bb in rows_bb]
    o_intra = each(_dot, qk, v_new)
    for h in range(nh):
        rows = slice(h * c, (h + 1) * c)
        for bb in rows_bb:
            state = st_scr[bb, h]
            o_ref[bb, :, h * HEAD_DV:(h + 1) * HEAD_DV] = (
                o_intra[bb][rows] + _dot(qd[bb][rows], state)).astype(o_ref.dtype)
            st_scr[bb, h] = state * g_last[bb][h * c:h * c + 1, :] + _dot(kd[bb][rows].T, v_new[bb][rows])


def _gdn_cumsum_matrix(reverse):
    c, nh = LIN_CHUNK, LIN_HEADS
    p = np.arange(c)
    tri = (p[None, :] >= p[:, None]) if reverse else (p[None, :] <= p[:, None])
    return np.kron(np.eye(nh), tri.astype(np.float32))


def _gdn(p, tail, conv_w, a_row, dt_row, *, n_ctx, reverse, direction, nb):
    bsz, l, _ = p.shape
    c = LIN_CHUNK
    n_chunks = l // c
    n_ctx_chunks = n_ctx // c
    hb = c // HALO
    n_hblocks = l // HALO
    cs = _doubled(_gdn_cumsum_matrix(reverse))

    def ch(s):
        return _scan_tile(s, n_ctx_chunks, n_chunks, reverse)

    def triple(col):
        return [pl.BlockSpec((nb, c, GROUP), lambda b, s: (b, ch(s), col)),
                pl.BlockSpec((nb, HALO, GROUP), lambda b, s: (b, jnp.maximum(ch(s) * hb - 1, 0), col)),
                pl.BlockSpec((nb, HALO, GROUP),
                             lambda b, s: (b, jnp.minimum((ch(s) + 1) * hb, n_hblocks - 1), col))]

    const2 = lambda b, s: (0, 0)
    return pl.pallas_call(
        functools.partial(_gdn_body, reverse=reverse, n_ctx_chunks=n_ctx_chunks, n_chunks=n_chunks,
                          beta_col=direction * LIN_HEADS, g_col=(2 + direction) * LIN_HEADS, nb=nb),
        grid=(bsz // nb, n_chunks),
        in_specs=[pl.BlockSpec(cs.shape, const2), pl.BlockSpec(conv_w.shape, const2),
                  pl.BlockSpec((1, LANES), const2), pl.BlockSpec((1, LANES), const2),
                  pl.BlockSpec((nb, c, LANES), lambda b, s: (b, ch(s), 0))]
                 + triple(5) + triple(6) + triple(7),
        out_specs=pl.BlockSpec((nb, c, GROUP), lambda b, s: (b, ch(s), 0)),
        out_shape=jax.ShapeDtypeStruct((bsz, l, GROUP), ACT),
        scratch_shapes=[pltpu.VMEM((3 * nb, c + 2 * HALO, GROUP), F32),
                        pltpu.VMEM((nb, LIN_HEADS, HEAD_DV, HEAD_DV), F32)],
        compiler_params=_cparams("parallel", "arbitrary"),
        name="gdn_rev" if reverse else "gdn_fwd",
    )(cs, conv_w, a_row, dt_row, tail, *([p] * 9))


def _post_body(x_ref, af_ref, ar_ref, ga_ref, bf_ref, br_ref, gb_ref, w_ref, ml_ref, mc_ref, o_ref,
               *, even, tm, n_ctx):
    t = pl.program_id(1)
    a = af_ref[...].astype(F32) + ar_ref[...].astype(F32)
    bsum = bf_ref[...].astype(F32) + br_ref[...].astype(F32)
    if even:
        ya = a * _gelu_tanh(ga_ref[...].astype(F32))
        yb = _head_norm(bsum, True) * _silu(gb_ref[...].astype(F32))
    else:
        ya = _head_norm(a, False) * _silu(ga_ref[...].astype(F32))
        yb = _head_norm(bsum, False) * _silu(gb_ref[...].astype(F32))
    half = w_ref.shape[0] // 2
    mix = jnp.dot(ya.astype(BF16), w_ref[0:half, :], preferred_element_type=F32)
    mix = mix + jnp.dot(yb.astype(BF16), w_ref[half:, :], preferred_element_type=F32)
    o_ref[...] = x_ref[...] + _row_select(t, tm, n_ctx, mc_ref, ml_ref, 2) * mix


def _post(x, mod, a_f, a_r, b_f, b_r, p, gate_a_col, gate_b_col, w_out, *, even, tm, n_ctx):
    bsz, l, d = x.shape
    ctx_row = bsz
    tok = lambda b, t: (b, t, 0)
    grp = pl.BlockSpec((None, tm, GROUP), tok)
    return pl.pallas_call(
        functools.partial(_post_body, even=even, tm=tm, n_ctx=n_ctx),
        grid=(bsz, l // tm),
        in_specs=[pl.BlockSpec((None, tm, d), tok), grp, grp,
                  pl.BlockSpec((None, tm, GROUP), lambda b, t: (b, t, gate_a_col)), grp, grp,
                  pl.BlockSpec((None, tm, GROUP), lambda b, t: (b, t, gate_b_col)),
                  pl.BlockSpec(w_out.shape, lambda b, t: (0, 0)),
                  pl.BlockSpec((None, 6, d), lambda b, t: (b, 0, 0)),
                  pl.BlockSpec((None, 6, d), lambda b, t: (ctx_row, 0, 0))],
        out_specs=pl.BlockSpec((None, tm, d), tok),
        out_shape=jax.ShapeDtypeStruct((bsz, l, d), F32),
        compiler_params=_cparams("parallel", "parallel"),
        name="post",
    )(x, a_f, a_r, p, b_f, b_r, p, w_out, mod, mod)


def _mlp_body(x_ref, g_ref, ml_ref, mc_ref, w1_ref, w2_ref, fg_ref, o_ref, h_scr, acc_scr, *, tm, n_ctx, final):
    t = pl.program_id(1)
    f = pl.program_id(2)
    if final:
        x_ref = x_ref.at[0]

    @pl.when(f == 0)
    def _():
        y = _rms(x_ref[...], g_ref[...])
        shift = _row_select(t, tm, n_ctx, mc_ref, ml_ref, 3)
        scale = _row_select(t, tm, n_ctx, mc_ref, ml_ref, 4)
        h_scr[...] = (y * (1.0 + scale) + shift).astype(BF16)
        acc_scr[...] = jnp.zeros_like(acc_scr)

    hid = jnp.maximum(jnp.dot(h_scr[...], w1_ref[...], preferred_element_type=F32), 0.0)
    acc_scr[...] += jnp.dot((hid * hid).astype(BF16), w2_ref[...], preferred_element_type=F32)

    @pl.when(f == pl.num_programs(2) - 1)
    def _():
        y = x_ref[...] + _row_select(t, tm, n_ctx, mc_ref, ml_ref, 5) * acc_scr[...]
        if final:
            y = _rms(y, fg_ref[...])
        o_ref[...] = y


def _mlp(x, mod, g, w1, w2, final_g, *, tm, tf, n_ctx, final):
    bsz, l, d = x.shape
    dff = w1.shape[1]
    ctx_row = bsz
    tok = lambda b, t, f: (b, t, 0)
    if final:
        l = l - n_ctx
        x_spec = pl.BlockSpec((pl.Element(1), pl.Element(tm), pl.Element(d)),
                              lambda b, t, f, off=n_ctx: (b, pl.multiple_of(off + t * tm, SUBLANES), 0))
        n_ctx = 0
    else:
        x_spec = pl.BlockSpec((None, tm, d), tok)
    return pl.pallas_call(
        functools.partial(_mlp_body, tm=tm, n_ctx=n_ctx, final=final),
        grid=(bsz, l // tm, dff // tf),
        in_specs=[x_spec,
                  pl.BlockSpec((1, d), lambda b, t, f: (0, 0)),
                  pl.BlockSpec((None, 6, d), lambda b, t, f: (b, 0, 0)),
                  pl.BlockSpec((None, 6, d), lambda b, t, f: (ctx_row, 0, 0)),
                  pl.BlockSpec((d, tf), lambda b, t, f: (0, f)),
                  pl.BlockSpec((tf, d), lambda b, t, f: (f, 0)),
                  pl.BlockSpec((1, d), lambda b, t, f: (0, 0))],
        out_specs=pl.BlockSpec((None, tm, d), tok),
        out_shape=jax.ShapeDtypeStruct((bsz, l, d), F32),
        scratch_shapes=[pltpu.VMEM((tm, d), BF16), pltpu.VMEM((tm, d), F32)],
        compiler_params=_cparams("parallel", "parallel", "arbitrary"),
        name="mlp",
    )(x, g.reshape(1, d), mod, mod, w1, w2, final_g.reshape(1, d))


def _token_tile(l):
    for div in (4, 8, 16, 17, 34):
        if l % div == 0 and (l // div) % 16 == 0 and l // div <= 1088:
            return l // div
    return l


def kernel(x, c, ctx, c_ctx, ada_w, ada_b, norm1_g, norm2_g, mix_w_out, mlp_w1, mlp_w2, ev_w_in, lru_conv_w,
           lru_conv_b, lru_wa, lru_ba, lru_wx, lru_bx, lru_lambda, ret_log_gamma, od_w_in, hg_lb_logits,
           gdn_conv_w, gdn_a_log, gdn_dt_bias, final_g):
    bsz, seq, d = x.shape
    n_ctx = ctx.shape[1]
    depth = ada_w.shape[0]
    l = n_ctx + seq
    tm = _token_tile(l)
    tf = 1024
    nb = 4 if bsz % 4 == 0 else 1

    mods = _mods(c, c_ctx, ada_w, ada_b)
    ret_dk = (ev_w_in.shape[2] - 4 * GROUP) // (2 * RET_HEADS)
    cos_t, sin_t = _rope_tables(n_ctx, seq, ret_dk)

    xs = jnp.concatenate([ctx, x], axis=1)
    for layer in range(depth):
        mod = mods[layer]
        w_out = mix_w_out[layer].astype(BF16)
        if layer % 2 == 0:
            e = layer // 2
            (p,) = _proj(xs, mod, norm1_g[layer], ev_w_in[e].astype(BF16), None, tm=tm, n_ctx=n_ctx)
            mix = []
            for dr, rev in enumerate((False, True)):
                mix.append((
                    _lru(p, lru_conv_w[e], lru_conv_b[e], _blockdiag_halves(lru_wa[e, dr]).astype(BF16),
                         lru_ba[e, dr], _blockdiag_halves(lru_wx[e, dr]).astype(BF16), lru_bx[e, dr],
                         lru_lambda[e, dr], n_ctx=n_ctx, reverse=rev),
                    _retention(p, cos_t, sin_t, ret_log_gamma[e, dr], n_ctx=n_ctx, reverse=rev, nb=nb)))
            gate_cols = (1, 4)
        else:
            o = layer // 2
            n_main = 9 * GROUP
            w_in = od_w_in[o]
            w_tail = jnp.pad(w_in[:, n_main:], ((0, 0), (0, LANES - (w_in.shape[1] - n_main)))).astype(BF16)
            p, tail = _proj(xs, mod, norm1_g[layer], w_in[:, :n_main].astype(BF16), w_tail, tm=tm, n_ctx=n_ctx)
            mix = []
            for dr, rev in enumerate((False, True)):
                g_col = (2 + dr) * LIN_HEADS
                a_row = jnp.zeros((1, LANES), F32).at[0, g_col:g_col + LIN_HEADS].set(gdn_a_log[o, dr])
                dt_row = jnp.zeros((1, LANES), F32).at[0, g_col:g_col + LIN_HEADS].set(gdn_dt_bias[o, dr])
                mix.append((
                    _gla(p, hg_lb_logits[dr], n_ctx=n_ctx, reverse=rev, layer=o, nb=nb),
                    _gdn(p, tail, gdn_conv_w[o], a_row, dt_row, n_ctx=n_ctx, reverse=rev, direction=dr, nb=nb)))
            gate_cols = (4, 8)
        xs = _post(xs, mod, mix[0][0], mix[1][0], mix[0][1], mix[1][1], p, gate_cols[0], gate_cols[1], w_out,
                   even=layer % 2 == 0, tm=tm, n_ctx=n_ctx)
        last = layer == depth - 1
        xs = _mlp(xs, mod, norm2_g[layer], mlp_w1[layer].astype(BF16), mlp_w2[layer].astype(BF16), final_g,
                  tm=_token_tile(seq) if last else tm, tf=tf, n_ctx=n_ctx, final=last)
    return xs
```

```python
import functools

import numpy as np
import jax
import jax.numpy as jnp
from jax import lax
from jax.experimental import pallas as pl
from jax.experimental.pallas import tpu as pltpu

F32 = jnp.float32
BF16 = jnp.bfloat16

EPS = 1e-6
GRID_W = 64
CONV_W = 4
CONV_LEFT = CONV_W // 2
LRU_BLOCKS = 8
LRU_C = 8.0
RET_HEADS = 4
RET_CHUNK = 256
ROPE_BASE = 10000.0
LIN_HEADS = 4
GLA_CHUNK = 128
LIN_CHUNK = 64
HEAD_DV = 128

LANES = 128
SUBLANES = 8
V7X_VMEM_BYTES = 64 * 1024 * 1024
VMEM_LIMIT = (V7X_VMEM_BYTES * 3) // 4
GROUP = 512
PROJ_MAX_TN = 2560

HALO = 2 * SUBLANES
ACT = jnp.bfloat16


def _cparams(*sem):
    return pltpu.CompilerParams(dimension_semantics=sem, vmem_limit_bytes=VMEM_LIMIT)


def _sigmoid(x):
    return 1.0 / (1.0 + jnp.exp(-x))


def _silu(x):
    return x * _sigmoid(x)


def _softplus(x):
    return jnp.maximum(x, 0.0) + jnp.log(1.0 + jnp.exp(-jnp.abs(x)))


def _gelu_tanh(x):
    return 0.5 * x * (1.0 + jnp.tanh(0.7978845608028654 * (x + 0.044715 * x * x * x)))


def _dot(a, b):
    return jnp.dot(a.astype(BF16), b.astype(BF16), preferred_element_type=F32)


def _dot_nt(a, b):
    return lax.dot_general(a.astype(BF16), b.astype(BF16), (((1,), (1,)), ((), ())),
                           preferred_element_type=F32)


def _split_bf16(x):
    hi = x.astype(BF16)
    lo = (x - hi.astype(F32)).astype(BF16)
    return hi, lo


def _dot_sel(m2, x):
    hi, lo = _split_bf16(x)
    return jnp.dot(m2, jnp.concatenate([hi, lo], axis=0), preferred_element_type=F32)


def _doubled(m):
    return jnp.asarray(np.concatenate([m, m], axis=1), BF16)


def _dot3(a, b):
    ah, al = _split_bf16(a)
    bh, bl = _split_bf16(b)
    return (jnp.dot(ah, bh, preferred_element_type=F32) + jnp.dot(ah, bl, preferred_element_type=F32)
            + jnp.dot(al, bh, preferred_element_type=F32))


def _row_select(t, tm, n_ctx, mc_ref, ml_ref, idx):
    rows = t * tm + lax.broadcasted_iota(jnp.int32, (tm, 1), 0)
    return jnp.where(rows < n_ctx, mc_ref[idx:idx + 1, :], ml_ref[idx:idx + 1, :])


def _rms(x, g):
    return x * lax.rsqrt(jnp.mean(x * x, axis=-1, keepdims=True) + EPS) * g


def _head_norm(o, center):
    outs = []
    for h in range(o.shape[-1] // HEAD_DV):
        oh = o[:, h * HEAD_DV:(h + 1) * HEAD_DV]
        if center:
            oh = oh - jnp.mean(oh, axis=-1, keepdims=True)
        outs.append(oh * lax.rsqrt(jnp.mean(oh * oh, axis=-1, keepdims=True) + EPS))
    return jnp.concatenate(outs, axis=-1)


def _scan_tile(step, n_first, n_total, reverse):
    if not reverse:
        return step
    return jnp.where(step < n_first, n_first - 1 - step, n_total - 1 - (step - n_first))


def _mods_body(s_ref, w_ref, b_ref, o_ref):
    s = _silu(s_ref[...])
    o_ref[...] = _dot3(s, w_ref[...]) + b_ref[...]


def _mods(c, c_ctx, ada_w, ada_b):
    depth, d, n6 = ada_w.shape
    bsz = c.shape[0]
    rows = -(-(bsz + 1) // SUBLANES) * SUBLANES
    s = jnp.zeros((rows, d), F32).at[:bsz].set(c).at[bsz].set(c_ctx)
    tn = 1024
    out = pl.pallas_call(
        _mods_body,
        grid=(depth, n6 // tn),
        in_specs=[pl.BlockSpec((rows, d), lambda l, n: (0, 0)),
                  pl.BlockSpec((None, d, tn), lambda l, n: (l, 0, n)),
                  pl.BlockSpec((None, 1, tn), lambda l, n: (l, 0, n))],
        out_specs=pl.BlockSpec((None, rows, tn), lambda l, n: (l, 0, n)),
        out_shape=jax.ShapeDtypeStruct((depth, rows, n6), F32),
        compiler_params=_cparams("parallel", "parallel"),
        name="mods",
    )(s, ada_w, ada_b.reshape(depth, 1, n6))
    return out.reshape(depth, rows, 6, d)


def _proj_body(*refs, tm, n_ctx, has_tail):
    if has_tail:
        x_ref, ml_ref, mc_ref, g_ref, w_ref, wt_ref, o_ref, ot_ref, h_scr = refs
    else:
        x_ref, ml_ref, mc_ref, g_ref, w_ref, o_ref, h_scr = refs
    t = pl.program_id(1)

    @pl.when(pl.program_id(2) == 0)
    def _():
        y = _rms(x_ref[...], g_ref[...])
        shift = _row_select(t, tm, n_ctx, mc_ref, ml_ref, 0)
        scale = _row_select(t, tm, n_ctx, mc_ref, ml_ref, 1)
        h_scr[...] = (y * (1.0 + scale) + shift).astype(BF16)
        if has_tail:
            ot_ref[...] = jnp.dot(h_scr[...], wt_ref[...], preferred_element_type=F32)

    o_ref[...] = jnp.dot(h_scr[...], w_ref[...], preferred_element_type=F32).astype(o_ref.dtype)


def _proj(x, mod, g, w, w_tail, *, tm, n_ctx):
    bsz, l, d = x.shape
    n_out = w.shape[1]
    tn = max(t for t in range(LANES, PROJ_MAX_TN + 1, LANES) if n_out % t == 0)
    ng = n_out // tn
    ctx_row = bsz
    in_specs = [pl.BlockSpec((None, tm, d), lambda b, t, n: (b, t, 0)),
                pl.BlockSpec((None, 6, d), lambda b, t, n: (b, 0, 0)),
                pl.BlockSpec((None, 6, d), lambda b, t, n: (ctx_row, 0, 0)),
                pl.BlockSpec((1, d), lambda b, t, n: (0, 0)),
                pl.BlockSpec((d, tn), lambda b, t, n: (0, n))]
    out_specs = [pl.BlockSpec((None, tm, tn), lambda b, t, n: (b, t, n))]
    out_shape = [jax.ShapeDtypeStruct((bsz, l, n_out), ACT)]
    args = [x, mod, mod, g.reshape(1, d), w]
    if w_tail is not None:
        in_specs.append(pl.BlockSpec((d, LANES), lambda b, t, n: (0, 0)))
        out_specs.append(pl.BlockSpec((None, tm, LANES), lambda b, t, n: (b, t, 0)))
        out_shape.append(jax.ShapeDtypeStruct((bsz, l, LANES), F32))
        args.append(w_tail)
    outs = pl.pallas_call(
        functools.partial(_proj_body, tm=tm, n_ctx=n_ctx, has_tail=w_tail is not None),
        grid=(bsz, l // tm, ng),
        in_specs=in_specs, out_specs=out_specs, out_shape=out_shape,
        scratch_shapes=[pltpu.VMEM((tm, d), BF16)],
        compiler_params=_cparams("parallel", "parallel", "arbitrary"),
        name="proj",
    )(*args)
    return outs


def _lru_body(x_ref, xp_ref, xn_ref, cw_ref, cb_ref, wa_ref, ba_ref, wx_ref, bx_ref, lam_ref,
              o_ref, ext_scr, a_scr, b_scr, h_scr, *, tl, n_ctx_tiles, n_tiles, reverse):
    step = pl.program_id(0)
    tile = _scan_tile(step, n_ctx_tiles, n_tiles, reverse)
    bsz = x_ref.shape[0]
    width = x_ref.shape[2]
    half = width // 2
    seg_first = jnp.logical_or(tile == 0, tile == n_ctx_tiles)
    seg_last = jnp.logical_or(tile == n_ctx_tiles - 1, tile == n_tiles - 1)
    keep_prev = jnp.where(seg_first, 0.0, 1.0)
    keep_next = jnp.where(seg_last, 0.0, 1.0)
    neg_c_sp = -LRU_C * _softplus(-lam_ref[...])

    @pl.when(step == 0)
    def _():
        h_scr[...] = jnp.zeros_like(h_scr)

    def gates(b, carry):
        ext_scr[0:HALO, :] = xp_ref[b].astype(F32) * keep_prev
        ext_scr[HALO:HALO + tl, :] = x_ref[b].astype(F32)
        ext_scr[HALO + tl:, :] = xn_ref[b].astype(F32) * keep_next
        u = cb_ref[...] + cw_ref[0:1, :] * ext_scr[pl.ds(HALO - CONV_LEFT, tl), :]
        for k in range(1, CONV_W):
            u = u + cw_ref[k:k + 1, :] * ext_scr[pl.ds(HALO - CONV_LEFT + k, tl), :]
        ub = u.astype(BF16)
        r_pre = jnp.concatenate(
            [jnp.dot(ub[:, :half], wa_ref[0], preferred_element_type=F32),
             jnp.dot(ub[:, half:], wa_ref[1], preferred_element_type=F32)], axis=-1) + ba_ref[...]
        i_pre = jnp.concatenate(
            [jnp.dot(ub[:, :half], wx_ref[0], preferred_element_type=F32),
             jnp.dot(ub[:, half:], wx_ref[1], preferred_element_type=F32)], axis=-1) + bx_ref[...]
        a = jnp.exp(neg_c_sp * _sigmoid(r_pre))
        a_scr[b] = a
        b_scr[b] = jnp.sqrt(1.0 - a * a) * (_sigmoid(i_pre) * u)
        return carry

    lax.fori_loop(0, bsz, gates, 0)

    def scan_rows(j, hs):
        idx = (tl - 1 - j) if reverse else j
        new = []
        for b in range(bsz):
            h = a_scr[b, pl.ds(idx, 1), :] * hs[b] + b_scr[b, pl.ds(idx, 1), :]
            b_scr[b, pl.ds(idx, 1), :] = h
            new.append(h)
        return tuple(new)

    hs = tuple(h_scr[b, 0:1, :] for b in range(bsz))
    hs = lax.fori_loop(0, tl, scan_rows, hs)
    for b in range(bsz):
        h_scr[b, 0:1, :] = hs[b]
    o_ref[...] = b_scr[...].astype(o_ref.dtype)


def _lru(p, conv_w, conv_b, wa, ba, wx, bx, lam, *, n_ctx, reverse, tl=256):
    bsz, l, _ = p.shape
    width = GROUP
    n_tiles = l // tl
    n_ctx_tiles = n_ctx // tl
    hb = tl // HALO
    n_hblocks = l // HALO

    def tile_of(s):
        return _scan_tile(s, n_ctx_tiles, n_tiles, reverse)

    vec = pl.BlockSpec((1, width), lambda s: (0, 0))
    mat = pl.BlockSpec((2, width // 2, width // 2), lambda s: (0, 0, 0))
    return pl.pallas_call(
        functools.partial(_lru_body, tl=tl, n_ctx_tiles=n_ctx_tiles, n_tiles=n_tiles, reverse=reverse),
        grid=(n_tiles,),
        in_specs=[pl.BlockSpec((bsz, tl, width), lambda s: (0, tile_of(s), 0)),
                  pl.BlockSpec((bsz, HALO, width), lambda s: (0, jnp.maximum(tile_of(s) * hb - 1, 0), 0)),
                  pl.BlockSpec((bsz, HALO, width),
                               lambda s: (0, jnp.minimum((tile_of(s) + 1) * hb, n_hblocks - 1), 0)),
                  pl.BlockSpec((CONV_W, width), lambda s: (0, 0)), vec, mat, vec, mat, vec, vec],
        out_specs=pl.BlockSpec((bsz, tl, width), lambda s: (0, tile_of(s), 0)),
        out_shape=jax.ShapeDtypeStruct((bsz, l, width), ACT),
        scratch_shapes=[pltpu.VMEM((tl + 2 * HALO, width), F32),
                        pltpu.VMEM((bsz, tl, width), F32),
                        pltpu.VMEM((bsz, tl, width), F32),
                        pltpu.VMEM((bsz, SUBLANES, width), F32)],
        compiler_params=_cparams("arbitrary"),
        name="lru_rev" if reverse else "lru_fwd",
    )(p, p, p, conv_w, conv_b.reshape(1, width), wa, ba.reshape(1, width), wx, bx.reshape(1, width),
      lam.reshape(1, width))


def _blockdiag_halves(w):
    nb, n, _ = w.shape
    per = nb // 2
    w = w.reshape(2, per, n, n)
    eye = jnp.eye(per, dtype=w.dtype)
    out = w[:, :, :, None, :] * eye[None, :, None, :, None]
    return out.reshape(2, per * n, per * n)


def _ret_body(lg_ref, q_ref, k_ref, v_ref, cos_ref, sin_ref, o_ref, dec_scr, qd_scr, kd_scr, cd_scr, st_scr,
              *, reverse, dk, nb):
    c = RET_CHUNK
    step = pl.program_id(1)

    @pl.when(step == 0)
    def _():
        st_scr[...] = jnp.zeros_like(st_scr)
        i = lax.broadcasted_iota(jnp.int32, (c, c), 0)
        j = lax.broadcasted_iota(jnp.int32, (c, c), 1)
        diff = (j - i) if reverse else (i - j)
        row = lax.broadcasted_iota(jnp.int32, (c, HEAD_DV), 0)
        pos = ((c - 1 - row) if reverse else row).astype(F32)
        dpos = jnp.maximum(diff, 0).astype(F32)
        for h in range(RET_HEADS):
            lg = lg_ref[h]
            dec_scr[h] = jnp.where(diff >= 0, jnp.exp(lg * dpos), 0.0)
            qd_scr[h] = jnp.exp(lg * (pos + 1.0))
            kd_scr[h] = jnp.exp(lg * (c - 1.0 - pos))
            cd_scr[h] = jnp.exp(jnp.full((SUBLANES, LANES), c, F32) * lg)

    def rope(x):
        width = x.shape[-1]
        lane = lax.broadcasted_iota(jnp.int32, x.shape, 1)
        first = (lane & (dk - 1)) < dk // 2
        rot = jnp.where(first, pltpu.roll(x, width - dk // 2, axis=1), pltpu.roll(x, dk // 2, axis=1))
        return x * cos_ref[...] + rot * sin_ref[...]

    q = [rope(q_ref[bb].astype(F32)) for bb in range(nb)]
    k = [rope(k_ref[bb].astype(F32)) * (dk ** -0.5) for bb in range(nb)]
    pairs = [(bb, h) for bb in range(nb) for h in range(RET_HEADS)]
    qh = [q[bb][:, h * dk:(h + 1) * dk] for bb, h in pairs]
    kh = [k[bb][:, h * dk:(h + 1) * dk] for bb, h in pairs]
    vh = [v_ref[bb, :, h * HEAD_DV:(h + 1) * HEAD_DV] for bb, h in pairs]
    s = [_dot_nt(qh[i], kh[i]) * dec_scr[h] for i, (bb, h) in enumerate(pairs)]
    state = [st_scr[bb, h] for bb, h in pairs]
    o_x = [_dot(qh[i], state[i]) * qd_scr[h] for i, (bb, h) in enumerate(pairs)]
    o_in = [_dot(s[i], vh[i]) for i in range(len(pairs))]
    kv = [_dot((kh[i] * kd_scr[h][:, :dk]).T, vh[i]) for i, (bb, h) in enumerate(pairs)]
    for i, (bb, h) in enumerate(pairs):
        st_scr[bb, h] = state[i] * cd_scr[h][0:1, :] + kv[i]
        o_ref[bb, :, h * HEAD_DV:(h + 1) * HEAD_DV] = (o_in[i] + o_x[i]).astype(o_ref.dtype)


def _retention(p, cos_t, sin_t, log_gamma, *, n_ctx, reverse, nb):
    bsz, l, _ = p.shape
    c = RET_CHUNK
    qk_w = cos_t.shape[1]
    dk = qk_w // RET_HEADS
    n_chunks = l // c
    n_ctx_chunks = n_ctx // c

    def ch(s):
        return _scan_tile(s, n_ctx_chunks, n_chunks, reverse)

    q_col = (2 * GROUP) // qk_w
    return pl.pallas_call(
        functools.partial(_ret_body, reverse=reverse, dk=dk, nb=nb),
        grid=(bsz // nb, n_chunks),
        in_specs=[pl.BlockSpec(memory_space=pltpu.SMEM),
                  pl.BlockSpec((nb, c, qk_w), lambda b, s: (b, ch(s), q_col)),
                  pl.BlockSpec((nb, c, qk_w), lambda b, s: (b, ch(s), q_col + 1)),
                  pl.BlockSpec((nb, c, GROUP), lambda b, s: (b, ch(s), 3)),
                  pl.BlockSpec((c, qk_w), lambda b, s: (ch(s), 0)),
                  pl.BlockSpec((c, qk_w), lambda b, s: (ch(s), 0))],
        out_specs=pl.BlockSpec((nb, c, GROUP), lambda b, s: (b, ch(s), 0)),
        out_shape=jax.ShapeDtypeStruct((bsz, l, GROUP), ACT),
        scratch_shapes=[pltpu.VMEM((RET_HEADS, c, c), F32),
                        pltpu.VMEM((RET_HEADS, c, HEAD_DV), F32),
                        pltpu.VMEM((RET_HEADS, c, HEAD_DV), F32),
                        pltpu.VMEM((RET_HEADS, SUBLANES, LANES), F32),
                        pltpu.VMEM((nb, RET_HEADS, dk, HEAD_DV), F32)],
        compiler_params=_cparams("parallel", "arbitrary"),
        name="ret_rev" if reverse else "ret_fwd",
    )(log_gamma, p, p, p, cos_t, sin_t)


def _rope_tables(n_ctx, seq, dk):
    n_freq = dk // 4
    inv = jnp.power(ROPE_BASE, -jnp.arange(n_freq, dtype=F32) / n_freq)
    rows = seq // GRID_W
    r = jnp.arange(rows, dtype=F32)
    col = jnp.arange(GRID_W, dtype=F32)
    row_ang = jnp.broadcast_to(r[:, None, None] * inv, (rows, GRID_W, n_freq))
    col_ang = jnp.broadcast_to(col[None, :, None] * inv, (rows, GRID_W, n_freq))
    ang = jnp.concatenate([row_ang, col_ang], axis=-1).reshape(rows * GRID_W, 2 * n_freq)
    cos, sin = jnp.cos(ang), jnp.sin(ang)
    cos_h = jnp.concatenate([cos, cos], axis=-1)
    sin_h = jnp.concatenate([-sin, sin], axis=-1)
    cos_t = jnp.concatenate([jnp.ones((n_ctx, dk), F32), cos_h], axis=0)
    sin_t = jnp.concatenate([jnp.zeros((n_ctx, dk), F32), sin_h], axis=0)
    return jnp.tile(cos_t, (1, RET_HEADS)), jnp.tile(sin_t, (1, RET_HEADS))


_GLA_LEVELS = tuple(GLA_CHUNK >> (i + 1) for i in range(GLA_CHUNK.bit_length() - 1))


def _gla_matrices(reverse):
    c = GLA_CHUNK
    p = np.arange(c)
    mats = [(p[None, :] <= p[:, None]), (p[None, :] > p[:, None])]
    for s in _GLA_LEVELS:
        bs = (p // s) * s
        right = ((p // s) % 2 == 1)
        m_r = (p[None, :] > bs[:, None]) & (p[None, :] <= p[:, None])
        m_l = (p[None, :] > p[:, None]) & (p[None, :] <= (bs + s)[:, None])
        mats.append(np.where(right[:, None], m_r, m_l))
    m = np.stack(mats).astype(np.float32)
    if reverse:
        m = m[:, ::-1, ::-1]
    return m.reshape(len(mats) * c, c)


def _gla_body(m_ref, lb_ref, q_ref, f_ref, v_ref, o_ref, st_scr, *, reverse, layer, nb):
    c = GLA_CHUNK
    step = pl.program_id(1)

    @pl.when(step == 0)
    def _():
        st_scr[...] = jnp.zeros_like(st_scr)

    lg = lb_ref[...]
    e = jnp.exp(lg - jnp.max(lg, axis=0, keepdims=True))
    sm = e / jnp.sum(e, axis=0, keepdims=True)
    lb = jnp.sum(sm[0:layer + 1, :], axis=0, keepdims=True) - sm[0:1, :]

    ri = lax.broadcasted_iota(jnp.int32, (c, 1), 0)
    i2 = lax.broadcasted_iota(jnp.int32, (c, c), 0)
    j2 = lax.broadcasted_iota(jnp.int32, (c, c), 1)
    if reverse:
        ri, i2, j2 = c - 1 - ri, c - 1 - i2, c - 1 - j2
    last = 0 if reverse else c - 1
    width = q_ref.shape[-1]
    nh = width // HEAD_DV

    rows_bb = list(range(nb))
    pairs = [(bb, h) for bb in rows_bb for h in range(nh)]

    def head(x, h):
        return x[:, h * HEAD_DV:(h + 1) * HEAD_DV]

    f = [lb + (1.0 - lb) * _sigmoid(f_ref[bb].astype(F32)) for bb in rows_bb]
    k = [1.0 - x for x in f]
    q = [_silu(q_ref[bb].astype(F32)) for bb in rows_bb]
    ex = [_dot_sel(m_ref[...], jnp.log(x)) for x in f]
    a_last = [jnp.exp(x[last:last + 1, :]) for x in ex]
    qa = [q[bb] * jnp.exp(ex[bb][0:c, :]) for bb in rows_bb]
    kl = [k[bb] * jnp.exp(ex[bb][c:2 * c, :]) for bb in rows_bb]

    qb = [x.astype(BF16) for x in q]
    kb = [x.astype(BF16) for x in k]
    scores = [jnp.where(i2 == j2, _dot_nt(head(qb[bb], h), head(kb[bb], h)), 0.0) for bb, h in pairs]
    for lvl, s in enumerate(_GLA_LEVELS):
        sh = s.bit_length() - 1
        right = jnp.broadcast_to(((ri >> sh) & 1).astype(F32), (c, width)).astype(BF16) > 0
        x = [jnp.where(right, qb[bb], kb[bb]) * jnp.exp(ex[bb][(2 + lvl) * c:(3 + lvl) * c, :].astype(BF16))
             for bb in rows_bb]
        bi, bj = i2 >> sh, j2 >> sh
        valid = bj == jnp.where((bi & 1) == 1, bi - 1, -1)
        gram = [_dot_nt(head(x[bb], h), head(x[bb], h)) for bb, h in pairs]
        scores = [sc + jnp.where(valid, gm, 0.0) for sc, gm in zip(scores, gram)]

    state_t = [st_scr[bb, h] for bb, h in pairs]
    o_intra = [_dot(scores[i], head(v_ref[bb], h)) for i, (bb, h) in enumerate(pairs)]
    o_inter = [_dot_nt(head(qa[bb], h), state_t[i]) for i, (bb, h) in enumerate(pairs)]
    kv = [_dot(head(v_ref[bb], h).astype(F32).T, head(kl[bb], h)) for bb, h in pairs]
    for i, (bb, h) in enumerate(pairs):
        o_ref[bb, :, h * HEAD_DV:(h + 1) * HEAD_DV] = (o_intra[i] + o_inter[i]).astype(o_ref.dtype)
        st_scr[bb, h] = state_t[i] * head(a_last[bb], h) + kv[i]


def _gla(p, lb_logits, *, n_ctx, reverse, layer, nb):
    bsz, l, _ = p.shape
    c = GLA_CHUNK
    n_chunks = l // c
    n_ctx_chunks = n_ctx // c
    m = _doubled(_gla_matrices(reverse))

    def ch(s):
        return _scan_tile(s, n_ctx_chunks, n_chunks, reverse)

    f_col = 2 if reverse else 1
    return pl.pallas_call(
        functools.partial(_gla_body, reverse=reverse, layer=layer, nb=nb),
        grid=(bsz // nb, n_chunks),
        in_specs=[pl.BlockSpec(m.shape, lambda b, s: (0, 0)),
                  pl.BlockSpec(lb_logits.shape, lambda b, s: (0, 0)),
                  pl.BlockSpec((nb, c, GROUP), lambda b, s: (b, ch(s), 0)),
                  pl.BlockSpec((nb, c, GROUP), lambda b, s: (b, ch(s), f_col)),
                  pl.BlockSpec((nb, c, GROUP), lambda b, s: (b, ch(s), 3))],
        out_specs=pl.BlockSpec((nb, c, GROUP), lambda b, s: (b, ch(s), 0)),
        out_shape=jax.ShapeDtypeStruct((bsz, l, GROUP), ACT),
        scratch_shapes=[pltpu.VMEM((nb, LIN_HEADS, HEAD_DV, HEAD_DV), F32)],
        compiler_params=_cparams("parallel", "arbitrary"),
        name="gla_rev" if reverse else "gla_fwd",
    )(m, lb_logits, p, p, p)


def _gdn_prep_body(cw_ref, q_ref, qp_ref, qn_ref, k_ref, kp_ref, kn_ref, v_ref, vp_ref, vn_ref,
                   qo_ref, ko_ref, vo_ref, ext_scr, *, tp, n_ctx_tiles, n_tiles):
    c = LIN_CHUNK
    nh = LIN_HEADS
    tile = pl.program_id(1)
    seg_first = jnp.logical_or(tile == 0, tile == n_ctx_tiles)
    seg_last = jnp.logical_or(tile == n_ctx_tiles - 1, tile == n_tiles - 1)
    keep_prev = jnp.where(seg_first, 0.0, 1.0)
    keep_next = jnp.where(seg_last, 0.0, 1.0)

    def conv_silu(x_ref, xp_ref, xn_ref, col):
        ext = ext_scr.at[col]
        ext[0:HALO, :] = xp_ref[...].astype(F32) * keep_prev
        ext[HALO:HALO + tp, :] = x_ref[...].astype(F32)
        ext[HALO + tp:, :] = xn_ref[...].astype(F32) * keep_next
        u = cw_ref[0:1, col * GROUP:(col + 1) * GROUP] * ext[pl.ds(HALO - CONV_LEFT, tp), :]
        for t in range(1, CONV_W):
            u = u + cw_ref[t:t + 1, col * GROUP:(col + 1) * GROUP] * ext[pl.ds(HALO - CONV_LEFT + t, tp), :]
        return _silu(u)

    def l2n(x):
        return x * lax.rsqrt(jnp.sum(x * x, axis=-1, keepdims=True) + EPS)

    def emit(out_ref, x, norm, scale):
        for h in range(nh):
            xh = x[:, h * HEAD_DV:(h + 1) * HEAD_DV]
            if norm:
                xh = l2n(xh) * scale
            for j in range(tp // c):
                out_ref[j, h * c:(h + 1) * c, :] = xh[j * c:(j + 1) * c, :].astype(out_ref.dtype)

    emit(qo_ref, conv_silu(q_ref, qp_ref, qn_ref, 0), True, HEAD_DV ** -0.5)
    emit(ko_ref, conv_silu(k_ref, kp_ref, kn_ref, 1), True, 1.0)
    emit(vo_ref, conv_silu(v_ref, vp_ref, vn_ref, 2), False, 1.0)


def _gdn_prep(p, conv_w, *, n_ctx, tp=256):
    bsz, l, _ = p.shape
    c = LIN_CHUNK
    n_tiles = l // tp
    hb = tp // HALO
    n_hblocks = l // HALO

    def triple(col):
        return [pl.BlockSpec((None, tp, GROUP), lambda b, t: (b, t, col)),
                pl.BlockSpec((None, HALO, GROUP), lambda b, t: (b, jnp.maximum(t * hb - 1, 0), col)),
                pl.BlockSpec((None, HALO, GROUP), lambda b, t: (b, jnp.minimum((t + 1) * hb, n_hblocks - 1), col))]

    out_spec = pl.BlockSpec((None, tp // c, LIN_HEADS * c, HEAD_DV), lambda b, t: (b, t, 0, 0))
    out_sds = jax.ShapeDtypeStruct((bsz, l // c, LIN_HEADS * c, HEAD_DV), ACT)
    return pl.pallas_call(
        functools.partial(_gdn_prep_body, tp=tp, n_ctx_tiles=n_ctx // tp, n_tiles=n_tiles),
        grid=(bsz, n_tiles),
        in_specs=[pl.BlockSpec(conv_w.shape, lambda b, t: (0, 0))] + triple(5) + triple(6) + triple(7),
        out_specs=[out_spec] * 3, out_shape=[out_sds] * 3,
        scratch_shapes=[pltpu.VMEM((3, tp + 2 * HALO, GROUP), F32)],
        compiler_params=_cparams("parallel", "parallel"),
        name="gdn_prep",
    )(conv_w, *([p] * 9))


def _gdn_body(cs_ref, arow_ref, dtrow_ref, t_ref, q_ref, k_ref, v_ref, o_ref, st_scr,
              *, reverse, beta_col, g_col, nb):
    c = LIN_CHUNK
    nh = LIN_HEADS
    n = nh * c
    step = pl.program_id(1)

    @pl.when(step == 0)
    def _():
        st_scr[...] = jnp.zeros_like(st_scr)

    i2 = lax.broadcasted_iota(jnp.int32, (c, n), 0)
    lane = lax.broadcasted_iota(jnp.int32, (c, n), 1)
    j2 = lane & (c - 1)
    grp = lane >> 6
    mask = (j2 >= i2) if reverse else (j2 <= i2)
    same32 = (i2 >> 5) == (j2 >> 5)
    same16 = (i2 >> 4) == (j2 >> 4)
    eye = jnp.where(i2 == j2, 1.0, 0.0)
    grp_rows = [jnp.where((lax.broadcasted_iota(jnp.int32, (1, n), 1) >> 6) == h, 1.0, 0.0).astype(BF16)
                for h in range(nh)]
    last = 0 if reverse else c - 1

    def each(fn, *lists):
        return [fn(*args) for args in zip(*lists)]

    def head_bcast(x, col):
        return jnp.concatenate([jnp.broadcast_to(x[:, col + h:col + h + 1], (c, HEAD_DV)) for h in range(nh)], axis=0)

    def side_by_side(x):
        out = x[(nh - 1) * c:nh * c, :]
        for h in range(nh - 2, -1, -1):
            out = jnp.where(grp == h, x[h * c:(h + 1) * c, :], out)
        return out

    def block_diag(x):
        xb = x.astype(BF16)
        return jnp.concatenate([xb * grp_rows[h] for h in range(nh)], axis=0)

    def dot_sbs(x, y_bd):
        return jnp.dot(x.astype(BF16), y_bd, preferred_element_type=F32)

    rows_bb = list(range(nb))
    q = [q_ref[bb].astype(F32) for bb in rows_bb]
    k = [k_ref[bb].astype(F32) for bb in rows_bb]
    v = [v_ref[bb].astype(F32) for bb in rows_bb]

    tail = [t_ref[bb] for bb in rows_bb]
    beta = [head_bcast(_sigmoid(x), beta_col) for x in tail]
    g = [head_bcast(-jnp.exp(arow_ref[...]) * _softplus(x + dtrow_ref[...]), g_col) for x in tail]
    gc = [_dot_sel(cs_ref[...], x) for x in g]
    gc_last = [jnp.concatenate([jnp.broadcast_to(x[h * c + last:h * c + last + 1, :], (c, HEAD_DV))
                                for h in range(nh)], axis=0) for x in gc]
    diff = [side_by_side(jnp.concatenate([x, x], axis=1)) - jnp.transpose(x)[0:1, :] for x in gc]
    gam = [jnp.where(mask, jnp.exp(jnp.where(mask, x, 0.0)), 0.0) for x in diff]

    kb = each(lambda a, b: a * b, k, beta)
    mm = each(lambda a, b, gm: jnp.where(i2 == j2, 0.0, side_by_side(_dot_nt(a, b)) * gm), kb, k, gam)
    qk = each(lambda a, b, gm: side_by_side(_dot_nt(a, b)) * gm, q, k, gam)
    egc = [jnp.exp(x) for x in gc]
    rhs = each(lambda a, b, kbx, e: jnp.concatenate([a * b, kbx * e], axis=1), v, beta, kb, egc)

    md = [jnp.where(same16, x, 0.0) for x in mm]
    bd = [block_diag(x) for x in md]
    m2 = each(dot_sbs, md, bd)
    t = [eye - x for x in md]
    bd = [block_diag(x) for x in m2]
    t = each(lambda a, b: a + dot_sbs(a, b), t, bd)
    m4 = each(dot_sbs, m2, bd)
    bd = [block_diag(x) for x in m4]
    t = each(lambda a, b: a + dot_sbs(a, b), t, bd)
    m8 = each(dot_sbs, m4, bd)
    t = each(lambda a, b: a + dot_sbs(a, block_diag(b)), t, m8)
    c32 = each(lambda a, b: jnp.where(same32, a - b, 0.0), mm, md)
    ct = each(lambda a, b: dot_sbs(a, block_diag(b)), c32, t)
    t = each(lambda a, b: a - dot_sbs(a, block_diag(b)), t, ct)
    c64 = [jnp.where(same32, 0.0, x) for x in mm]
    ct = each(lambda a, b: dot_sbs(a, block_diag(b)), c64, t)
    t = each(lambda a, b: a - dot_sbs(a, block_diag(b)), t, ct)
    sol = each(lambda a, b: jnp.dot(block_diag(a), b.astype(BF16), preferred_element_type=F32), t, rhs)

    qd = each(lambda a, e: a * e, q, egc)
    kd = each(lambda a, gl, x: a * jnp.exp(gl - x), k, gc_last, gc)
    g_last = [jnp.exp(x) for x in gc_last]
    v_new = [jnp.concatenate([sol[bb][h * c:(h + 1) * c, :HEAD_DV]
                              - _dot(sol[bb][h * c:(h + 1) * c, HEAD_DV:], st_scr[bb, h]) for h in range(nh)], axis=0)
             for bb in rows_bb]
    o_intra = each(lambda a, b: jnp.dot(block_diag(a), b.astype(BF16), preferred_element_type=F32), qk, v_new)
    for h in range(nh):
        rows = slice(h * c, (h + 1) * c)
        for bb in rows_bb:
            state = st_scr[bb, h]
            o_ref[bb, :, h * HEAD_DV:(h + 1) * HEAD_DV] = (
                o_intra[bb][rows] + _dot(qd[bb][rows], state)).astype(o_ref.dtype)
            st_scr[bb, h] = state * g_last[bb][h * c:h * c + 1, :] + _dot(kd[bb][rows].T, v_new[bb][rows])


def _gdn_cumsum_matrix(reverse):
    c, nh = LIN_CHUNK, LIN_HEADS
    p = np.arange(c)
    tri = (p[None, :] >= p[:, None]) if reverse else (p[None, :] <= p[:, None])
    return np.kron(np.eye(nh), tri.astype(np.float32))


def _gdn(qkv, tail, a_row, dt_row, *, n_ctx, reverse, direction, nb):
    q, k, v = qkv
    bsz, n_chunks, n, _ = q.shape
    c = LIN_CHUNK
    n_ctx_chunks = n_ctx // c
    cs = _doubled(_gdn_cumsum_matrix(reverse))

    def ch(s):
        return _scan_tile(s, n_ctx_chunks, n_chunks, reverse)

    const2 = lambda b, s: (0, 0)
    stacked = pl.BlockSpec((nb, None, n, HEAD_DV), lambda b, s: (b, ch(s), 0, 0))
    return pl.pallas_call(
        functools.partial(_gdn_body, reverse=reverse, beta_col=direction * LIN_HEADS,
                          g_col=(2 + direction) * LIN_HEADS, nb=nb),
        grid=(bsz // nb, n_chunks),
        in_specs=[pl.BlockSpec(cs.shape, const2),
                  pl.BlockSpec((1, LANES), const2), pl.BlockSpec((1, LANES), const2),
                  pl.BlockSpec((nb, c, LANES), lambda b, s: (b, ch(s), 0)), stacked, stacked, stacked],
        out_specs=pl.BlockSpec((nb, c, GROUP), lambda b, s: (b, ch(s), 0)),
        out_shape=jax.ShapeDtypeStruct((bsz, n_chunks * c, GROUP), ACT),
        scratch_shapes=[pltpu.VMEM((nb, LIN_HEADS, HEAD_DV, HEAD_DV), F32)],
        compiler_params=_cparams("parallel", "arbitrary"),
        name="gdn_rev" if reverse else "gdn_fwd",
    )(cs, a_row, dt_row, tail, q, k, v)


def _post_body(x_ref, af_ref, ar_ref, ga_ref, bf_ref, br_ref, gb_ref, w_ref, ml_ref, mc_ref, o_ref,
               *, even, tm, n_ctx):
    t = pl.program_id(1)
    a = af_ref[...].astype(F32) + ar_ref[...].astype(F32)
    bsum = bf_ref[...].astype(F32) + br_ref[...].astype(F32)
    if even:
        ya = a * _gelu_tanh(ga_ref[...].astype(F32))
        yb = _head_norm(bsum, True) * _silu(gb_ref[...].astype(F32))
    else:
        ya = _head_norm(a, False) * _silu(ga_ref[...].astype(F32))
        yb = _head_norm(bsum, False) * _silu(gb_ref[...].astype(F32))
    half = w_ref.shape[0] // 2
    mix = jnp.dot(ya.astype(BF16), w_ref[0:half, :], preferred_element_type=F32)
    mix = mix + jnp.dot(yb.astype(BF16), w_ref[half:, :], preferred_element_type=F32)
    o_ref[...] = x_ref[...] + _row_select(t, tm, n_ctx, mc_ref, ml_ref, 2) * mix


def _post(x, mod, a_f, a_r, b_f, b_r, p, gate_a_col, gate_b_col, w_out, *, even, tm, n_ctx):
    bsz, l, d = x.shape
    ctx_row = bsz
    tok = lambda b, t: (b, t, 0)
    grp = pl.BlockSpec((None, tm, GROUP), tok)
    return pl.pallas_call(
        functools.partial(_post_body, even=even, tm=tm, n_ctx=n_ctx),
        grid=(bsz, l // tm),
        in_specs=[pl.BlockSpec((None, tm, d), tok), grp, grp,
                  pl.BlockSpec((None, tm, GROUP), lambda b, t: (b, t, gate_a_col)), grp, grp,
                  pl.BlockSpec((None, tm, GROUP), lambda b, t: (b, t, gate_b_col)),
                  pl.BlockSpec(w_out.shape, lambda b, t: (0, 0)),
                  pl.BlockSpec((None, 6, d), lambda b, t: (b, 0, 0)),
                  pl.BlockSpec((None, 6, d), lambda b, t: (ctx_row, 0, 0))],
        out_specs=pl.BlockSpec((None, tm, d), tok),
        out_shape=jax.ShapeDtypeStruct((bsz, l, d), F32),
        compiler_params=_cparams("parallel", "parallel"),
        name="post",
    )(x, a_f, a_r, p, b_f, b_r, p, w_out, mod, mod)


def _mlp_body(x_ref, g_ref, ml_ref, mc_ref, w1_ref, w2_ref, fg_ref, o_ref, h_scr, acc_scr, *, tm, n_ctx, final):
    t = pl.program_id(1)
    f = pl.program_id(2)
    if final:
        x_ref = x_ref.at[0]

    @pl.when(f == 0)
    def _():
        y = _rms(x_ref[...], g_ref[...])
        shift = _row_select(t, tm, n_ctx, mc_ref, ml_ref, 3)
        scale = _row_select(t, tm, n_ctx, mc_ref, ml_ref, 4)
        h_scr[...] = (y * (1.0 + scale) + shift).astype(BF16)
        acc_scr[...] = jnp.zeros_like(acc_scr)

    hid = jnp.maximum(jnp.dot(h_scr[...], w1_ref[...], preferred_element_type=F32), 0.0)
    acc_scr[...] += jnp.dot((hid * hid).astype(BF16), w2_ref[...], preferred_element_type=F32)

    @pl.when(f == pl.num_programs(2) - 1)
    def _():
        y = x_ref[...] + _row_select(t, tm, n_ctx, mc_ref, ml_ref, 5) * acc_scr[...]
        if final:
            y = _rms(y, fg_ref[...])
        o_ref[...] = y


def _mlp(x, mod, g, w1, w2, final_g, *, tm, tf, n_ctx, final):
    bsz, l, d = x.shape
    dff = w1.shape[1]
    ctx_row = bsz
    tok = lambda b, t, f: (b, t, 0)
    if final:
        l = l - n_ctx
        x_spec = pl.BlockSpec((pl.Element(1), pl.Element(tm), pl.Element(d)),
                              lambda b, t, f, off=n_ctx: (b, pl.multiple_of(off + t * tm, SUBLANES), 0))
        n_ctx = 0
    else:
        x_spec = pl.BlockSpec((None, tm, d), tok)
    return pl.pallas_call(
        functools.partial(_mlp_body, tm=tm, n_ctx=n_ctx, final=final),
        grid=(bsz, l // tm, dff // tf),
        in_specs=[x_spec,
                  pl.BlockSpec((1, d), lambda b, t, f: (0, 0)),
                  pl.BlockSpec((None, 6, d), lambda b, t, f: (b, 0, 0)),
                  pl.BlockSpec((None, 6, d), lambda b, t, f: (ctx_row, 0, 0)),
                  pl.BlockSpec((d, tf), lambda b, t, f: (0, f)),
                  pl.BlockSpec((tf, d), lambda b, t, f: (f, 0)),
                  pl.BlockSpec((1, d), lambda b, t, f: (0, 0))],
        out_specs=pl.BlockSpec((None, tm, d), tok),
        out_shape=jax.ShapeDtypeStruct((bsz, l, d), F32),
        scratch_shapes=[pltpu.VMEM((tm, d), BF16), pltpu.VMEM((tm, d), F32)],
        compiler_params=_cparams("parallel", "parallel", "arbitrary"),
        name="mlp",
    )(x, g.reshape(1, d), mod, mod, w1, w2, final_g.reshape(1, d))


def _token_tile(l):
    for div in (4, 8, 16, 17, 34):
        if l % div == 0 and (l // div) % 16 == 0 and l // div <= 1088:
            return l // div
    return l


def kernel(x, c, ctx, c_ctx, ada_w, ada_b, norm1_g, norm2_g, mix_w_out, mlp_w1, mlp_w2, ev_w_in, lru_conv_w,
           lru_conv_b, lru_wa, lru_ba, lru_wx, lru_bx, lru_lambda, ret_log_gamma, od_w_in, hg_lb_logits,
           gdn_conv_w, gdn_a_log, gdn_dt_bias, final_g):
    bsz, seq, d = x.shape
    n_ctx = ctx.shape[1]
    depth = ada_w.shape[0]
    l = n_ctx + seq
    tm = _token_tile(l)
    tf = 1024
    nb = 4 if bsz % 4 == 0 else 1

    mods = _mods(c, c_ctx, ada_w, ada_b)
    ret_dk = (ev_w_in.shape[2] - 4 * GROUP) // (2 * RET_HEADS)
    cos_t, sin_t = _rope_tables(n_ctx, seq, ret_dk)

    xs = jnp.concatenate([ctx, x], axis=1)
    for layer in range(depth):
        mod = mods[layer]
        w_out = mix_w_out[layer].astype(BF16)
        if layer % 2 == 0:
            e = layer // 2
            (p,) = _proj(xs, mod, norm1_g[layer], ev_w_in[e].astype(BF16), None, tm=tm, n_ctx=n_ctx)
            mix = []
            for dr, rev in enumerate((False, True)):
                mix.append((
                    _lru(p, lru_conv_w[e], lru_conv_b[e], _blockdiag_halves(lru_wa[e, dr]).astype(BF16),
                         lru_ba[e, dr], _blockdiag_halves(lru_wx[e, dr]).astype(BF16), lru_bx[e, dr],
                         lru_lambda[e, dr], n_ctx=n_ctx, reverse=rev),
                    _retention(p, cos_t, sin_t, ret_log_gamma[e, dr], n_ctx=n_ctx, reverse=rev, nb=nb)))
            gate_cols = (1, 4)
        else:
            o = layer // 2
            n_main = 9 * GROUP
            w_in = od_w_in[o]
            w_tail = jnp.pad(w_in[:, n_main:], ((0, 0), (0, LANES - (w_in.shape[1] - n_main)))).astype(BF16)
            p, tail = _proj(xs, mod, norm1_g[layer], w_in[:, :n_main].astype(BF16), w_tail, tm=tm, n_ctx=n_ctx)
            qkv = _gdn_prep(p, gdn_conv_w[o], n_ctx=n_ctx)
            mix = []
            for dr, rev in enumerate((False, True)):
                g_col = (2 + dr) * LIN_HEADS
                a_row = jnp.zeros((1, LANES), F32).at[0, g_col:g_col + LIN_HEADS].set(gdn_a_log[o, dr])
                dt_row = jnp.zeros((1, LANES), F32).at[0, g_col:g_col + LIN_HEADS].set(gdn_dt_bias[o, dr])
                mix.append((
                    _gla(p, hg_lb_logits[dr], n_ctx=n_ctx, reverse=rev, layer=o, nb=nb),
                    _gdn(qkv, tail, a_row, dt_row, n_ctx=n_ctx, reverse=rev, direction=dr, nb=nb)))
            gate_cols = (4, 8)
        xs = _post(xs, mod, mix[0][0], mix[1][0], mix[0][1], mix[1][1], p, gate_cols[0], gate_cols[1], w_out,
                   even=layer % 2 == 0, tm=tm, n_ctx=n_ctx)
        last = layer == depth - 1
        xs = _mlp(xs, mod, norm2_g[layer], mlp_w1[layer].astype(BF16), mlp_w2[layer].astype(BF16), final_g,
                  tm=_token_tile(seq) if last else tm, tf=tf, n_ctx=n_ctx, final=last)
    return xs
```

```python
import functools

import numpy as np
import jax
import jax.numpy as jnp
from jax import lax
from jax.experimental import pallas as pl
from jax.experimental.pallas import tpu as pltpu

F32 = jnp.float32
BF16 = jnp.bfloat16

EPS = 1e-6
GRID_W = 64
CONV_W = 4
CONV_LEFT = CONV_W // 2
LRU_BLOCKS = 8
LRU_C = 8.0
RET_HEADS = 4
RET_CHUNK = 256
ROPE_BASE = 10000.0
LIN_HEADS = 4
GLA_CHUNK = 128
LIN_CHUNK = 64
HEAD_DV = 128

LANES = 128
SUBLANES = 8
V7X_VMEM_BYTES = 64 * 1024 * 1024
VMEM_LIMIT = (V7X_VMEM_BYTES * 3) // 4
GROUP = 512
PROJ_MAX_TN = 2560

HALO = 2 * SUBLANES
ACT = jnp.bfloat16


def _cparams(*sem):
    return pltpu.CompilerParams(dimension_semantics=sem, vmem_limit_bytes=VMEM_LIMIT)


def _sigmoid(x):
    return 1.0 / (1.0 + jnp.exp(-x))


def _silu(x):
    return x * _sigmoid(x)


def _softplus(x):
    return jnp.maximum(x, 0.0) + jnp.log(1.0 + jnp.exp(-jnp.abs(x)))


def _gelu_tanh(x):
    return 0.5 * x * (1.0 + jnp.tanh(0.7978845608028654 * (x + 0.044715 * x * x * x)))


def _dot(a, b):
    return jnp.dot(a.astype(BF16), b.astype(BF16), preferred_element_type=F32)


def _dot_nt(a, b):
    return lax.dot_general(a.astype(BF16), b.astype(BF16), (((1,), (1,)), ((), ())),
                           preferred_element_type=F32)


def _split_bf16(x):
    hi = x.astype(BF16)
    lo = (x - hi.astype(F32)).astype(BF16)
    return hi, lo


def _dot_sel(m2, x):
    hi, lo = _split_bf16(x)
    return jnp.dot(m2, jnp.concatenate([hi, lo], axis=0), preferred_element_type=F32)


def _doubled(m):
    return jnp.asarray(np.concatenate([m, m], axis=1), BF16)


def _dot3(a, b):
    ah, al = _split_bf16(a)
    bh, bl = _split_bf16(b)
    return (jnp.dot(ah, bh, preferred_element_type=F32) + jnp.dot(ah, bl, preferred_element_type=F32)
            + jnp.dot(al, bh, preferred_element_type=F32))


def _row_select(t, tm, n_ctx, mc_ref, ml_ref, idx):
    rows = t * tm + lax.broadcasted_iota(jnp.int32, (tm, 1), 0)
    return jnp.where(rows < n_ctx, mc_ref[idx:idx + 1, :], ml_ref[idx:idx + 1, :])


def _rms(x, g):
    return x * lax.rsqrt(jnp.mean(x * x, axis=-1, keepdims=True) + EPS) * g


def _head_norm(o, center):
    outs = []
    for h in range(o.shape[-1] // HEAD_DV):
        oh = o[:, h * HEAD_DV:(h + 1) * HEAD_DV]
        if center:
            oh = oh - jnp.mean(oh, axis=-1, keepdims=True)
        outs.append(oh * lax.rsqrt(jnp.mean(oh * oh, axis=-1, keepdims=True) + EPS))
    return jnp.concatenate(outs, axis=-1)


def _scan_tile(step, n_first, n_total, reverse):
    if not reverse:
        return step
    return jnp.where(step < n_first, n_first - 1 - step, n_total - 1 - (step - n_first))


def _mods_body(s_ref, w_ref, b_ref, o_ref):
    s = _silu(s_ref[...])
    o_ref[...] = _dot3(s, w_ref[...]) + b_ref[...]


def _mods(c, c_ctx, ada_w, ada_b):
    depth, d, n6 = ada_w.shape
    bsz = c.shape[0]
    rows = -(-(bsz + 1) // SUBLANES) * SUBLANES
    s = jnp.zeros((rows, d), F32).at[:bsz].set(c).at[bsz].set(c_ctx)
    tn = 1024
    out = pl.pallas_call(
        _mods_body,
        grid=(depth, n6 // tn),
        in_specs=[pl.BlockSpec((rows, d), lambda l, n: (0, 0)),
                  pl.BlockSpec((None, d, tn), lambda l, n: (l, 0, n)),
                  pl.BlockSpec((None, 1, tn), lambda l, n: (l, 0, n))],
        out_specs=pl.BlockSpec((None, rows, tn), lambda l, n: (l, 0, n)),
        out_shape=jax.ShapeDtypeStruct((depth, rows, n6), F32),
        compiler_params=_cparams("parallel", "parallel"),
        name="mods",
    )(s, ada_w, ada_b.reshape(depth, 1, n6))
    return out.reshape(depth, rows, 6, d)


def _proj_body(*refs, tm, n_ctx, has_tail):
    if has_tail:
        x_ref, ml_ref, mc_ref, g_ref, w_ref, wt_ref, o_ref, ot_ref, h_scr = refs
    else:
        x_ref, ml_ref, mc_ref, g_ref, w_ref, o_ref, h_scr = refs
    t = pl.program_id(1)

    @pl.when(pl.program_id(2) == 0)
    def _():
        y = _rms(x_ref[...], g_ref[...])
        shift = _row_select(t, tm, n_ctx, mc_ref, ml_ref, 0)
        scale = _row_select(t, tm, n_ctx, mc_ref, ml_ref, 1)
        h_scr[...] = (y * (1.0 + scale) + shift).astype(BF16)
        if has_tail:
            ot_ref[...] = jnp.dot(h_scr[...], wt_ref[...], preferred_element_type=F32)

    o_ref[...] = jnp.dot(h_scr[...], w_ref[...], preferred_element_type=F32).astype(o_ref.dtype)


def _proj(x, mod, g, w, w_tail, *, tm, n_ctx):
    bsz, l, d = x.shape
    n_out = w.shape[1]
    tn = max(t for t in range(LANES, PROJ_MAX_TN + 1, LANES) if n_out % t == 0)
    ng = n_out // tn
    ctx_row = bsz
    in_specs = [pl.BlockSpec((None, tm, d), lambda b, t, n: (b, t, 0)),
                pl.BlockSpec((None, 6, d), lambda b, t, n: (b, 0, 0)),
                pl.BlockSpec((None, 6, d), lambda b, t, n: (ctx_row, 0, 0)),
                pl.BlockSpec((1, d), lambda b, t, n: (0, 0)),
                pl.BlockSpec((d, tn), lambda b, t, n: (0, n))]
    out_specs = [pl.BlockSpec((None, tm, tn), lambda b, t, n: (b, t, n))]
    out_shape = [jax.ShapeDtypeStruct((bsz, l, n_out), ACT)]
    args = [x, mod, mod, g.reshape(1, d), w]
    if w_tail is not None:
        in_specs.append(pl.BlockSpec((d, LANES), lambda b, t, n: (0, 0)))
        out_specs.append(pl.BlockSpec((None, tm, LANES), lambda b, t, n: (b, t, 0)))
        out_shape.append(jax.ShapeDtypeStruct((bsz, l, LANES), F32))
        args.append(w_tail)
    outs = pl.pallas_call(
        functools.partial(_proj_body, tm=tm, n_ctx=n_ctx, has_tail=w_tail is not None),
        grid=(bsz, l // tm, ng),
        in_specs=in_specs, out_specs=out_specs, out_shape=out_shape,
        scratch_shapes=[pltpu.VMEM((tm, d), BF16)],
        compiler_params=_cparams("parallel", "parallel", "arbitrary"),
        name="proj",
    )(*args)
    return outs


def _lru_body(sm_ref, se_ref, x_ref, xp_ref, xn_ref, cw_ref, cb_ref, wa_ref, ba_ref, wx_ref, bx_ref, lam_ref,
              o_ref, a_scr, b_scr, h_scr, *, tl, n_ctx_tiles, n_tiles, reverse):
    step = pl.program_id(0)
    tile = _scan_tile(step, n_ctx_tiles, n_tiles, reverse)
    bsz = x_ref.shape[0]
    width = x_ref.shape[2]
    half = width // 2
    seg_first = jnp.logical_or(tile == 0, tile == n_ctx_tiles)
    seg_last = jnp.logical_or(tile == n_ctx_tiles - 1, tile == n_tiles - 1)
    keep_prev = jnp.where(seg_first, 0.0, 1.0)
    keep_next = jnp.where(seg_last, 0.0, 1.0)
    neg_c_sp = -LRU_C * _softplus(-lam_ref[...])

    @pl.when(step == 0)
    def _():
        h_scr[...] = jnp.zeros_like(h_scr)

    pair = 2 if bsz % 2 == 0 else 1

    def block_dot(ub, w_ref):
        return jnp.concatenate([jnp.dot(ub[:, :half], w_ref[0], preferred_element_type=F32),
                                jnp.dot(ub[:, half:], w_ref[1], preferred_element_type=F32)], axis=-1)

    def gates(bp, carry):
        bs = [bp * pair + n for n in range(pair)]
        taps = [_conv_taps(sm_ref, se_ref, x_ref[b], xp_ref[b], xn_ref[b], keep_prev, keep_next, tl) for b in bs]
        u = [cb_ref[...] + sum(cw_ref[k:k + 1, :] * tp[k] for k in range(CONV_W)) for tp in taps]
        ub = [x.astype(BF16) for x in u]
        r_pre = [block_dot(x, wa_ref) + ba_ref[...] for x in ub]
        i_pre = [block_dot(x, wx_ref) + bx_ref[...] for x in ub]
        a = [jnp.exp(neg_c_sp * _sigmoid(x)) for x in r_pre]
        for n, b in enumerate(bs):
            a_scr[b] = a[n]
            b_scr[b] = jnp.sqrt(1.0 - a[n] * a[n]) * (_sigmoid(i_pre[n]) * u[n])
        return carry

    lax.fori_loop(0, bsz // pair, gates, 0)

    def scan_rows(j, hs):
        idx = (tl - 1 - j) if reverse else j
        new = []
        for b in range(bsz):
            h = a_scr[b, pl.ds(idx, 1), :] * hs[b] + b_scr[b, pl.ds(idx, 1), :]
            b_scr[b, pl.ds(idx, 1), :] = h
            new.append(h)
        return tuple(new)

    hs = tuple(h_scr[b, 0:1, :] for b in range(bsz))
    hs = lax.fori_loop(0, tl, scan_rows, hs)
    for b in range(bsz):
        h_scr[b, 0:1, :] = hs[b]
    o_ref[...] = b_scr[...].astype(o_ref.dtype)


def _lru(p, conv_w, conv_b, wa, ba, wx, bx, lam, *, n_ctx, reverse, tl=256):
    bsz, l, _ = p.shape
    width = GROUP
    n_tiles = l // tl
    n_ctx_tiles = n_ctx // tl
    hb = tl // HALO
    n_hblocks = l // HALO

    def tile_of(s):
        return _scan_tile(s, n_ctx_tiles, n_tiles, reverse)

    vec = pl.BlockSpec((1, width), lambda s: (0, 0))
    mat = pl.BlockSpec((2, width // 2, width // 2), lambda s: (0, 0, 0))
    sm, se = (jnp.asarray(m, p.dtype) for m in _shift_matrices(tl))
    return pl.pallas_call(
        functools.partial(_lru_body, tl=tl, n_ctx_tiles=n_ctx_tiles, n_tiles=n_tiles, reverse=reverse),
        grid=(n_tiles,),
        in_specs=[pl.BlockSpec(sm.shape, lambda s: (0, 0)), pl.BlockSpec(se.shape, lambda s: (0, 0)),
                  pl.BlockSpec((bsz, tl, width), lambda s: (0, tile_of(s), 0)),
                  pl.BlockSpec((bsz, HALO, width), lambda s: (0, jnp.maximum(tile_of(s) * hb - 1, 0), 0)),
                  pl.BlockSpec((bsz, HALO, width),
                               lambda s: (0, jnp.minimum((tile_of(s) + 1) * hb, n_hblocks - 1), 0)),
                  pl.BlockSpec((CONV_W, width), lambda s: (0, 0)), vec, mat, vec, mat, vec, vec],
        out_specs=pl.BlockSpec((bsz, tl, width), lambda s: (0, tile_of(s), 0)),
        out_shape=jax.ShapeDtypeStruct((bsz, l, width), ACT),
        scratch_shapes=[pltpu.VMEM((bsz, tl, width), F32),
                        pltpu.VMEM((bsz, tl, width), F32),
                        pltpu.VMEM((bsz, SUBLANES, width), F32)],
        compiler_params=_cparams("arbitrary"),
        name="lru_rev" if reverse else "lru_fwd",
    )(sm, se, p, p, p, conv_w, conv_b.reshape(1, width), wa, ba.reshape(1, width), wx, bx.reshape(1, width),
      lam.reshape(1, width))


def _blockdiag_halves(w):
    nb, n, _ = w.shape
    per = nb // 2
    w = w.reshape(2, per, n, n)
    eye = jnp.eye(per, dtype=w.dtype)
    out = w[:, :, :, None, :] * eye[None, :, None, :, None]
    return out.reshape(2, per * n, per * n)


def _ret_body(lg_ref, q_ref, k_ref, v_ref, cos_ref, sin_ref, o_ref, dec_scr, qd_scr, kd_scr, cd_scr, st_scr,
              *, reverse, dk, nb):
    c = RET_CHUNK
    step = pl.program_id(1)

    @pl.when(step == 0)
    def _():
        st_scr[...] = jnp.zeros_like(st_scr)
        i = lax.broadcasted_iota(jnp.int32, (c, c), 0)
        j = lax.broadcasted_iota(jnp.int32, (c, c), 1)
        diff = (j - i) if reverse else (i - j)
        row = lax.broadcasted_iota(jnp.int32, (c, HEAD_DV), 0)
        pos = ((c - 1 - row) if reverse else row).astype(F32)
        dpos = jnp.maximum(diff, 0).astype(F32)
        for h in range(RET_HEADS):
            lg = lg_ref[h]
            dec_scr[h] = jnp.where(diff >= 0, jnp.exp(lg * dpos), 0.0)
            qd_scr[h] = jnp.exp(lg * (pos + 1.0))
            kd_scr[h] = jnp.exp(lg * (c - 1.0 - pos))
            cd_scr[h] = jnp.exp(jnp.full((SUBLANES, LANES), c, F32) * lg)

    def rope(x):
        width = x.shape[-1]
        lane = lax.broadcasted_iota(jnp.int32, x.shape, 1)
        first = (lane & (dk - 1)) < dk // 2
        rot = jnp.where(first, pltpu.roll(x, width - dk // 2, axis=1), pltpu.roll(x, dk // 2, axis=1))
        return x * cos_ref[...] + rot * sin_ref[...]

    q = [rope(q_ref[bb].astype(F32)) for bb in range(nb)]
    k = [rope(k_ref[bb].astype(F32)) * (dk ** -0.5) for bb in range(nb)]
    pairs = [(bb, h) for bb in range(nb) for h in range(RET_HEADS)]
    qh = [q[bb][:, h * dk:(h + 1) * dk] for bb, h in pairs]
    kh = [k[bb][:, h * dk:(h + 1) * dk] for bb, h in pairs]
    vh = [v_ref[bb, :, h * HEAD_DV:(h + 1) * HEAD_DV] for bb, h in pairs]
    s = [_dot_nt(qh[i], kh[i]) * dec_scr[h] for i, (bb, h) in enumerate(pairs)]
    state = [st_scr[bb, h] for bb, h in pairs]
    o_x = [_dot(qh[i], state[i]) * qd_scr[h] for i, (bb, h) in enumerate(pairs)]
    o_in = [_dot(s[i], vh[i]) for i in range(len(pairs))]
    kv = [_dot((kh[i] * kd_scr[h][:, :dk]).T, vh[i]) for i, (bb, h) in enumerate(pairs)]
    for i, (bb, h) in enumerate(pairs):
        st_scr[bb, h] = state[i] * cd_scr[h][0:1, :] + kv[i]
        o_ref[bb, :, h * HEAD_DV:(h + 1) * HEAD_DV] = (o_in[i] + o_x[i]).astype(o_ref.dtype)


def _retention(p, cos_t, sin_t, log_gamma, *, n_ctx, reverse, nb):
    bsz, l, _ = p.shape
    c = RET_CHUNK
    qk_w = cos_t.shape[1]
    dk = qk_w // RET_HEADS
    n_chunks = l // c
    n_ctx_chunks = n_ctx // c

    def ch(s):
        return _scan_tile(s, n_ctx_chunks, n_chunks, reverse)

    q_col = (2 * GROUP) // qk_w
    return pl.pallas_call(
        functools.partial(_ret_body, reverse=reverse, dk=dk, nb=nb),
        grid=(bsz // nb, n_chunks),
        in_specs=[pl.BlockSpec(memory_space=pltpu.SMEM),
                  pl.BlockSpec((nb, c, qk_w), lambda b, s: (b, ch(s), q_col)),
                  pl.BlockSpec((nb, c, qk_w), lambda b, s: (b, ch(s), q_col + 1)),
                  pl.BlockSpec((nb, c, GROUP), lambda b, s: (b, ch(s), 3)),
                  pl.BlockSpec((c, qk_w), lambda b, s: (ch(s), 0)),
                  pl.BlockSpec((c, qk_w), lambda b, s: (ch(s), 0))],
        out_specs=pl.BlockSpec((nb, c, GROUP), lambda b, s: (b, ch(s), 0)),
        out_shape=jax.ShapeDtypeStruct((bsz, l, GROUP), ACT),
        scratch_shapes=[pltpu.VMEM((RET_HEADS, c, c), F32),
                        pltpu.VMEM((RET_HEADS, c, HEAD_DV), F32),
                        pltpu.VMEM((RET_HEADS, c, HEAD_DV), F32),
                        pltpu.VMEM((RET_HEADS, SUBLANES, LANES), F32),
                        pltpu.VMEM((nb, RET_HEADS, dk, HEAD_DV), F32)],
        compiler_params=_cparams("parallel", "arbitrary"),
        name="ret_rev" if reverse else "ret_fwd",
    )(log_gamma, p, p, p, cos_t, sin_t)


def _rope_tables(n_ctx, seq, dk):
    n_freq = dk // 4
    inv = jnp.power(ROPE_BASE, -jnp.arange(n_freq, dtype=F32) / n_freq)
    rows = seq // GRID_W
    r = jnp.arange(rows, dtype=F32)
    col = jnp.arange(GRID_W, dtype=F32)
    row_ang = jnp.broadcast_to(r[:, None, None] * inv, (rows, GRID_W, n_freq))
    col_ang = jnp.broadcast_to(col[None, :, None] * inv, (rows, GRID_W, n_freq))
    ang = jnp.concatenate([row_ang, col_ang], axis=-1).reshape(rows * GRID_W, 2 * n_freq)
    cos, sin = jnp.cos(ang), jnp.sin(ang)
    cos_h = jnp.concatenate([cos, cos], axis=-1)
    sin_h = jnp.concatenate([-sin, sin], axis=-1)
    cos_t = jnp.concatenate([jnp.ones((n_ctx, dk), F32), cos_h], axis=0)
    sin_t = jnp.concatenate([jnp.zeros((n_ctx, dk), F32), sin_h], axis=0)
    return jnp.tile(cos_t, (1, RET_HEADS)), jnp.tile(sin_t, (1, RET_HEADS))


_GLA_LEVELS = tuple(GLA_CHUNK >> (i + 1) for i in range(GLA_CHUNK.bit_length() - 1))


def _gla_matrices(reverse):
    c = GLA_CHUNK
    p = np.arange(c)
    mats = [(p[None, :] <= p[:, None]), (p[None, :] > p[:, None])]
    for s in _GLA_LEVELS:
        bs = (p // s) * s
        right = ((p // s) % 2 == 1)
        m_r = (p[None, :] > bs[:, None]) & (p[None, :] <= p[:, None])
        m_l = (p[None, :] > p[:, None]) & (p[None, :] <= (bs + s)[:, None])
        mats.append(np.where(right[:, None], m_r, m_l))
    m = np.stack(mats).astype(np.float32)
    if reverse:
        m = m[:, ::-1, ::-1]
    return m.reshape(len(mats) * c, c)


def _gla_body(m_ref, lb_ref, q_ref, f_ref, v_ref, o_ref, st_scr, *, reverse, layer, nb):
    c = GLA_CHUNK
    step = pl.program_id(1)

    @pl.when(step == 0)
    def _():
        st_scr[...] = jnp.zeros_like(st_scr)

    lg = lb_ref[...]
    e = jnp.exp(lg - jnp.max(lg, axis=0, keepdims=True))
    sm = e / jnp.sum(e, axis=0, keepdims=True)
    lb = jnp.sum(sm[0:layer + 1, :], axis=0, keepdims=True) - sm[0:1, :]

    ri = lax.broadcasted_iota(jnp.int32, (c, 1), 0)
    i2 = lax.broadcasted_iota(jnp.int32, (c, c), 0)
    j2 = lax.broadcasted_iota(jnp.int32, (c, c), 1)
    if reverse:
        ri, i2, j2 = c - 1 - ri, c - 1 - i2, c - 1 - j2
    last = 0 if reverse else c - 1
    width = q_ref.shape[-1]
    nh = width // HEAD_DV

    rows_bb = list(range(nb))
    pairs = [(bb, h) for bb in rows_bb for h in range(nh)]

    def head(x, h):
        return x[:, h * HEAD_DV:(h + 1) * HEAD_DV]

    f = [lb + (1.0 - lb) * _sigmoid(f_ref[bb].astype(F32)) for bb in rows_bb]
    k = [1.0 - x for x in f]
    q = [_silu(q_ref[bb].astype(F32)) for bb in rows_bb]
    ex = [_dot_sel(m_ref[...], jnp.log(x)) for x in f]
    a_last = [jnp.exp(x[last:last + 1, :]) for x in ex]
    qa = [q[bb] * jnp.exp(ex[bb][0:c, :]) for bb in rows_bb]
    kl = [k[bb] * jnp.exp(ex[bb][c:2 * c, :]) for bb in rows_bb]

    qb = [x.astype(BF16) for x in q]
    kb = [x.astype(BF16) for x in k]
    scores = [jnp.where(i2 == j2, _dot_nt(head(qb[bb], h), head(kb[bb], h)), 0.0) for bb, h in pairs]
    for lvl, s in enumerate(_GLA_LEVELS):
        sh = s.bit_length() - 1
        right = jnp.broadcast_to(((ri >> sh) & 1).astype(F32), (c, width)).astype(BF16) > 0
        x = [jnp.where(right, qb[bb], kb[bb]) * jnp.exp(ex[bb][(2 + lvl) * c:(3 + lvl) * c, :].astype(BF16))
             for bb in rows_bb]
        bi, bj = i2 >> sh, j2 >> sh
        valid = bj == jnp.where((bi & 1) == 1, bi - 1, -1)
        gram = [_dot_nt(head(x[bb], h), head(x[bb], h)) for bb, h in pairs]
        scores = [sc + jnp.where(valid, gm, 0.0) for sc, gm in zip(scores, gram)]

    state_t = [st_scr[bb, h] for bb, h in pairs]
    o_intra = [_dot(scores[i], head(v_ref[bb], h)) for i, (bb, h) in enumerate(pairs)]
    o_inter = [_dot_nt(head(qa[bb], h), state_t[i]) for i, (bb, h) in enumerate(pairs)]
    kv = [_dot(head(v_ref[bb], h).astype(F32).T, head(kl[bb], h)) for bb, h in pairs]
    for i, (bb, h) in enumerate(pairs):
        o_ref[bb, :, h * HEAD_DV:(h + 1) * HEAD_DV] = (o_intra[i] + o_inter[i]).astype(o_ref.dtype)
        st_scr[bb, h] = state_t[i] * head(a_last[bb], h) + kv[i]


def _gla(p, lb_logits, *, n_ctx, reverse, layer, nb):
    bsz, l, _ = p.shape
    c = GLA_CHUNK
    n_chunks = l // c
    n_ctx_chunks = n_ctx // c
    m = _doubled(_gla_matrices(reverse))

    def ch(s):
        return _scan_tile(s, n_ctx_chunks, n_chunks, reverse)

    f_col = 2 if reverse else 1
    return pl.pallas_call(
        functools.partial(_gla_body, reverse=reverse, layer=layer, nb=nb),
        grid=(bsz // nb, n_chunks),
        in_specs=[pl.BlockSpec(m.shape, lambda b, s: (0, 0)),
                  pl.BlockSpec(lb_logits.shape, lambda b, s: (0, 0)),
                  pl.BlockSpec((nb, c, GROUP), lambda b, s: (b, ch(s), 0)),
                  pl.BlockSpec((nb, c, GROUP), lambda b, s: (b, ch(s), f_col)),
                  pl.BlockSpec((nb, c, GROUP), lambda b, s: (b, ch(s), 3))],
        out_specs=pl.BlockSpec((nb, c, GROUP), lambda b, s: (b, ch(s), 0)),
        out_shape=jax.ShapeDtypeStruct((bsz, l, GROUP), ACT),
        scratch_shapes=[pltpu.VMEM((nb, LIN_HEADS, HEAD_DV, HEAD_DV), F32)],
        compiler_params=_cparams("parallel", "arbitrary"),
        name="gla_rev" if reverse else "gla_fwd",
    )(m, lb_logits, p, p, p)


def _shift_matrices(rows):
    offs = [t - CONV_LEFT for t in range(CONV_W) if t != CONV_LEFT]
    main = np.zeros((len(offs) * rows, rows), np.float32)
    edge = np.zeros((len(offs) * SUBLANES, 2 * HALO), np.float32)
    for n, off in enumerate(offs):
        for i in range(rows):
            j = i + off
            if 0 <= j < rows:
                main[n * rows + i, j] = 1.0
            elif j < 0:
                edge[n * SUBLANES + i, HALO + j] = 1.0
            else:
                edge[n * SUBLANES + i - (rows - SUBLANES), HALO + j - rows] = 1.0
    return main, edge


def _conv_taps(sm_ref, se_ref, x, xp, xn, keep_prev, keep_next, rows):
    sh = jnp.dot(sm_ref[...], x, preferred_element_type=F32)
    halo = jnp.concatenate([xp * keep_prev.astype(xp.dtype), xn * keep_next.astype(xn.dtype)], axis=0)
    fix = jnp.dot(se_ref[...], halo, preferred_element_type=F32)
    taps, n = [], 0
    for t in range(CONV_W):
        off = t - CONV_LEFT
        if off == 0:
            taps.append(x.astype(F32))
            continue
        s = sh[n * rows:(n + 1) * rows, :]
        f = fix[n * SUBLANES:(n + 1) * SUBLANES, :]
        if off < 0:
            s = jnp.concatenate([s[0:SUBLANES] + f, s[SUBLANES:]], axis=0)
        else:
            s = jnp.concatenate([s[:rows - SUBLANES], s[rows - SUBLANES:] + f], axis=0)
        taps.append(s)
        n += 1
    return taps


def _gdn_prep_body(sm_ref, se_ref, cw_ref, q_ref, qp_ref, qn_ref, k_ref, kp_ref, kn_ref, v_ref, vp_ref, vn_ref,
                   qo_ref, ko_ref, vo_ref, *, tp, n_ctx_tiles, n_tiles):
    c = LIN_CHUNK
    nh = LIN_HEADS
    tile = pl.program_id(1)
    seg_first = jnp.logical_or(tile == 0, tile == n_ctx_tiles)
    seg_last = jnp.logical_or(tile == n_ctx_tiles - 1, tile == n_tiles - 1)
    keep_prev = jnp.where(seg_first, 0.0, 1.0)
    keep_next = jnp.where(seg_last, 0.0, 1.0)

    all_taps = [_conv_taps(sm_ref, se_ref, x[...], xp[...], xn[...], keep_prev, keep_next, tp)
                for x, xp, xn in ((q_ref, qp_ref, qn_ref), (k_ref, kp_ref, kn_ref), (v_ref, vp_ref, vn_ref))]

    def conv_silu(col):
        taps = all_taps[col]
        u = cw_ref[0:1, col * GROUP:(col + 1) * GROUP] * taps[0]
        for t in range(1, CONV_W):
            u = u + cw_ref[t:t + 1, col * GROUP:(col + 1) * GROUP] * taps[t]
        return _silu(u)

    def l2n(x):
        return x * lax.rsqrt(jnp.sum(x * x, axis=-1, keepdims=True) + EPS)

    def emit(out_ref, x, norm, scale):
        for h in range(nh):
            xh = x[:, h * HEAD_DV:(h + 1) * HEAD_DV]
            if norm:
                xh = l2n(xh) * scale
            for j in range(tp // c):
                out_ref[j, h * c:(h + 1) * c, :] = xh[j * c:(j + 1) * c, :].astype(out_ref.dtype)

    emit(qo_ref, conv_silu(0), True, HEAD_DV ** -0.5)
    emit(ko_ref, conv_silu(1), True, 1.0)
    emit(vo_ref, conv_silu(2), False, 1.0)


def _gdn_prep(p, conv_w, *, n_ctx, tp=256):
    bsz, l, _ = p.shape
    c = LIN_CHUNK
    n_tiles = l // tp
    hb = tp // HALO
    n_hblocks = l // HALO

    def triple(col):
        return [pl.BlockSpec((None, tp, GROUP), lambda b, t: (b, t, col)),
                pl.BlockSpec((None, HALO, GROUP), lambda b, t: (b, jnp.maximum(t * hb - 1, 0), col)),
                pl.BlockSpec((None, HALO, GROUP), lambda b, t: (b, jnp.minimum((t + 1) * hb, n_hblocks - 1), col))]

    out_spec = pl.BlockSpec((None, tp // c, LIN_HEADS * c, HEAD_DV), lambda b, t: (b, t, 0, 0))
    out_sds = jax.ShapeDtypeStruct((bsz, l // c, LIN_HEADS * c, HEAD_DV), ACT)
    sm, se = (jnp.asarray(m, p.dtype) for m in _shift_matrices(tp))
    const2 = lambda b, t: (0, 0)
    return pl.pallas_call(
        functools.partial(_gdn_prep_body, tp=tp, n_ctx_tiles=n_ctx // tp, n_tiles=n_tiles),
        grid=(bsz, n_tiles),
        in_specs=[pl.BlockSpec(sm.shape, const2), pl.BlockSpec(se.shape, const2), pl.BlockSpec(conv_w.shape, const2)]
                 + triple(5) + triple(6) + triple(7),
        out_specs=[out_spec] * 3, out_shape=[out_sds] * 3,
        compiler_params=_cparams("parallel", "parallel"),
        name="gdn_prep",
    )(sm, se, conv_w, *([p] * 9))


def _gdn_body(cs_ref, arow_ref, dtrow_ref, t_ref, q_ref, k_ref, v_ref, o_ref, st_scr,
              *, reverse, beta_col, g_col, nb):
    c = LIN_CHUNK
    nh = LIN_HEADS
    n = nh * c
    step = pl.program_id(1)

    @pl.when(step == 0)
    def _():
        st_scr[...] = jnp.zeros_like(st_scr)

    i2 = lax.broadcasted_iota(jnp.int32, (c, n), 0)
    lane = lax.broadcasted_iota(jnp.int32, (c, n), 1)
    j2 = lane & (c - 1)
    grp = lane >> 6
    mask = (j2 >= i2) if reverse else (j2 <= i2)
    same32 = (i2 >> 5) == (j2 >> 5)
    same16 = (i2 >> 4) == (j2 >> 4)
    eye = jnp.where(i2 == j2, 1.0, 0.0)
    grp_rows = [jnp.where((lax.broadcasted_iota(jnp.int32, (1, n), 1) >> 6) == h, 1.0, 0.0).astype(BF16)
                for h in range(nh)]
    last = 0 if reverse else c - 1

    def each(fn, *lists):
        return [fn(*args) for args in zip(*lists)]

    def head_bcast(x, col):
        return jnp.concatenate([jnp.broadcast_to(x[:, col + h:col + h + 1], (c, HEAD_DV)) for h in range(nh)], axis=0)

    def side_by_side(x):
        out = x[(nh - 1) * c:nh * c, :]
        for h in range(nh - 2, -1, -1):
            out = jnp.where(grp == h, x[h * c:(h + 1) * c, :], out)
        return out

    def block_diag(x):
        xb = x.astype(BF16)
        return jnp.concatenate([xb * grp_rows[h] for h in range(nh)], axis=0)

    def dot_sbs(x, y_bd):
        return jnp.dot(x.astype(BF16), y_bd, preferred_element_type=F32)

    rows_bb = list(range(nb))
    q = [q_ref[bb].astype(F32) for bb in rows_bb]
    k = [k_ref[bb].astype(F32) for bb in rows_bb]
    v = [v_ref[bb].astype(F32) for bb in rows_bb]

    tail = [t_ref[bb] for bb in rows_bb]
    beta = [head_bcast(_sigmoid(x), beta_col) for x in tail]
    g = [head_bcast(-jnp.exp(arow_ref[...]) * _softplus(x + dtrow_ref[...]), g_col) for x in tail]
    gc = [_dot_sel(cs_ref[...], x) for x in g]
    gc_last = [jnp.concatenate([jnp.broadcast_to(x[h * c + last:h * c + last + 1, :], (c, HEAD_DV))
                                for h in range(nh)], axis=0) for x in gc]
    diff = [side_by_side(jnp.concatenate([x, x], axis=1)) - jnp.transpose(x)[0:1, :] for x in gc]
    gam = [jnp.where(mask, jnp.exp(jnp.where(mask, x, 0.0)), 0.0) for x in diff]

    kb = each(lambda a, b: a * b, k, beta)
    mm = each(lambda a, b, gm: jnp.where(i2 == j2, 0.0, side_by_side(_dot_nt(a, b)) * gm), kb, k, gam)
    qk = each(lambda a, b, gm: side_by_side(_dot_nt(a, b)) * gm, q, k, gam)
    egc = [jnp.exp(x) for x in gc]
    rhs = each(lambda a, b, kbx, e: jnp.concatenate([a * b, kbx * e], axis=1), v, beta, kb, egc)

    md = [jnp.where(same16, x, 0.0) for x in mm]
    bd = [block_diag(x) for x in md]
    m2 = each(dot_sbs, md, bd)
    t = [eye - x for x in md]
    bd = [block_diag(x) for x in m2]
    t = each(lambda a, b: a + dot_sbs(a, b), t, bd)
    m4 = each(dot_sbs, m2, bd)
    bd = [block_diag(x) for x in m4]
    t = each(lambda a, b: a + dot_sbs(a, b), t, bd)
    m8 = each(dot_sbs, m4, bd)
    t = each(lambda a, b: a + dot_sbs(a, block_diag(b)), t, m8)
    c32 = each(lambda a, b: jnp.where(same32, a - b, 0.0), mm, md)
    ct = each(lambda a, b: dot_sbs(a, block_diag(b)), c32, t)
    t = each(lambda a, b: a - dot_sbs(a, block_diag(b)), t, ct)
    c64 = [jnp.where(same32, 0.0, x) for x in mm]
    ct = each(lambda a, b: dot_sbs(a, block_diag(b)), c64, t)
    t = each(lambda a, b: a - dot_sbs(a, block_diag(b)), t, ct)
    sol = each(lambda a, b: jnp.dot(block_diag(a), b.astype(BF16), preferred_element_type=F32), t, rhs)

    qd = each(lambda a, e: a * e, q, egc)
    kd = each(lambda a, gl, x: a * jnp.exp(gl - x), k, gc_last, gc)
    g_last = [jnp.exp(x) for x in gc_last]
    v_new = [jnp.concatenate([sol[bb][h * c:(h + 1) * c, :HEAD_DV]
                              - _dot(sol[bb][h * c:(h + 1) * c, HEAD_DV:], st_scr[bb, h]) for h in range(nh)], axis=0)
             for bb in rows_bb]
    o_intra = each(lambda a, b: jnp.dot(block_diag(a), b.astype(BF16), preferred_element_type=F32), qk, v_new)
    for h in range(nh):
        rows = slice(h * c, (h + 1) * c)
        for bb in rows_bb:
            state = st_scr[bb, h]
            o_ref[bb, :, h * HEAD_DV:(h + 1) * HEAD_DV] = (
                o_intra[bb][rows] + _dot(qd[bb][rows], state)).astype(o_ref.dtype)
            st_scr[bb, h] = state * g_last[bb][h * c:h * c + 1, :] + _dot(kd[bb][rows].T, v_new[bb][rows])


def _gdn_cumsum_matrix(reverse):
    c, nh = LIN_CHUNK, LIN_HEADS
    p = np.arange(c)
    tri = (p[None, :] >= p[:, None]) if reverse else (p[None, :] <= p[:, None])
    return np.kron(np.eye(nh), tri.astype(np.float32))


def _gdn(qkv, tail, a_row, dt_row, *, n_ctx, reverse, direction, nb):
    q, k, v = qkv
    bsz, n_chunks, n, _ = q.shape
    c = LIN_CHUNK
    n_ctx_chunks = n_ctx // c
    cs = _doubled(_gdn_cumsum_matrix(reverse))

    def ch(s):
        return _scan_tile(s, n_ctx_chunks, n_chunks, reverse)

    const2 = lambda b, s: (0, 0)
    stacked = pl.BlockSpec((nb, None, n, HEAD_DV), lambda b, s: (b, ch(s), 0, 0))
    return pl.pallas_call(
        functools.partial(_gdn_body, reverse=reverse, beta_col=direction * LIN_HEADS,
                          g_col=(2 + direction) * LIN_HEADS, nb=nb),
        grid=(bsz // nb, n_chunks),
        in_specs=[pl.BlockSpec(cs.shape, const2),
                  pl.BlockSpec((1, LANES), const2), pl.BlockSpec((1, LANES), const2),
                  pl.BlockSpec((nb, c, LANES), lambda b, s: (b, ch(s), 0)), stacked, stacked, stacked],
        out_specs=pl.BlockSpec((nb, c, GROUP), lambda b, s: (b, ch(s), 0)),
        out_shape=jax.ShapeDtypeStruct((bsz, n_chunks * c, GROUP), ACT),
        scratch_shapes=[pltpu.VMEM((nb, LIN_HEADS, HEAD_DV, HEAD_DV), F32)],
        compiler_params=_cparams("parallel", "arbitrary"),
        name="gdn_rev" if reverse else "gdn_fwd",
    )(cs, a_row, dt_row, tail, q, k, v)


def _post_body(x_ref, af_ref, ar_ref, ga_ref, bf_ref, br_ref, gb_ref, w_ref, ml_ref, mc_ref, o_ref,
               *, even, tm, n_ctx):
    t = pl.program_id(1)
    a = af_ref[...].astype(F32) + ar_ref[...].astype(F32)
    bsum = bf_ref[...].astype(F32) + br_ref[...].astype(F32)
    if even:
        ya = a * _gelu_tanh(ga_ref[...].astype(F32))
        yb = _head_norm(bsum, True) * _silu(gb_ref[...].astype(F32))
    else:
        ya = _head_norm(a, False) * _silu(ga_ref[...].astype(F32))
        yb = _head_norm(bsum, False) * _silu(gb_ref[...].astype(F32))
    half = w_ref.shape[0] // 2
    mix = jnp.dot(ya.astype(BF16), w_ref[0:half, :], preferred_element_type=F32)
    mix = mix + jnp.dot(yb.astype(BF16), w_ref[half:, :], preferred_element_type=F32)
    o_ref[...] = x_ref[...] + _row_select(t, tm, n_ctx, mc_ref, ml_ref, 2) * mix


def _post(x, mod, a_f, a_r, b_f, b_r, p, gate_a_col, gate_b_col, w_out, *, even, tm, n_ctx):
    bsz, l, d = x.shape
    ctx_row = bsz
    tok = lambda b, t: (b, t, 0)
    grp = pl.BlockSpec((None, tm, GROUP), tok)
    return pl.pallas_call(
        functools.partial(_post_body, even=even, tm=tm, n_ctx=n_ctx),
        grid=(bsz, l // tm),
        in_specs=[pl.BlockSpec((None, tm, d), tok), grp, grp,
                  pl.BlockSpec((None, tm, GROUP), lambda b, t: (b, t, gate_a_col)), grp, grp,
                  pl.BlockSpec((None, tm, GROUP), lambda b, t: (b, t, gate_b_col)),
                  pl.BlockSpec(w_out.shape, lambda b, t: (0, 0)),
                  pl.BlockSpec((None, 6, d), lambda b, t: (b, 0, 0)),
                  pl.BlockSpec((None, 6, d), lambda b, t: (ctx_row, 0, 0))],
        out_specs=pl.BlockSpec((None, tm, d), tok),
        out_shape=jax.ShapeDtypeStruct((bsz, l, d), F32),
        compiler_params=_cparams("parallel", "parallel"),
        name="post",
    )(x, a_f, a_r, p, b_f, b_r, p, w_out, mod, mod)


def _mlp_body(x_ref, g_ref, ml_ref, mc_ref, w1_ref, w2_ref, fg_ref, o_ref, h_scr, acc_scr, *, tm, n_ctx, final):
    t = pl.program_id(1)
    f = pl.program_id(2)
    if final:
        x_ref = x_ref.at[0]

    @pl.when(f == 0)
    def _():
        y = _rms(x_ref[...], g_ref[...])
        shift = _row_select(t, tm, n_ctx, mc_ref, ml_ref, 3)
        scale = _row_select(t, tm, n_ctx, mc_ref, ml_ref, 4)
        h_scr[...] = (y * (1.0 + scale) + shift).astype(BF16)
        acc_scr[...] = jnp.zeros_like(acc_scr)

    hid = jnp.maximum(jnp.dot(h_scr[...], w1_ref[...], preferred_element_type=F32), 0.0)
    acc_scr[...] += jnp.dot((hid * hid).astype(BF16), w2_ref[...], preferred_element_type=F32)

    @pl.when(f == pl.num_programs(2) - 1)
    def _():
        y = x_ref[...] + _row_select(t, tm, n_ctx, mc_ref, ml_ref, 5) * acc_scr[...]
        if final:
            y = _rms(y, fg_ref[...])
        o_ref[...] = y


def _mlp(x, mod, g, w1, w2, final_g, *, tm, tf, n_ctx, final):
    bsz, l, d = x.shape
    dff = w1.shape[1]
    ctx_row = bsz
    tok = lambda b, t, f: (b, t, 0)
    if final:
        l = l - n_ctx
        x_spec = pl.BlockSpec((pl.Element(1), pl.Element(tm), pl.Element(d)),
                              lambda b, t, f, off=n_ctx: (b, pl.multiple_of(off + t * tm, SUBLANES), 0))
        n_ctx = 0
    else:
        x_spec = pl.BlockSpec((None, tm, d), tok)
    return pl.pallas_call(
        functools.partial(_mlp_body, tm=tm, n_ctx=n_ctx, final=final),
        grid=(bsz, l // tm, dff // tf),
        in_specs=[x_spec,
                  pl.BlockSpec((1, d), lambda b, t, f: (0, 0)),
                  pl.BlockSpec((None, 6, d), lambda b, t, f: (b, 0, 0)),
                  pl.BlockSpec((None, 6, d), lambda b, t, f: (ctx_row, 0, 0)),
                  pl.BlockSpec((d, tf), lambda b, t, f: (0, f)),
                  pl.BlockSpec((tf, d), lambda b, t, f: (f, 0)),
                  pl.BlockSpec((1, d), lambda b, t, f: (0, 0))],
        out_specs=pl.BlockSpec((None, tm, d), tok),
        out_shape=jax.ShapeDtypeStruct((bsz, l, d), F32),
        scratch_shapes=[pltpu.VMEM((tm, d), BF16), pltpu.VMEM((tm, d), F32)],
        compiler_params=_cparams("parallel", "parallel", "arbitrary"),
        name="mlp",
    )(x, g.reshape(1, d), mod, mod, w1, w2, final_g.reshape(1, d))


def _token_tile(l):
    for div in (4, 8, 16, 17, 34):
        if l % div == 0 and (l // div) % 16 == 0 and l // div <= 1088:
            return l // div
    return l


def kernel(x, c, ctx, c_ctx, ada_w, ada_b, norm1_g, norm2_g, mix_w_out, mlp_w1, mlp_w2, ev_w_in, lru_conv_w,
           lru_conv_b, lru_wa, lru_ba, lru_wx, lru_bx, lru_lambda, ret_log_gamma, od_w_in, hg_lb_logits,
           gdn_conv_w, gdn_a_log, gdn_dt_bias, final_g):
    bsz, seq, d = x.shape
    n_ctx = ctx.shape[1]
    depth = ada_w.shape[0]
    l = n_ctx + seq
    tm = _token_tile(l)
    tf = 1024
    nb = next(n for n in (8, 4, 2, 1) if bsz % n == 0)

    mods = _mods(c, c_ctx, ada_w, ada_b)
    ret_dk = (ev_w_in.shape[2] - 4 * GROUP) // (2 * RET_HEADS)
    cos_t, sin_t = _rope_tables(n_ctx, seq, ret_dk)

    xs = jnp.concatenate([ctx, x], axis=1)
    for layer in range(depth):
        mod = mods[layer]
        w_out = mix_w_out[layer].astype(BF16)
        if layer % 2 == 0:
            e = layer // 2
            (p,) = _proj(xs, mod, norm1_g[layer], ev_w_in[e].astype(BF16), None, tm=tm, n_ctx=n_ctx)
            mix = []
            for dr, rev in enumerate((False, True)):
                mix.append((
                    _lru(p, lru_conv_w[e], lru_conv_b[e], _blockdiag_halves(lru_wa[e, dr]).astype(BF16),
                         lru_ba[e, dr], _blockdiag_halves(lru_wx[e, dr]).astype(BF16), lru_bx[e, dr],
                         lru_lambda[e, dr], n_ctx=n_ctx, reverse=rev),
                    _retention(p, cos_t, sin_t, ret_log_gamma[e, dr], n_ctx=n_ctx, reverse=rev, nb=nb)))
            gate_cols = (1, 4)
        else:
            o = layer // 2
            n_main = 9 * GROUP
            w_in = od_w_in[o]
            w_tail = jnp.pad(w_in[:, n_main:], ((0, 0), (0, LANES - (w_in.shape[1] - n_main)))).astype(BF16)
            p, tail = _proj(xs, mod, norm1_g[layer], w_in[:, :n_main].astype(BF16), w_tail, tm=tm, n_ctx=n_ctx)
            qkv = _gdn_prep(p, gdn_conv_w[o], n_ctx=n_ctx)
            mix = []
            for dr, rev in enumerate((False, True)):
                g_col = (2 + dr) * LIN_HEADS
                a_row = jnp.zeros((1, LANES), F32).at[0, g_col:g_col + LIN_HEADS].set(gdn_a_log[o, dr])
                dt_row = jnp.zeros((1, LANES), F32).at[0, g_col:g_col + LIN_HEADS].set(gdn_dt_bias[o, dr])
                mix.append((
                    _gla(p, hg_lb_logits[dr], n_ctx=n_ctx, reverse=rev, layer=o, nb=nb),
                    _gdn(qkv, tail, a_row, dt_row, n_ctx=n_ctx, reverse=rev, direction=dr, nb=nb)))
            gate_cols = (4, 8)
        xs = _post(xs, mod, mix[0][0], mix[1][0], mix[0][1], mix[1][1], p, gate_cols[0], gate_cols[1], w_out,
                   even=layer % 2 == 0, tm=tm, n_ctx=n_ctx)
        last = layer == depth - 1
        xs = _mlp(xs, mod, norm2_g[layer], mlp_w1[layer].astype(BF16), mlp_w2[layer].astype(BF16), final_g,
                  tm=_token_tile(seq) if last else tm, tf=tf, n_ctx=n_ctx, final=last)
    return xs
```

```python
import functools

import numpy as np
import jax
import jax.numpy as jnp
from jax import lax
from jax.experimental import pallas as pl
from jax.experimental.pallas import tpu as pltpu

F32 = jnp.float32
BF16 = jnp.bfloat16

EPS = 1e-6
GRID_W = 64
CONV_W = 4
CONV_LEFT = CONV_W // 2
LRU_BLOCKS = 8
LRU_C = 8.0
RET_HEADS = 4
RET_CHUNK = 256
ROPE_BASE = 10000.0
LIN_HEADS = 4
GLA_CHUNK = 128
LIN_CHUNK = 64
HEAD_DV = 128

LANES = 128
SUBLANES = 8
V7X_VMEM_BYTES = 64 * 1024 * 1024
VMEM_LIMIT = (V7X_VMEM_BYTES * 3) // 4
GROUP = 512
PROJ_MAX_TN = 2560

HALO = 2 * SUBLANES
ACT = jnp.bfloat16


def _cparams(*sem):
    return pltpu.CompilerParams(dimension_semantics=sem, vmem_limit_bytes=VMEM_LIMIT)


def _sigmoid(x):
    return 1.0 / (1.0 + jnp.exp(-x))


def _silu(x):
    return x * _sigmoid(x)


def _softplus(x):
    return jnp.maximum(x, 0.0) + jnp.log(1.0 + jnp.exp(-jnp.abs(x)))


def _gelu_tanh(x):
    return 0.5 * x * (1.0 + jnp.tanh(0.7978845608028654 * (x + 0.044715 * x * x * x)))


def _dot(a, b):
    return jnp.dot(a.astype(BF16), b.astype(BF16), preferred_element_type=F32)


def _dot_nt(a, b):
    return lax.dot_general(a.astype(BF16), b.astype(BF16), (((1,), (1,)), ((), ())),
                           preferred_element_type=F32)


def _split_bf16(x):
    hi = x.astype(BF16)
    lo = (x - hi.astype(F32)).astype(BF16)
    return hi, lo


def _dot_sel(m2, x):
    hi, lo = _split_bf16(x)
    return jnp.dot(m2, jnp.concatenate([hi, lo], axis=0), preferred_element_type=F32)


def _doubled(m):
    return jnp.asarray(np.concatenate([m, m], axis=1), BF16)


def _dot3(a, b):
    ah, al = _split_bf16(a)
    bh, bl = _split_bf16(b)
    return (jnp.dot(ah, bh, preferred_element_type=F32) + jnp.dot(ah, bl, preferred_element_type=F32)
            + jnp.dot(al, bh, preferred_element_type=F32))


def _row_select(t, tm, n_ctx, mc_ref, ml_ref, idx):
    rows = t * tm + lax.broadcasted_iota(jnp.int32, (tm, 1), 0)
    return jnp.where(rows < n_ctx, mc_ref[idx:idx + 1, :], ml_ref[idx:idx + 1, :])


def _rms(x, g):
    return x * lax.rsqrt(jnp.mean(x * x, axis=-1, keepdims=True) + EPS) * g


def _head_norm(o, center):
    outs = []
    for h in range(o.shape[-1] // HEAD_DV):
        oh = o[:, h * HEAD_DV:(h + 1) * HEAD_DV]
        if center:
            oh = oh - jnp.mean(oh, axis=-1, keepdims=True)
        outs.append(oh * lax.rsqrt(jnp.mean(oh * oh, axis=-1, keepdims=True) + EPS))
    return jnp.concatenate(outs, axis=-1)


def _scan_tile(step, n_first, n_total, reverse):
    if not reverse:
        return step
    return jnp.where(step < n_first, n_first - 1 - step, n_total - 1 - (step - n_first))


def _mods_body(s_ref, w_ref, b_ref, o_ref):
    s = _silu(s_ref[...])
    o_ref[...] = _dot3(s, w_ref[...]) + b_ref[...]


def _mods(c, c_ctx, ada_w, ada_b):
    depth, d, n6 = ada_w.shape
    bsz = c.shape[0]
    rows = -(-(bsz + 1) // SUBLANES) * SUBLANES
    s = jnp.zeros((rows, d), F32).at[:bsz].set(c).at[bsz].set(c_ctx)
    tn = 1024
    out = pl.pallas_call(
        _mods_body,
        grid=(depth, n6 // tn),
        in_specs=[pl.BlockSpec((rows, d), lambda l, n: (0, 0)),
                  pl.BlockSpec((None, d, tn), lambda l, n: (l, 0, n)),
                  pl.BlockSpec((None, 1, tn), lambda l, n: (l, 0, n))],
        out_specs=pl.BlockSpec((None, rows, tn), lambda l, n: (l, 0, n)),
        out_shape=jax.ShapeDtypeStruct((depth, rows, n6), F32),
        compiler_params=_cparams("parallel", "parallel"),
        name="mods",
    )(s, ada_w, ada_b.reshape(depth, 1, n6))
    return out.reshape(depth, rows, 6, d)


def _proj_body(*refs, tm, n_ctx, n_tok_tiles, ng, has_tail):
    if has_tail:
        x_ref, ml_ref, mc_ref, g_ref, w_ref, wt_ref, o_ref, ot_ref, h_scr = refs
    else:
        x_ref, ml_ref, mc_ref, g_ref, w_ref, o_ref, h_scr = refs
    g = pl.program_id(0)
    n = pl.program_id(1)
    n_tiles = pl.num_programs(0) - 1
    slot = g % 2
    t = jnp.minimum(g, n_tiles - 1) % n_tok_tiles

    def normalise():
        y = _rms(x_ref[...], g_ref[...])
        shift = _row_select(t, tm, n_ctx, mc_ref, ml_ref, 0)
        scale = _row_select(t, tm, n_ctx, mc_ref, ml_ref, 1)
        h_scr[slot] = (y * (1.0 + scale) + shift).astype(BF16)

    def multiply():
        o_ref[...] = jnp.dot(h_scr[1 - slot], w_ref[...], preferred_element_type=F32).astype(o_ref.dtype)

    def multiply_tail():
        ot_ref[...] = jnp.dot(h_scr[1 - slot], wt_ref[...], preferred_element_type=F32)

    last_n = n == ng - 1

    tn = w_ref.shape[1]
    col_chunks = next(k for k in (4, 5, 3, 2, 1) if tn % (k * 2 * LANES) == 0)
    row_chunks = next(k for k in (4, 2, 1) if tm % (k * 2 * SUBLANES) == 0)

    def normalise_rows(r):
        rows = pl.ds(r * (tm // row_chunks), tm // row_chunks)
        y = _rms(x_ref[rows, :], g_ref[...])
        row0 = t * tm + r * (tm // row_chunks)
        ridx = row0 + lax.broadcasted_iota(jnp.int32, (tm // row_chunks, 1), 0)
        shift = jnp.where(ridx < n_ctx, mc_ref[0:1, :], ml_ref[0:1, :])
        scale = jnp.where(ridx < n_ctx, mc_ref[1:2, :], ml_ref[1:2, :])
        h_scr[slot, rows, :] = (y * (1.0 + scale) + shift).astype(BF16)

    def multiply_cols(j):
        cols = pl.ds(j * (tn // col_chunks), tn // col_chunks)
        o_ref[:, cols] = jnp.dot(h_scr[1 - slot], w_ref[:, cols], preferred_element_type=F32).astype(o_ref.dtype)

    @pl.when(jnp.logical_and(last_n, g > 0))
    def _():
        for j in range(max(col_chunks, row_chunks)):
            if j < col_chunks:
                multiply_cols(j)
            if j < row_chunks:
                normalise_rows(j)
        if has_tail and ng == 1:
            multiply_tail()

    @pl.when(jnp.logical_and(last_n, g == 0))
    def _():
        normalise()

    if ng > 1:
        @pl.when(jnp.logical_and(jnp.logical_not(last_n), g > 0))
        def _():
            multiply()
            if has_tail:
                pl.when(n == 0)(multiply_tail)


def _proj(x, mod, g, w, w_tail, *, tm, n_ctx):
    bsz, l, d = x.shape
    n_out = w.shape[1]
    tn = max(t for t in range(LANES, PROJ_MAX_TN + 1, LANES) if n_out % t == 0)
    ng = n_out // tn
    ctx_row = bsz
    n_tok_tiles = l // tm
    n_tiles = bsz * n_tok_tiles

    def tile_in(s):
        s = jnp.minimum(s, n_tiles - 1)
        return s // n_tok_tiles, s % n_tok_tiles

    def tile_out(s):
        s = jnp.maximum(s - 1, 0)
        return s // n_tok_tiles, s % n_tok_tiles

    in_specs = [pl.BlockSpec((None, tm, d), lambda s, n: (*tile_in(s), 0)),
                pl.BlockSpec((None, 6, d), lambda s, n: (tile_in(s)[0], 0, 0)),
                pl.BlockSpec((None, 6, d), lambda s, n: (ctx_row, 0, 0)),
                pl.BlockSpec((1, d), lambda s, n: (0, 0)),
                pl.BlockSpec((d, tn), lambda s, n: (0, n))]
    out_specs = [pl.BlockSpec((None, tm, tn), lambda s, n: (*tile_out(s), jnp.where(s == 0, 0, n)))]
    out_shape = [jax.ShapeDtypeStruct((bsz, l, n_out), ACT)]
    args = [x, mod, mod, g.reshape(1, d), w]
    if w_tail is not None:
        in_specs.append(pl.BlockSpec((d, LANES), lambda s, n: (0, 0)))
        out_specs.append(pl.BlockSpec((None, tm, LANES), lambda s, n: (*tile_out(s), 0)))
        out_shape.append(jax.ShapeDtypeStruct((bsz, l, LANES), F32))
        args.append(w_tail)
    outs = pl.pallas_call(
        functools.partial(_proj_body, tm=tm, n_ctx=n_ctx, n_tok_tiles=n_tok_tiles, ng=ng,
                          has_tail=w_tail is not None),
        grid=(n_tiles + 1, ng),
        in_specs=in_specs, out_specs=out_specs, out_shape=out_shape,
        scratch_shapes=[pltpu.VMEM((2, tm, d), BF16)],
        compiler_params=_cparams("arbitrary", "arbitrary"),
        name="proj",
    )(*args)
    return outs


def _lru_body(sm_ref, se_ref, x_ref, xp_ref, xn_ref, cw_ref, cb_ref, wa_ref, ba_ref, wx_ref, bx_ref, lam_ref,
              o_ref, a_scr, b_scr, h_scr, *, tl, n_ctx_tiles, n_tiles, reverse):
    step = pl.program_id(0)
    tile = _scan_tile(step, n_ctx_tiles, n_tiles, reverse)
    bsz = x_ref.shape[0]
    width = x_ref.shape[2]
    half = width // 2
    seg_first = jnp.logical_or(tile == 0, tile == n_ctx_tiles)
    seg_last = jnp.logical_or(tile == n_ctx_tiles - 1, tile == n_tiles - 1)
    keep_prev = jnp.where(seg_first, 0.0, 1.0)
    keep_next = jnp.where(seg_last, 0.0, 1.0)
    neg_c_sp = -LRU_C * _softplus(-lam_ref[...])

    @pl.when(step == 0)
    def _():
        h_scr[...] = jnp.zeros_like(h_scr)

    pair = 2 if bsz % 2 == 0 else 1

    def block_dot(ub, w_ref):
        return jnp.concatenate([jnp.dot(ub[:, :half], w_ref[0], preferred_element_type=F32),
                                jnp.dot(ub[:, half:], w_ref[1], preferred_element_type=F32)], axis=-1)

    def gates(bp, carry):
        bs = [bp * pair + n for n in range(pair)]
        taps = [_conv_taps(sm_ref, se_ref, x_ref[b], xp_ref[b], xn_ref[b], keep_prev, keep_next, tl) for b in bs]
        u = [cb_ref[...] + sum(cw_ref[k:k + 1, :] * tp[k] for k in range(CONV_W)) for tp in taps]
        ub = [x.astype(BF16) for x in u]
        r_pre = [block_dot(x, wa_ref) + ba_ref[...] for x in ub]
        i_pre = [block_dot(x, wx_ref) + bx_ref[...] for x in ub]
        a = [jnp.exp(neg_c_sp * _sigmoid(x)) for x in r_pre]
        for n, b in enumerate(bs):
            a_scr[b] = a[n]
            b_scr[b] = jnp.sqrt(1.0 - a[n] * a[n]) * (_sigmoid(i_pre[n]) * u[n])
        return carry

    lax.fori_loop(0, bsz // pair, gates, 0)

    def scan_rows(j, hs):
        idx = (tl - 1 - j) if reverse else j
        new = []
        for b in range(bsz):
            h = a_scr[b, pl.ds(idx, 1), :] * hs[b] + b_scr[b, pl.ds(idx, 1), :]
            b_scr[b, pl.ds(idx, 1), :] = h
            new.append(h)
        return tuple(new)

    hs = tuple(h_scr[b, 0:1, :] for b in range(bsz))
    hs = lax.fori_loop(0, tl, scan_rows, hs)
    for b in range(bsz):
        h_scr[b, 0:1, :] = hs[b]
    o_ref[...] = b_scr[...].astype(o_ref.dtype)


def _lru(p, conv_w, conv_b, wa, ba, wx, bx, lam, *, n_ctx, reverse, tl=256):
    bsz, l, _ = p.shape
    width = GROUP
    n_tiles = l // tl
    n_ctx_tiles = n_ctx // tl
    hb = tl // HALO
    n_hblocks = l // HALO

    def tile_of(s):
        return _scan_tile(s, n_ctx_tiles, n_tiles, reverse)

    vec = pl.BlockSpec((1, width), lambda s: (0, 0))
    mat = pl.BlockSpec((2, width // 2, width // 2), lambda s: (0, 0, 0))
    sm, se = (jnp.asarray(m, p.dtype) for m in _shift_matrices(tl))
    return pl.pallas_call(
        functools.partial(_lru_body, tl=tl, n_ctx_tiles=n_ctx_tiles, n_tiles=n_tiles, reverse=reverse),
        grid=(n_tiles,),
        in_specs=[pl.BlockSpec(sm.shape, lambda s: (0, 0)), pl.BlockSpec(se.shape, lambda s: (0, 0)),
                  pl.BlockSpec((bsz, tl, width), lambda s: (0, tile_of(s), 0)),
                  pl.BlockSpec((bsz, HALO, width), lambda s: (0, jnp.maximum(tile_of(s) * hb - 1, 0), 0)),
                  pl.BlockSpec((bsz, HALO, width),
                               lambda s: (0, jnp.minimum((tile_of(s) + 1) * hb, n_hblocks - 1), 0)),
                  pl.BlockSpec((CONV_W, width), lambda s: (0, 0)), vec, mat, vec, mat, vec, vec],
        out_specs=pl.BlockSpec((bsz, tl, width), lambda s: (0, tile_of(s), 0)),
        out_shape=jax.ShapeDtypeStruct((bsz, l, width), ACT),
        scratch_shapes=[pltpu.VMEM((bsz, tl, width), F32),
                        pltpu.VMEM((bsz, tl, width), F32),
                        pltpu.VMEM((bsz, SUBLANES, width), F32)],
        compiler_params=_cparams("arbitrary"),
        name="lru_rev" if reverse else "lru_fwd",
    )(sm, se, p, p, p, conv_w, conv_b.reshape(1, width), wa, ba.reshape(1, width), wx, bx.reshape(1, width),
      lam.reshape(1, width))


def _blockdiag_halves(w):
    nb, n, _ = w.shape
    per = nb // 2
    w = w.reshape(2, per, n, n)
    eye = jnp.eye(per, dtype=w.dtype)
    out = w[:, :, :, None, :] * eye[None, :, None, :, None]
    return out.reshape(2, per * n, per * n)


def _ret_body(lg_ref, q_ref, k_ref, v_ref, cos_ref, sin_ref, o_ref, dec_scr, qd_scr, kd_scr, cd_scr, st_scr,
              *, reverse, dk, nb):
    c = RET_CHUNK
    step = pl.program_id(1)

    @pl.when(step == 0)
    def _():
        st_scr[...] = jnp.zeros_like(st_scr)
        i = lax.broadcasted_iota(jnp.int32, (c, c), 0)
        j = lax.broadcasted_iota(jnp.int32, (c, c), 1)
        diff = (j - i) if reverse else (i - j)
        row = lax.broadcasted_iota(jnp.int32, (c, HEAD_DV), 0)
        pos = ((c - 1 - row) if reverse else row).astype(F32)
        dpos = jnp.maximum(diff, 0).astype(F32)
        for h in range(RET_HEADS):
            lg = lg_ref[h]
            dec_scr[h] = jnp.where(diff >= 0, jnp.exp(lg * dpos), 0.0)
            qd_scr[h] = jnp.exp(lg * (pos + 1.0))
            kd_scr[h] = jnp.exp(lg * (c - 1.0 - pos))
            cd_scr[h] = jnp.exp(jnp.full((SUBLANES, LANES), c, F32) * lg)

    def rope(x):
        width = x.shape[-1]
        lane = lax.broadcasted_iota(jnp.int32, x.shape, 1)
        first = (lane & (dk - 1)) < dk // 2
        rot = jnp.where(first, pltpu.roll(x, width - dk // 2, axis=1), pltpu.roll(x, dk // 2, axis=1))
        return x * cos_ref[...] + rot * sin_ref[...]

    q = [rope(q_ref[bb].astype(F32)) for bb in range(nb)]
    k = [rope(k_ref[bb].astype(F32)) * (dk ** -0.5) for bb in range(nb)]
    pairs = [(bb, h) for bb in range(nb) for h in range(RET_HEADS)]
    qh = [q[bb][:, h * dk:(h + 1) * dk] for bb, h in pairs]
    kh = [k[bb][:, h * dk:(h + 1) * dk] for bb, h in pairs]
    vh = [v_ref[bb, :, h * HEAD_DV:(h + 1) * HEAD_DV] for bb, h in pairs]
    s = [_dot_nt(qh[i], kh[i]) * dec_scr[h] for i, (bb, h) in enumerate(pairs)]
    state = [st_scr[bb, h] for bb, h in pairs]
    o_x = [_dot(qh[i], state[i]) * qd_scr[h] for i, (bb, h) in enumerate(pairs)]
    o_in = [_dot(s[i], vh[i]) for i in range(len(pairs))]
    kv = [_dot((kh[i] * kd_scr[h][:, :dk]).T, vh[i]) for i, (bb, h) in enumerate(pairs)]
    for i, (bb, h) in enumerate(pairs):
        st_scr[bb, h] = state[i] * cd_scr[h][0:1, :] + kv[i]
        o_ref[bb, :, h * HEAD_DV:(h + 1) * HEAD_DV] = (o_in[i] + o_x[i]).astype(o_ref.dtype)


def _retention(p, cos_t, sin_t, log_gamma, *, n_ctx, reverse, nb):
    bsz, l, _ = p.shape
    c = RET_CHUNK
    qk_w = cos_t.shape[1]
    dk = qk_w // RET_HEADS
    n_chunks = l // c
    n_ctx_chunks = n_ctx // c

    def ch(s):
        return _scan_tile(s, n_ctx_chunks, n_chunks, reverse)

    q_col = (2 * GROUP) // qk_w
    return pl.pallas_call(
        functools.partial(_ret_body, reverse=reverse, dk=dk, nb=nb),
        grid=(bsz // nb, n_chunks),
        in_specs=[pl.BlockSpec(memory_space=pltpu.SMEM),
                  pl.BlockSpec((nb, c, qk_w), lambda b, s: (b, ch(s), q_col)),
                  pl.BlockSpec((nb, c, qk_w), lambda b, s: (b, ch(s), q_col + 1)),
                  pl.BlockSpec((nb, c, GROUP), lambda b, s: (b, ch(s), 3)),
                  pl.BlockSpec((c, qk_w), lambda b, s: (ch(s), 0)),
                  pl.BlockSpec((c, qk_w), lambda b, s: (ch(s), 0))],
        out_specs=pl.BlockSpec((nb, c, GROUP), lambda b, s: (b, ch(s), 0)),
        out_shape=jax.ShapeDtypeStruct((bsz, l, GROUP), ACT),
        scratch_shapes=[pltpu.VMEM((RET_HEADS, c, c), F32),
                        pltpu.VMEM((RET_HEADS, c, HEAD_DV), F32),
                        pltpu.VMEM((RET_HEADS, c, HEAD_DV), F32),
                        pltpu.VMEM((RET_HEADS, SUBLANES, LANES), F32),
                        pltpu.VMEM((nb, RET_HEADS, dk, HEAD_DV), F32)],
        compiler_params=_cparams("parallel", "arbitrary"),
        name="ret_rev" if reverse else "ret_fwd",
    )(log_gamma, p, p, p, cos_t, sin_t)


def _rope_tables(n_ctx, seq, dk):
    n_freq = dk // 4
    inv = jnp.power(ROPE_BASE, -jnp.arange(n_freq, dtype=F32) / n_freq)
    rows = seq // GRID_W
    r = jnp.arange(rows, dtype=F32)
    col = jnp.arange(GRID_W, dtype=F32)
    row_ang = jnp.broadcast_to(r[:, None, None] * inv, (rows, GRID_W, n_freq))
    col_ang = jnp.broadcast_to(col[None, :, None] * inv, (rows, GRID_W, n_freq))
    ang = jnp.concatenate([row_ang, col_ang], axis=-1).reshape(rows * GRID_W, 2 * n_freq)
    cos, sin = jnp.cos(ang), jnp.sin(ang)
    cos_h = jnp.concatenate([cos, cos], axis=-1)
    sin_h = jnp.concatenate([-sin, sin], axis=-1)
    cos_t = jnp.concatenate([jnp.ones((n_ctx, dk), F32), cos_h], axis=0)
    sin_t = jnp.concatenate([jnp.zeros((n_ctx, dk), F32), sin_h], axis=0)
    return jnp.tile(cos_t, (1, RET_HEADS)), jnp.tile(sin_t, (1, RET_HEADS))


_GLA_LEVELS = tuple(GLA_CHUNK >> (i + 1) for i in range(GLA_CHUNK.bit_length() - 1))


def _gla_matrices(reverse):
    c = GLA_CHUNK
    p = np.arange(c)
    mats = [(p[None, :] <= p[:, None]), (p[None, :] > p[:, None])]
    for s in _GLA_LEVELS:
        bs = (p // s) * s
        right = ((p // s) % 2 == 1)
        m_r = (p[None, :] > bs[:, None]) & (p[None, :] <= p[:, None])
        m_l = (p[None, :] > p[:, None]) & (p[None, :] <= (bs + s)[:, None])
        mats.append(np.where(right[:, None], m_r, m_l))
    m = np.stack(mats).astype(np.float32)
    if reverse:
        m = m[:, ::-1, ::-1]
    return m.reshape(len(mats) * c, c)


def _gla_body(m_ref, lb_ref, q_ref, f_ref, v_ref, o_ref, st_scr, *, reverse, layer, nb):
    c = GLA_CHUNK
    step = pl.program_id(1)

    @pl.when(step == 0)
    def _():
        st_scr[...] = jnp.zeros_like(st_scr)

    lg = lb_ref[...]
    e = jnp.exp(lg - jnp.max(lg, axis=0, keepdims=True))
    sm = e / jnp.sum(e, axis=0, keepdims=True)
    lb = jnp.sum(sm[0:layer + 1, :], axis=0, keepdims=True) - sm[0:1, :]

    ri = lax.broadcasted_iota(jnp.int32, (c, 1), 0)
    i2 = lax.broadcasted_iota(jnp.int32, (c, c), 0)
    j2 = lax.broadcasted_iota(jnp.int32, (c, c), 1)
    if reverse:
        ri, i2, j2 = c - 1 - ri, c - 1 - i2, c - 1 - j2
    last = 0 if reverse else c - 1
    width = q_ref.shape[-1]
    nh = width // HEAD_DV

    rows_bb = list(range(nb))
    pairs = [(bb, h) for bb in rows_bb for h in range(nh)]

    def head(x, h):
        return x[:, h * HEAD_DV:(h + 1) * HEAD_DV]

    f = [lb + (1.0 - lb) * _sigmoid(f_ref[bb].astype(F32)) for bb in rows_bb]
    k = [1.0 - x for x in f]
    q = [_silu(q_ref[bb].astype(F32)) for bb in rows_bb]
    ex = [_dot_sel(m_ref[...], jnp.log(x)) for x in f]
    a_last = [jnp.exp(x[last:last + 1, :]) for x in ex]
    qa = [q[bb] * jnp.exp(ex[bb][0:c, :]) for bb in rows_bb]
    kl = [k[bb] * jnp.exp(ex[bb][c:2 * c, :]) for bb in rows_bb]

    qb = [x.astype(BF16) for x in q]
    kb = [x.astype(BF16) for x in k]
    scores = [jnp.where(i2 == j2, _dot_nt(head(qb[bb], h), head(kb[bb], h)), 0.0) for bb, h in pairs]
    for lvl, s in enumerate(_GLA_LEVELS):
        sh = s.bit_length() - 1
        right = jnp.broadcast_to(((ri >> sh) & 1).astype(F32), (c, width)).astype(BF16) > 0
        x = [jnp.where(right, qb[bb], kb[bb]) * jnp.exp(ex[bb][(2 + lvl) * c:(3 + lvl) * c, :].astype(BF16))
             for bb in rows_bb]
        bi, bj = i2 >> sh, j2 >> sh
        valid = bj == jnp.where((bi & 1) == 1, bi - 1, -1)
        gram = [_dot_nt(head(x[bb], h), head(x[bb], h)) for bb, h in pairs]
        scores = [jnp.where(valid, gm, sc) for sc, gm in zip(scores, gram)]

    state_t = [st_scr[bb, h] for bb, h in pairs]
    o_intra = [_dot(scores[i], head(v_ref[bb], h)) for i, (bb, h) in enumerate(pairs)]
    o_inter = [_dot_nt(head(qa[bb], h), state_t[i]) for i, (bb, h) in enumerate(pairs)]
    kv = [_dot(head(v_ref[bb], h).astype(F32).T, head(kl[bb], h)) for bb, h in pairs]
    for i, (bb, h) in enumerate(pairs):
        o_ref[bb, :, h * HEAD_DV:(h + 1) * HEAD_DV] = (o_intra[i] + o_inter[i]).astype(o_ref.dtype)
        st_scr[bb, h] = state_t[i] * head(a_last[bb], h) + kv[i]


def _gla(p, lb_logits, *, n_ctx, reverse, layer, nb):
    bsz, l, _ = p.shape
    c = GLA_CHUNK
    n_chunks = l // c
    n_ctx_chunks = n_ctx // c
    m = _doubled(_gla_matrices(reverse))

    def ch(s):
        return _scan_tile(s, n_ctx_chunks, n_chunks, reverse)

    f_col = 2 if reverse else 1
    return pl.pallas_call(
        functools.partial(_gla_body, reverse=reverse, layer=layer, nb=nb),
        grid=(bsz // nb, n_chunks),
        in_specs=[pl.BlockSpec(m.shape, lambda b, s: (0, 0)),
                  pl.BlockSpec(lb_logits.shape, lambda b, s: (0, 0)),
                  pl.BlockSpec((nb, c, GROUP), lambda b, s: (b, ch(s), 0)),
                  pl.BlockSpec((nb, c, GROUP), lambda b, s: (b, ch(s), f_col)),
                  pl.BlockSpec((nb, c, GROUP), lambda b, s: (b, ch(s), 3))],
        out_specs=pl.BlockSpec((nb, c, GROUP), lambda b, s: (b, ch(s), 0)),
        out_shape=jax.ShapeDtypeStruct((bsz, l, GROUP), ACT),
        scratch_shapes=[pltpu.VMEM((nb, LIN_HEADS, HEAD_DV, HEAD_DV), F32)],
        compiler_params=_cparams("parallel", "arbitrary"),
        name="gla_rev" if reverse else "gla_fwd",
    )(m, lb_logits, p, p, p)


def _shift_matrices(rows):
    offs = [t - CONV_LEFT for t in range(CONV_W) if t != CONV_LEFT]
    main = np.zeros((len(offs) * rows, rows), np.float32)
    edge = np.zeros((len(offs) * SUBLANES, 2 * HALO), np.float32)
    for n, off in enumerate(offs):
        for i in range(rows):
            j = i + off
            if 0 <= j < rows:
                main[n * rows + i, j] = 1.0
            elif j < 0:
                edge[n * SUBLANES + i, HALO + j] = 1.0
            else:
                edge[n * SUBLANES + i - (rows - SUBLANES), HALO + j - rows] = 1.0
    return main, edge


def _conv_taps(sm_ref, se_ref, x, xp, xn, keep_prev, keep_next, rows):
    sh = jnp.dot(sm_ref[...], x, preferred_element_type=F32)
    halo = jnp.concatenate([xp * keep_prev.astype(xp.dtype), xn * keep_next.astype(xn.dtype)], axis=0)
    fix = jnp.dot(se_ref[...], halo, preferred_element_type=F32)
    taps, n = [], 0
    for t in range(CONV_W):
        off = t - CONV_LEFT
        if off == 0:
            taps.append(x.astype(F32))
            continue
        s = sh[n * rows:(n + 1) * rows, :]
        f = fix[n * SUBLANES:(n + 1) * SUBLANES, :]
        if off < 0:
            s = jnp.concatenate([s[0:SUBLANES] + f, s[SUBLANES:]], axis=0)
        else:
            s = jnp.concatenate([s[:rows - SUBLANES], s[rows - SUBLANES:] + f], axis=0)
        taps.append(s)
        n += 1
    return taps


def _gdn_prep_body(sm_ref, se_ref, cw_ref, q_ref, qp_ref, qn_ref, k_ref, kp_ref, kn_ref, v_ref, vp_ref, vn_ref,
                   qo_ref, ko_ref, vo_ref, *, tp, n_ctx_tiles, n_tiles):
    c = LIN_CHUNK
    nh = LIN_HEADS
    tile = pl.program_id(1)
    seg_first = jnp.logical_or(tile == 0, tile == n_ctx_tiles)
    seg_last = jnp.logical_or(tile == n_ctx_tiles - 1, tile == n_tiles - 1)
    keep_prev = jnp.where(seg_first, 0.0, 1.0)
    keep_next = jnp.where(seg_last, 0.0, 1.0)

    all_taps = [_conv_taps(sm_ref, se_ref, x[...], xp[...], xn[...], keep_prev, keep_next, tp)
                for x, xp, xn in ((q_ref, qp_ref, qn_ref), (k_ref, kp_ref, kn_ref), (v_ref, vp_ref, vn_ref))]

    def conv_silu(col):
        taps = all_taps[col]
        u = cw_ref[0:1, col * GROUP:(col + 1) * GROUP] * taps[0]
        for t in range(1, CONV_W):
            u = u + cw_ref[t:t + 1, col * GROUP:(col + 1) * GROUP] * taps[t]
        return _silu(u)

    def l2n(x):
        return x * lax.rsqrt(jnp.sum(x * x, axis=-1, keepdims=True) + EPS)

    def emit(out_ref, x, norm, scale):
        for h in range(nh):
            xh = x[:, h * HEAD_DV:(h + 1) * HEAD_DV]
            if norm:
                xh = l2n(xh) * scale
            for j in range(tp // c):
                out_ref[j, h * c:(h + 1) * c, :] = xh[j * c:(j + 1) * c, :].astype(out_ref.dtype)

    emit(qo_ref, conv_silu(0), True, HEAD_DV ** -0.5)
    emit(ko_ref, conv_silu(1), True, 1.0)
    emit(vo_ref, conv_silu(2), False, 1.0)


def _gdn_prep(p, conv_w, *, n_ctx, tp=256):
    bsz, l, _ = p.shape
    c = LIN_CHUNK
    n_tiles = l // tp
    hb = tp // HALO
    n_hblocks = l // HALO

    def triple(col):
        return [pl.BlockSpec((None, tp, GROUP), lambda b, t: (b, t, col)),
                pl.BlockSpec((None, HALO, GROUP), lambda b, t: (b, jnp.maximum(t * hb - 1, 0), col)),
                pl.BlockSpec((None, HALO, GROUP), lambda b, t: (b, jnp.minimum((t + 1) * hb, n_hblocks - 1), col))]

    out_spec = pl.BlockSpec((None, tp // c, LIN_HEADS * c, HEAD_DV), lambda b, t: (b, t, 0, 0))
    out_sds = jax.ShapeDtypeStruct((bsz, l // c, LIN_HEADS * c, HEAD_DV), ACT)
    sm, se = (jnp.asarray(m, p.dtype) for m in _shift_matrices(tp))
    const2 = lambda b, t: (0, 0)
    return pl.pallas_call(
        functools.partial(_gdn_prep_body, tp=tp, n_ctx_tiles=n_ctx // tp, n_tiles=n_tiles),
        grid=(bsz, n_tiles),
        in_specs=[pl.BlockSpec(sm.shape, const2), pl.BlockSpec(se.shape, const2), pl.BlockSpec(conv_w.shape, const2)]
                 + triple(5) + triple(6) + triple(7),
        out_specs=[out_spec] * 3, out_shape=[out_sds] * 3,
        compiler_params=_cparams("parallel", "parallel"),
        name="gdn_prep",
    )(sm, se, conv_w, *([p] * 9))


def _gdn_body(cs_ref, arow_ref, dtrow_ref, t_ref, q_ref, k_ref, v_ref, o_ref, st_scr,
              *, reverse, beta_col, g_col, nb):
    c = LIN_CHUNK
    nh = LIN_HEADS
    n = nh * c
    step = pl.program_id(1)

    @pl.when(step == 0)
    def _():
        st_scr[...] = jnp.zeros_like(st_scr)

    i2 = lax.broadcasted_iota(jnp.int32, (c, n), 0)
    lane = lax.broadcasted_iota(jnp.int32, (c, n), 1)
    j2 = lane & (c - 1)
    grp = lane >> 6
    mask = (j2 >= i2) if reverse else (j2 <= i2)
    same32 = (i2 >> 5) == (j2 >> 5)
    same16 = (i2 >> 4) == (j2 >> 4)
    eye = jnp.where(i2 == j2, 1.0, 0.0)
    grp_rows = [jnp.where((lax.broadcasted_iota(jnp.int32, (1, n), 1) >> 6) == h, 1.0, 0.0).astype(BF16)
                for h in range(nh)]
    last = 0 if reverse else c - 1

    def each(fn, *lists):
        return [fn(*args) for args in zip(*lists)]

    def head_bcast(x, col):
        return jnp.concatenate([jnp.broadcast_to(x[:, col + h:col + h + 1], (c, HEAD_DV)) for h in range(nh)], axis=0)

    def side_by_side(x):
        out = x[(nh - 1) * c:nh * c, :]
        for h in range(nh - 2, -1, -1):
            out = jnp.where(grp == h, x[h * c:(h + 1) * c, :], out)
        return out

    def block_diag(x):
        xb = x.astype(BF16)
        return jnp.concatenate([xb * grp_rows[h] for h in range(nh)], axis=0)

    def dot_sbs(x, y_bd):
        return jnp.dot(x.astype(BF16), y_bd, preferred_element_type=F32)

    rows_bb = list(range(nb))
    q = [q_ref[bb].astype(F32) for bb in rows_bb]
    k = [k_ref[bb].astype(F32) for bb in rows_bb]
    v = [v_ref[bb].astype(F32) for bb in rows_bb]

    tail = [t_ref[bb] for bb in rows_bb]
    beta = [head_bcast(_sigmoid(x), beta_col) for x in tail]
    g = [head_bcast(-jnp.exp(arow_ref[...]) * _softplus(x + dtrow_ref[...]), g_col) for x in tail]
    gc = [_dot_sel(cs_ref[...], x) for x in g]
    gc_last = [jnp.concatenate([jnp.broadcast_to(x[h * c + last:h * c + last + 1, :], (c, HEAD_DV))
                                for h in range(nh)], axis=0) for x in gc]
    diff = [side_by_side(jnp.concatenate([x, x], axis=1)) - jnp.transpose(x)[0:1, :] for x in gc]
    gam = [jnp.where(mask, jnp.exp(jnp.where(mask, x, 0.0)), 0.0) for x in diff]

    kb = each(lambda a, b: a * b, k, beta)
    kq = each(lambda a, b, kx: _dot_nt(jnp.concatenate([a, b], axis=0), kx), kb, q, k)
    mm = each(lambda x, gm: jnp.where(i2 == j2, 0.0, side_by_side(x[:n]) * gm), kq, gam)
    qk = each(lambda x, gm: side_by_side(x[n:]) * gm, kq, gam)
    egc = [jnp.exp(x) for x in gc]
    rhs = each(lambda a, b, kbx, e: jnp.concatenate([a * b, kbx * e], axis=1), v, beta, kb, egc)

    md = [jnp.where(same16, x, 0.0) for x in mm]
    m2 = each(lambda a: dot_sbs(a, block_diag(a)), md)
    t = [eye - x for x in md]
    tm = each(lambda a, b: dot_sbs(jnp.concatenate([a, b], axis=0), block_diag(b)), t, m2)
    t = each(lambda a, x: a + x[:c], t, tm)
    m4 = [x[c:] for x in tm]
    tm = each(lambda a, b: dot_sbs(jnp.concatenate([a, b], axis=0), block_diag(b)), t, m4)
    t = each(lambda a, x: a + x[:c], t, tm)
    t = each(lambda a, x: a + dot_sbs(a, block_diag(x[c:])), t, tm)
    c32 = each(lambda a, b: jnp.where(same32, a - b, 0.0), mm, md)
    ct = each(lambda a, b: dot_sbs(a, block_diag(b)), c32, t)
    t = each(lambda a, b: a - dot_sbs(a, block_diag(b)), t, ct)
    c64 = [jnp.where(same32, 0.0, x) for x in mm]
    ct = each(lambda a, b: dot_sbs(a, block_diag(b)), c64, t)
    t = each(lambda a, b: a - dot_sbs(a, block_diag(b)), t, ct)
    sol = each(lambda a, b: jnp.dot(block_diag(a), b.astype(BF16), preferred_element_type=F32), t, rhs)

    qd = each(lambda a, e: a * e, q, egc)
    kd = each(lambda a, gl, x: a * jnp.exp(gl - x), k, gc_last, gc)
    g_last = [jnp.exp(x) for x in gc_last]
    heads = list(range(nh))
    state = [[st_scr[bb, h] for h in heads] for bb in rows_bb]
    ws = [[_dot(jnp.concatenate([sol[bb][h * c:(h + 1) * c, HEAD_DV:], qd[bb][h * c:(h + 1) * c]], axis=0),
                state[bb][h]) for h in heads] for bb in rows_bb]
    v_new = [jnp.concatenate([sol[bb][h * c:(h + 1) * c, :HEAD_DV] - ws[bb][h][:c] for h in heads], axis=0)
             for bb in rows_bb]
    o_intra = each(lambda a, b: jnp.dot(block_diag(a), b.astype(BF16), preferred_element_type=F32), qk, v_new)
    for h in heads:
        rows = slice(h * c, (h + 1) * c)
        for bb in rows_bb:
            o_ref[bb, :, h * HEAD_DV:(h + 1) * HEAD_DV] = (o_intra[bb][rows] + ws[bb][h][c:]).astype(o_ref.dtype)
            st_scr[bb, h] = (state[bb][h] * g_last[bb][h * c:h * c + 1, :]
                             + _dot(kd[bb][rows].T, v_new[bb][rows]))


def _gdn_cumsum_matrix(reverse):
    c, nh = LIN_CHUNK, LIN_HEADS
    p = np.arange(c)
    tri = (p[None, :] >= p[:, None]) if reverse else (p[None, :] <= p[:, None])
    return np.kron(np.eye(nh), tri.astype(np.float32))


def _gdn(qkv, tail, a_row, dt_row, *, n_ctx, reverse, direction, nb):
    q, k, v = qkv
    bsz, n_chunks, n, _ = q.shape
    c = LIN_CHUNK
    n_ctx_chunks = n_ctx // c
    cs = _doubled(_gdn_cumsum_matrix(reverse))

    def ch(s):
        return _scan_tile(s, n_ctx_chunks, n_chunks, reverse)

    const2 = lambda b, s: (0, 0)
    stacked = pl.BlockSpec((nb, None, n, HEAD_DV), lambda b, s: (b, ch(s), 0, 0))
    return pl.pallas_call(
        functools.partial(_gdn_body, reverse=reverse, beta_col=direction * LIN_HEADS,
                          g_col=(2 + direction) * LIN_HEADS, nb=nb),
        grid=(bsz // nb, n_chunks),
        in_specs=[pl.BlockSpec(cs.shape, const2),
                  pl.BlockSpec((1, LANES), const2), pl.BlockSpec((1, LANES), const2),
                  pl.BlockSpec((nb, c, LANES), lambda b, s: (b, ch(s), 0)), stacked, stacked, stacked],
        out_specs=pl.BlockSpec((nb, c, GROUP), lambda b, s: (b, ch(s), 0)),
        out_shape=jax.ShapeDtypeStruct((bsz, n_chunks * c, GROUP), ACT),
        scratch_shapes=[pltpu.VMEM((nb, LIN_HEADS, HEAD_DV, HEAD_DV), F32)],
        compiler_params=_cparams("parallel", "arbitrary"),
        name="gdn_rev" if reverse else "gdn_fwd",
    )(cs, a_row, dt_row, tail, q, k, v)


def _post_body(x_ref, af_ref, ar_ref, ga_ref, bf_ref, br_ref, gb_ref, w_ref, ml_ref, mc_ref, o_ref,
               *, even, tm, n_ctx):
    t = pl.program_id(1)
    a = af_ref[...].astype(F32) + ar_ref[...].astype(F32)
    bsum = bf_ref[...].astype(F32) + br_ref[...].astype(F32)
    if even:
        ya = a * _gelu_tanh(ga_ref[...].astype(F32))
        yb = _head_norm(bsum, True) * _silu(gb_ref[...].astype(F32))
    else:
        ya = _head_norm(a, False) * _silu(ga_ref[...].astype(F32))
        yb = _head_norm(bsum, False) * _silu(gb_ref[...].astype(F32))
    half = w_ref.shape[0] // 2
    mix = jnp.dot(ya.astype(BF16), w_ref[0:half, :], preferred_element_type=F32)
    mix = mix + jnp.dot(yb.astype(BF16), w_ref[half:, :], preferred_element_type=F32)
    o_ref[...] = x_ref[...] + _row_select(t, tm, n_ctx, mc_ref, ml_ref, 2) * mix


def _post(x, mod, a_f, a_r, b_f, b_r, p, gate_a_col, gate_b_col, w_out, *, even, tm, n_ctx):
    bsz, l, d = x.shape
    ctx_row = bsz
    tok = lambda b, t: (b, t, 0)
    grp = pl.BlockSpec((None, tm, GROUP), tok)
    return pl.pallas_call(
        functools.partial(_post_body, even=even, tm=tm, n_ctx=n_ctx),
        grid=(bsz, l // tm),
        in_specs=[pl.BlockSpec((None, tm, d), tok), grp, grp,
                  pl.BlockSpec((None, tm, GROUP), lambda b, t: (b, t, gate_a_col)), grp, grp,
                  pl.BlockSpec((None, tm, GROUP), lambda b, t: (b, t, gate_b_col)),
                  pl.BlockSpec(w_out.shape, lambda b, t: (0, 0)),
                  pl.BlockSpec((None, 6, d), lambda b, t: (b, 0, 0)),
                  pl.BlockSpec((None, 6, d), lambda b, t: (ctx_row, 0, 0))],
        out_specs=pl.BlockSpec((None, tm, d), tok),
        out_shape=jax.ShapeDtypeStruct((bsz, l, d), F32),
        compiler_params=_cparams("parallel", "parallel"),
        name="post",
    )(x, a_f, a_r, p, b_f, b_r, p, w_out, mod, mod)


def _mlp_body(x_ref, g_ref, ml_ref, mc_ref, w1_ref, w2_ref, fg_ref, o_ref, h_scr, acc_scr, *, tm, n_ctx, final):
    t = pl.program_id(1)
    f = pl.program_id(2)
    if final:
        x_ref = x_ref.at[0]

    @pl.when(f == 0)
    def _():
        y = _rms(x_ref[...], g_ref[...])
        shift = _row_select(t, tm, n_ctx, mc_ref, ml_ref, 3)
        scale = _row_select(t, tm, n_ctx, mc_ref, ml_ref, 4)
        h_scr[...] = (y * (1.0 + scale) + shift).astype(BF16)
        acc_scr[...] = jnp.zeros_like(acc_scr)

    hid = jnp.maximum(jnp.dot(h_scr[...], w1_ref[...], preferred_element_type=F32), 0.0)
    acc_scr[...] += jnp.dot((hid * hid).astype(BF16), w2_ref[...], preferred_element_type=F32)

    @pl.when(f == pl.num_programs(2) - 1)
    def _():
        y = x_ref[...] + _row_select(t, tm, n_ctx, mc_ref, ml_ref, 5) * acc_scr[...]
        if final:
            y = _rms(y, fg_ref[...])
        o_ref[...] = y


def _mlp(x, mod, g, w1, w2, final_g, *, tm, tf, n_ctx, final):
    bsz, l, d = x.shape
    dff = w1.shape[1]
    ctx_row = bsz
    tok = lambda b, t, f: (b, t, 0)
    if final:
        l = l - n_ctx
        x_spec = pl.BlockSpec((pl.Element(1), pl.Element(tm), pl.Element(d)),
                              lambda b, t, f, off=n_ctx: (b, pl.multiple_of(off + t * tm, SUBLANES), 0))
        n_ctx = 0
    else:
        x_spec = pl.BlockSpec((None, tm, d), tok)
    return pl.pallas_call(
        functools.partial(_mlp_body, tm=tm, n_ctx=n_ctx, final=final),
        grid=(bsz, l // tm, dff // tf),
        in_specs=[x_spec,
                  pl.BlockSpec((1, d), lambda b, t, f: (0, 0)),
                  pl.BlockSpec((None, 6, d), lambda b, t, f: (b, 0, 0)),
                  pl.BlockSpec((None, 6, d), lambda b, t, f: (ctx_row, 0, 0)),
                  pl.BlockSpec((d, tf), lambda b, t, f: (0, f)),
                  pl.BlockSpec((tf, d), lambda b, t, f: (f, 0)),
                  pl.BlockSpec((1, d), lambda b, t, f: (0, 0))],
        out_specs=pl.BlockSpec((None, tm, d), tok),
        out_shape=jax.ShapeDtypeStruct((bsz, l, d), F32),
        scratch_shapes=[pltpu.VMEM((tm, d), BF16), pltpu.VMEM((tm, d), F32)],
        compiler_params=_cparams("parallel", "parallel", "arbitrary"),
        name="mlp",
    )(x, g.reshape(1, d), mod, mod, w1, w2, final_g.reshape(1, d))


def _token_tile(l):
    for div in (4, 8, 16, 17, 34):
        if l % div == 0 and (l // div) % 16 == 0 and l // div <= 1088:
            return l // div
    return l


def kernel(x, c, ctx, c_ctx, ada_w, ada_b, norm1_g, norm2_g, mix_w_out, mlp_w1, mlp_w2, ev_w_in, lru_conv_w,
           lru_conv_b, lru_wa, lru_ba, lru_wx, lru_bx, lru_lambda, ret_log_gamma, od_w_in, hg_lb_logits,
           gdn_conv_w, gdn_a_log, gdn_dt_bias, final_g):
    bsz, seq, d = x.shape
    n_ctx = ctx.shape[1]
    depth = ada_w.shape[0]
    l = n_ctx + seq
    tm = _token_tile(l)
    tf = 1024
    nb = next(n for n in (8, 4, 2, 1) if bsz % n == 0)

    mods = _mods(c, c_ctx, ada_w, ada_b)
    ret_dk = (ev_w_in.shape[2] - 4 * GROUP) // (2 * RET_HEADS)
    cos_t, sin_t = _rope_tables(n_ctx, seq, ret_dk)

    xs = jnp.concatenate([ctx, x], axis=1)
    for layer in range(depth):
        mod = mods[layer]
        w_out = mix_w_out[layer].astype(BF16)
        if layer % 2 == 0:
            e = layer // 2
            (p,) = _proj(xs, mod, norm1_g[layer], ev_w_in[e].astype(BF16), None, tm=tm, n_ctx=n_ctx)
            mix = []
            for dr, rev in enumerate((False, True)):
                mix.append((
                    _lru(p, lru_conv_w[e], lru_conv_b[e], _blockdiag_halves(lru_wa[e, dr]).astype(BF16),
                         lru_ba[e, dr], _blockdiag_halves(lru_wx[e, dr]).astype(BF16), lru_bx[e, dr],
                         lru_lambda[e, dr], n_ctx=n_ctx, reverse=rev),
                    _retention(p, cos_t, sin_t, ret_log_gamma[e, dr], n_ctx=n_ctx, reverse=rev, nb=nb)))
            gate_cols = (1, 4)
        else:
            o = layer // 2
            n_main = 9 * GROUP
            w_in = od_w_in[o]
            w_tail = jnp.pad(w_in[:, n_main:], ((0, 0), (0, LANES - (w_in.shape[1] - n_main)))).astype(BF16)
            p, tail = _proj(xs, mod, norm1_g[layer], w_in[:, :n_main].astype(BF16), w_tail, tm=tm, n_ctx=n_ctx)
            qkv = _gdn_prep(p, gdn_conv_w[o], n_ctx=n_ctx)
            mix = []
            for dr, rev in enumerate((False, True)):
                g_col = (2 + dr) * LIN_HEADS
                a_row = jnp.zeros((1, LANES), F32).at[0, g_col:g_col + LIN_HEADS].set(gdn_a_log[o, dr])
                dt_row = jnp.zeros((1, LANES), F32).at[0, g_col:g_col + LIN_HEADS].set(gdn_dt_bias[o, dr])
                mix.append((
                    _gla(p, hg_lb_logits[dr], n_ctx=n_ctx, reverse=rev, layer=o, nb=nb),
                    _gdn(qkv, tail, a_row, dt_row, n_ctx=n_ctx, reverse=rev, direction=dr, nb=nb)))
            gate_cols = (4, 8)
        xs = _post(xs, mod, mix[0][0], mix[1][0], mix[0][1], mix[1][1], p, gate_cols[0], gate_cols[1], w_out,
                   even=layer % 2 == 0, tm=tm, n_ctx=n_ctx)
        last = layer == depth - 1
        xs = _mlp(xs, mod, norm2_g[layer], mlp_w1[layer].astype(BF16), mlp_w2[layer].astype(BF16), final_g,
                  tm=_token_tile(seq) if last else tm, tf=tf, n_ctx=n_ctx, final=last)
    return xs
```

```python
import functools

import numpy as np
import jax
import jax.numpy as jnp
from jax import lax
from jax.experimental import pallas as pl
from jax.experimental.pallas import tpu as pltpu

F32 = jnp.float32
BF16 = jnp.bfloat16

EPS = 1e-6
GRID_W = 64
CONV_W = 4
CONV_LEFT = CONV_W // 2
LRU_BLOCKS = 8
LRU_C = 8.0
RET_HEADS = 4
RET_CHUNK = 256
ROPE_BASE = 10000.0
LIN_HEADS = 4
GLA_CHUNK = 128
LIN_CHUNK = 64
HEAD_DV = 128

LANES = 128
SUBLANES = 8
V7X_VMEM_BYTES = 64 * 1024 * 1024
VMEM_LIMIT = (V7X_VMEM_BYTES * 3) // 4
GROUP = 512

HALO = 2 * SUBLANES
ACT = jnp.bfloat16


def _cparams(*sem):
    return pltpu.CompilerParams(dimension_semantics=sem, vmem_limit_bytes=VMEM_LIMIT)


def _sigmoid(x):
    return 1.0 / (1.0 + jnp.exp(-x))


def _silu(x):
    return x * _sigmoid(x)


def _softplus(x):
    return jnp.maximum(x, 0.0) + jnp.log(1.0 + jnp.exp(-jnp.abs(x)))


def _gelu_tanh(x):
    return 0.5 * x * (1.0 + jnp.tanh(0.7978845608028654 * (x + 0.044715 * x * x * x)))


def _dot(a, b):
    return jnp.dot(a.astype(BF16), b.astype(BF16), preferred_element_type=F32)


def _dot_nt(a, b):
    return lax.dot_general(a.astype(BF16), b.astype(BF16), (((1,), (1,)), ((), ())),
                           preferred_element_type=F32)


def _split_bf16(x):
    hi = x.astype(BF16)
    lo = (x - hi.astype(F32)).astype(BF16)
    return hi, lo


def _dot_sel(m2, x):
    hi, lo = _split_bf16(x)
    return jnp.dot(m2, jnp.concatenate([hi, lo], axis=0), preferred_element_type=F32)


def _doubled(m):
    return jnp.asarray(np.concatenate([m, m], axis=1), BF16)


def _dot3(a, b):
    ah, al = _split_bf16(a)
    bh, bl = _split_bf16(b)
    return (jnp.dot(ah, bh, preferred_element_type=F32) + jnp.dot(ah, bl, preferred_element_type=F32)
            + jnp.dot(al, bh, preferred_element_type=F32))


def _row_select(t, tm, n_ctx, mc_ref, ml_ref, idx):
    rows = t * tm + lax.broadcasted_iota(jnp.int32, (tm, 1), 0)
    return jnp.where(rows < n_ctx, mc_ref[idx:idx + 1, :], ml_ref[idx:idx + 1, :])


def _rms(x, g):
    return x * lax.rsqrt(jnp.mean(x * x, axis=-1, keepdims=True) + EPS) * g


def _head_norm(o, center):
    outs = []
    for h in range(o.shape[-1] // HEAD_DV):
        oh = o[:, h * HEAD_DV:(h + 1) * HEAD_DV]
        if center:
            oh = oh - jnp.mean(oh, axis=-1, keepdims=True)
        outs.append(oh * lax.rsqrt(jnp.mean(oh * oh, axis=-1, keepdims=True) + EPS))
    return jnp.concatenate(outs, axis=-1)


def _scan_tile(step, n_first, n_total, reverse):
    if not reverse:
        return step
    return jnp.where(step < n_first, n_first - 1 - step, n_total - 1 - (step - n_first))


def _mods_body(s_ref, w_ref, b_ref, o_ref):
    s = _silu(s_ref[...])
    o_ref[...] = _dot3(s, w_ref[...]) + b_ref[...]


def _mods(c, c_ctx, ada_w, ada_b):
    depth, d, n6 = ada_w.shape
    bsz = c.shape[0]
    rows = -(-(bsz + 1) // SUBLANES) * SUBLANES
    s = jnp.zeros((rows, d), F32).at[:bsz].set(c).at[bsz].set(c_ctx)
    tn = 1024
    out = pl.pallas_call(
        _mods_body,
        grid=(depth, n6 // tn),
        in_specs=[pl.BlockSpec((rows, d), lambda l, n: (0, 0)),
                  pl.BlockSpec((None, d, tn), lambda l, n: (l, 0, n)),
                  pl.BlockSpec((None, 1, tn), lambda l, n: (l, 0, n))],
        out_specs=pl.BlockSpec((None, rows, tn), lambda l, n: (l, 0, n)),
        out_shape=jax.ShapeDtypeStruct((depth, rows, n6), F32),
        compiler_params=_cparams("parallel", "parallel"),
        name="mods",
    )(s, ada_w, ada_b.reshape(depth, 1, n6))
    return out.reshape(depth, rows, 6, d)


def _proj_body(*refs, tm, n_ctx, has_tail):
    if has_tail:
        x_ref, ml_ref, mc_ref, g_ref, w_ref, wt_ref, o_ref, ot_ref = refs
    else:
        x_ref, ml_ref, mc_ref, g_ref, w_ref, o_ref = refs
    t = pl.program_id(1)
    y = _rms(x_ref[...], g_ref[...])
    shift = _row_select(t, tm, n_ctx, mc_ref, ml_ref, 0)
    scale = _row_select(t, tm, n_ctx, mc_ref, ml_ref, 1)
    h = (y * (1.0 + scale) + shift).astype(BF16)
    o_ref[...] = jnp.dot(h, w_ref[...], preferred_element_type=F32).astype(o_ref.dtype)
    if has_tail:
        ot_ref[...] = jnp.dot(h, wt_ref[...], preferred_element_type=F32)


def _proj(x, mod, g, w, w_tail, *, tm, n_ctx):
    bsz, l, d = x.shape
    n_out = w.shape[1]
    ctx_row = bsz
    resident = dict(pipeline_mode=pl.Buffered(1))
    in_specs = [pl.BlockSpec((None, tm, d), lambda b, t: (b, t, 0)),
                pl.BlockSpec((None, 6, d), lambda b, t: (b, 0, 0)),
                pl.BlockSpec((None, 6, d), lambda b, t: (ctx_row, 0, 0)),
                pl.BlockSpec((1, d), lambda b, t: (0, 0)),
                pl.BlockSpec((d, n_out), lambda b, t: (0, 0), **resident)]
    out_specs = [pl.BlockSpec((None, tm, n_out), lambda b, t: (b, t, 0))]
    out_shape = [jax.ShapeDtypeStruct((bsz, l, n_out), ACT)]
    args = [x, mod, mod, g.reshape(1, d), w]
    if w_tail is not None:
        in_specs.append(pl.BlockSpec((d, LANES), lambda b, t: (0, 0), **resident))
        out_specs.append(pl.BlockSpec((None, tm, LANES), lambda b, t: (b, t, 0)))
        out_shape.append(jax.ShapeDtypeStruct((bsz, l, LANES), F32))
        args.append(w_tail)
    outs = pl.pallas_call(
        functools.partial(_proj_body, tm=tm, n_ctx=n_ctx, has_tail=w_tail is not None),
        grid=(bsz, l // tm),
        in_specs=in_specs, out_specs=out_specs, out_shape=out_shape,
        compiler_params=_cparams("parallel", "parallel"),
        name="proj",
    )(*args)
    return outs


def _lru_body(sm_ref, se_ref, x_ref, xp_ref, xn_ref, cw_ref, cb_ref, wa_ref, ba_ref, wx_ref, bx_ref, lam_ref,
              o_ref, a_scr, b_scr, h_scr, *, tl, n_ctx_tiles, n_tiles, reverse):
    step = pl.program_id(0)
    tile = _scan_tile(step, n_ctx_tiles, n_tiles, reverse)
    bsz = x_ref.shape[0]
    width = x_ref.shape[2]
    half = width // 2
    seg_first = jnp.logical_or(tile == 0, tile == n_ctx_tiles)
    seg_last = jnp.logical_or(tile == n_ctx_tiles - 1, tile == n_tiles - 1)
    keep_prev = jnp.where(seg_first, 0.0, 1.0)
    keep_next = jnp.where(seg_last, 0.0, 1.0)
    neg_c_sp = -LRU_C * _softplus(-lam_ref[...])

    @pl.when(step == 0)
    def _():
        h_scr[...] = jnp.zeros_like(h_scr)

    pair = 2 if bsz % 2 == 0 else 1

    def block_dot(ub, w_ref):
        return jnp.concatenate([jnp.dot(ub[:, :half], w_ref[0], preferred_element_type=F32),
                                jnp.dot(ub[:, half:], w_ref[1], preferred_element_type=F32)], axis=-1)

    def gates(bp, carry):
        bs = [bp * pair + n for n in range(pair)]
        taps = [_conv_taps(sm_ref, se_ref, x_ref[b], xp_ref[b], xn_ref[b], keep_prev, keep_next, tl) for b in bs]
        u = [cb_ref[...] + sum(cw_ref[k:k + 1, :] * tp[k] for k in range(CONV_W)) for tp in taps]
        ub = [x.astype(BF16) for x in u]
        r_pre = [block_dot(x, wa_ref) + ba_ref[...] for x in ub]
        i_pre = [block_dot(x, wx_ref) + bx_ref[...] for x in ub]
        a = [jnp.exp(neg_c_sp * _sigmoid(x)) for x in r_pre]
        for n, b in enumerate(bs):
            a_scr[b] = a[n]
            b_scr[b] = jnp.sqrt(1.0 - a[n] * a[n]) * (_sigmoid(i_pre[n]) * u[n])
        return carry

    lax.fori_loop(0, bsz // pair, gates, 0)

    def scan_rows(j, hs):
        idx = (tl - 1 - j) if reverse else j
        new = []
        for b in range(bsz):
            h = a_scr[b, pl.ds(idx, 1), :] * hs[b] + b_scr[b, pl.ds(idx, 1), :]
            b_scr[b, pl.ds(idx, 1), :] = h
            new.append(h)
        return tuple(new)

    hs = tuple(h_scr[b, 0:1, :] for b in range(bsz))
    hs = lax.fori_loop(0, tl, scan_rows, hs)
    for b in range(bsz):
        h_scr[b, 0:1, :] = hs[b]
    o_ref[...] = b_scr[...].astype(o_ref.dtype)


def _lru(p, conv_w, conv_b, wa, ba, wx, bx, lam, *, n_ctx, reverse, tl=256):
    bsz, l, _ = p.shape
    width = GROUP
    n_tiles = l // tl
    n_ctx_tiles = n_ctx // tl
    hb = tl // HALO
    n_hblocks = l // HALO

    def tile_of(s):
        return _scan_tile(s, n_ctx_tiles, n_tiles, reverse)

    vec = pl.BlockSpec((1, width), lambda s: (0, 0))
    mat = pl.BlockSpec((2, width // 2, width // 2), lambda s: (0, 0, 0))
    sm, se = (jnp.asarray(m, p.dtype) for m in _shift_matrices(tl))
    return pl.pallas_call(
        functools.partial(_lru_body, tl=tl, n_ctx_tiles=n_ctx_tiles, n_tiles=n_tiles, reverse=reverse),
        grid=(n_tiles,),
        in_specs=[pl.BlockSpec(sm.shape, lambda s: (0, 0)), pl.BlockSpec(se.shape, lambda s: (0, 0)),
                  pl.BlockSpec((bsz, tl, width), lambda s: (0, tile_of(s), 0)),
                  pl.BlockSpec((bsz, HALO, width), lambda s: (0, jnp.maximum(tile_of(s) * hb - 1, 0), 0)),
                  pl.BlockSpec((bsz, HALO, width),
                               lambda s: (0, jnp.minimum((tile_of(s) + 1) * hb, n_hblocks - 1), 0)),
                  pl.BlockSpec((CONV_W, width), lambda s: (0, 0)), vec, mat, vec, mat, vec, vec],
        out_specs=pl.BlockSpec((bsz, tl, width), lambda s: (0, tile_of(s), 0)),
        out_shape=jax.ShapeDtypeStruct((bsz, l, width), ACT),
        scratch_shapes=[pltpu.VMEM((bsz, tl, width), F32),
                        pltpu.VMEM((bsz, tl, width), F32),
                        pltpu.VMEM((bsz, SUBLANES, width), F32)],
        compiler_params=_cparams("arbitrary"),
        name="lru_rev" if reverse else "lru_fwd",
    )(sm, se, p, p, p, conv_w, conv_b.reshape(1, width), wa, ba.reshape(1, width), wx, bx.reshape(1, width),
      lam.reshape(1, width))


def _blockdiag_halves(w):
    nb, n, _ = w.shape
    per = nb // 2
    w = w.reshape(2, per, n, n)
    eye = jnp.eye(per, dtype=w.dtype)
    out = w[:, :, :, None, :] * eye[None, :, None, :, None]
    return out.reshape(2, per * n, per * n)


def _ret_body(lg_ref, q_ref, k_ref, v_ref, cos_ref, sin_ref, o_ref, dec_scr, qd_scr, kd_scr, cd_scr, st_scr,
              *, reverse, dk, nb):
    c = RET_CHUNK
    step = pl.program_id(1)

    @pl.when(step == 0)
    def _():
        st_scr[...] = jnp.zeros_like(st_scr)
        i = lax.broadcasted_iota(jnp.int32, (c, c), 0)
        j = lax.broadcasted_iota(jnp.int32, (c, c), 1)
        diff = (j - i) if reverse else (i - j)
        row = lax.broadcasted_iota(jnp.int32, (c, HEAD_DV), 0)
        pos = ((c - 1 - row) if reverse else row).astype(F32)
        dpos = jnp.maximum(diff, 0).astype(F32)
        for h in range(RET_HEADS):
            lg = lg_ref[h]
            dec_scr[h] = jnp.where(diff >= 0, jnp.exp(lg * dpos), 0.0)
            qd_scr[h] = jnp.exp(lg * (pos + 1.0))
            kd_scr[h] = jnp.exp(lg * (c - 1.0 - pos))
            cd_scr[h] = jnp.exp(jnp.full((SUBLANES, LANES), c, F32) * lg)

    def rope(x):
        width = x.shape[-1]
        lane = lax.broadcasted_iota(jnp.int32, x.shape, 1)
        first = (lane & (dk - 1)) < dk // 2
        rot = jnp.where(first, pltpu.roll(x, width - dk // 2, axis=1), pltpu.roll(x, dk // 2, axis=1))
        return x * cos_ref[...] + rot * sin_ref[...]

    q = [rope(q_ref[bb].astype(F32)) for bb in range(nb)]
    k = [rope(k_ref[bb].astype(F32)) * (dk ** -0.5) for bb in range(nb)]
    pairs = [(bb, h) for bb in range(nb) for h in range(RET_HEADS)]
    qh = [q[bb][:, h * dk:(h + 1) * dk] for bb, h in pairs]
    kh = [k[bb][:, h * dk:(h + 1) * dk] for bb, h in pairs]
    vh = [v_ref[bb, :, h * HEAD_DV:(h + 1) * HEAD_DV] for bb, h in pairs]
    s = [_dot_nt(qh[i], kh[i]) * dec_scr[h] for i, (bb, h) in enumerate(pairs)]
    state = [st_scr[bb, h] for bb, h in pairs]
    o_x = [_dot(qh[i], state[i]) * qd_scr[h] for i, (bb, h) in enumerate(pairs)]
    o_in = [_dot(s[i], vh[i]) for i in range(len(pairs))]
    kv = [_dot((kh[i] * kd_scr[h][:, :dk]).T, vh[i]) for i, (bb, h) in enumerate(pairs)]
    for i, (bb, h) in enumerate(pairs):
        st_scr[bb, h] = state[i] * cd_scr[h][0:1, :] + kv[i]
        o_ref[bb, :, h * HEAD_DV:(h + 1) * HEAD_DV] = (o_in[i] + o_x[i]).astype(o_ref.dtype)


def _retention(p, cos_t, sin_t, log_gamma, *, n_ctx, reverse, nb):
    bsz, l, _ = p.shape
    c = RET_CHUNK
    qk_w = cos_t.shape[1]
    dk = qk_w // RET_HEADS
    n_chunks = l // c
    n_ctx_chunks = n_ctx // c

    def ch(s):
        return _scan_tile(s, n_ctx_chunks, n_chunks, reverse)

    q_col = (2 * GROUP) // qk_w
    return pl.pallas_call(
        functools.partial(_ret_body, reverse=reverse, dk=dk, nb=nb),
        grid=(bsz // nb, n_chunks),
        in_specs=[pl.BlockSpec(memory_space=pltpu.SMEM),
                  pl.BlockSpec((nb, c, qk_w), lambda b, s: (b, ch(s), q_col)),
                  pl.BlockSpec((nb, c, qk_w), lambda b, s: (b, ch(s), q_col + 1)),
                  pl.BlockSpec((nb, c, GROUP), lambda b, s: (b, ch(s), 3)),
                  pl.BlockSpec((c, qk_w), lambda b, s: (ch(s), 0)),
                  pl.BlockSpec((c, qk_w), lambda b, s: (ch(s), 0))],
        out_specs=pl.BlockSpec((nb, c, GROUP), lambda b, s: (b, ch(s), 0)),
        out_shape=jax.ShapeDtypeStruct((bsz, l, GROUP), ACT),
        scratch_shapes=[pltpu.VMEM((RET_HEADS, c, c), F32),
                        pltpu.VMEM((RET_HEADS, c, HEAD_DV), F32),
                        pltpu.VMEM((RET_HEADS, c, HEAD_DV), F32),
                        pltpu.VMEM((RET_HEADS, SUBLANES, LANES), F32),
                        pltpu.VMEM((nb, RET_HEADS, dk, HEAD_DV), F32)],
        compiler_params=_cparams("parallel", "arbitrary"),
        name="ret_rev" if reverse else "ret_fwd",
    )(log_gamma, p, p, p, cos_t, sin_t)


def _rope_tables(n_ctx, seq, dk):
    n_freq = dk // 4
    inv = jnp.power(ROPE_BASE, -jnp.arange(n_freq, dtype=F32) / n_freq)
    rows = seq // GRID_W
    r = jnp.arange(rows, dtype=F32)
    col = jnp.arange(GRID_W, dtype=F32)
    row_ang = jnp.broadcast_to(r[:, None, None] * inv, (rows, GRID_W, n_freq))
    col_ang = jnp.broadcast_to(col[None, :, None] * inv, (rows, GRID_W, n_freq))
    ang = jnp.concatenate([row_ang, col_ang], axis=-1).reshape(rows * GRID_W, 2 * n_freq)
    cos, sin = jnp.cos(ang), jnp.sin(ang)
    cos_h = jnp.concatenate([cos, cos], axis=-1)
    sin_h = jnp.concatenate([-sin, sin], axis=-1)
    cos_t = jnp.concatenate([jnp.ones((n_ctx, dk), F32), cos_h], axis=0)
    sin_t = jnp.concatenate([jnp.zeros((n_ctx, dk), F32), sin_h], axis=0)
    return jnp.tile(cos_t, (1, RET_HEADS)), jnp.tile(sin_t, (1, RET_HEADS))


_GLA_LEVELS = tuple(GLA_CHUNK >> (i + 1) for i in range(GLA_CHUNK.bit_length() - 1))


def _gla_matrices(reverse):
    c = GLA_CHUNK
    p = np.arange(c)
    mats = [(p[None, :] <= p[:, None]), (p[None, :] > p[:, None])]
    for s in _GLA_LEVELS:
        bs = (p // s) * s
        right = ((p // s) % 2 == 1)
        m_r = (p[None, :] > bs[:, None]) & (p[None, :] <= p[:, None])
        m_l = (p[None, :] > p[:, None]) & (p[None, :] <= (bs + s)[:, None])
        mats.append(np.where(right[:, None], m_r, m_l))
    m = np.stack(mats).astype(np.float32)
    if reverse:
        m = m[:, ::-1, ::-1]
    return m.reshape(len(mats) * c, c)


def _gla_body(m_ref, lb_ref, q_ref, f_ref, v_ref, o_ref, st_scr, *, reverse, layer, nb):
    c = GLA_CHUNK
    step = pl.program_id(1)

    @pl.when(step == 0)
    def _():
        st_scr[...] = jnp.zeros_like(st_scr)

    lg = lb_ref[...]
    e = jnp.exp(lg - jnp.max(lg, axis=0, keepdims=True))
    sm = e / jnp.sum(e, axis=0, keepdims=True)
    lb = jnp.sum(sm[0:layer + 1, :], axis=0, keepdims=True) - sm[0:1, :]

    ri = lax.broadcasted_iota(jnp.int32, (c, 1), 0)
    i2 = lax.broadcasted_iota(jnp.int32, (c, c), 0)
    j2 = lax.broadcasted_iota(jnp.int32, (c, c), 1)
    if reverse:
        ri, i2, j2 = c - 1 - ri, c - 1 - i2, c - 1 - j2
    last = 0 if reverse else c - 1
    width = q_ref.shape[-1]
    nh = width // HEAD_DV

    rows_bb = list(range(nb))
    pairs = [(bb, h) for bb in rows_bb for h in range(nh)]

    def head(x, h):
        return x[:, h * HEAD_DV:(h + 1) * HEAD_DV]

    f = [lb + (1.0 - lb) * _sigmoid(f_ref[bb].astype(F32)) for bb in rows_bb]
    k = [1.0 - x for x in f]
    q = [_silu(q_ref[bb].astype(F32)) for bb in rows_bb]
    ex = [_dot_sel(m_ref[...], jnp.log(x)) for x in f]
    a_last = [jnp.exp(x[last:last + 1, :]) for x in ex]
    qa = [q[bb] * jnp.exp(ex[bb][0:c, :]) for bb in rows_bb]
    kl = [k[bb] * jnp.exp(ex[bb][c:2 * c, :]) for bb in rows_bb]

    qb = [x.astype(BF16) for x in q]
    kb = [x.astype(BF16) for x in k]
    scores = [jnp.where(i2 == j2, _dot_nt(head(qb[bb], h), head(kb[bb], h)), 0.0) for bb, h in pairs]
    for lvl, s in enumerate(_GLA_LEVELS):
        sh = s.bit_length() - 1
        right = jnp.broadcast_to(((ri >> sh) & 1).astype(F32), (c, width)).astype(BF16) > 0
        x = [jnp.where(right, qb[bb], kb[bb]) * jnp.exp(ex[bb][(2 + lvl) * c:(3 + lvl) * c, :].astype(BF16))
             for bb in rows_bb]
        bi, bj = i2 >> sh, j2 >> sh
        valid = bj == jnp.where((bi & 1) == 1, bi - 1, -1)
        gram = [_dot_nt(head(x[bb], h), head(x[bb], h)) for bb, h in pairs]
        scores = [jnp.where(valid, gm, sc) for sc, gm in zip(scores, gram)]

    state_t = [st_scr[bb, h] for bb, h in pairs]
    o_intra = [_dot(scores[i], head(v_ref[bb], h)) for i, (bb, h) in enumerate(pairs)]
    o_inter = [_dot_nt(head(qa[bb], h), state_t[i]) for i, (bb, h) in enumerate(pairs)]
    kv = [_dot(head(v_ref[bb], h).astype(F32).T, head(kl[bb], h)) for bb, h in pairs]
    for i, (bb, h) in enumerate(pairs):
        o_ref[bb, :, h * HEAD_DV:(h + 1) * HEAD_DV] = (o_intra[i] + o_inter[i]).astype(o_ref.dtype)
        st_scr[bb, h] = state_t[i] * head(a_last[bb], h) + kv[i]


def _gla(p, lb_logits, *, n_ctx, reverse, layer, nb):
    bsz, l, _ = p.shape
    c = GLA_CHUNK
    n_chunks = l // c
    n_ctx_chunks = n_ctx // c
    m = _doubled(_gla_matrices(reverse))

    def ch(s):
        return _scan_tile(s, n_ctx_chunks, n_chunks, reverse)

    f_col = 2 if reverse else 1
    return pl.pallas_call(
        functools.partial(_gla_body, reverse=reverse, layer=layer, nb=nb),
        grid=(bsz // nb, n_chunks),
        in_specs=[pl.BlockSpec(m.shape, lambda b, s: (0, 0)),
                  pl.BlockSpec(lb_logits.shape, lambda b, s: (0, 0)),
                  pl.BlockSpec((nb, c, GROUP), lambda b, s: (b, ch(s), 0)),
                  pl.BlockSpec((nb, c, GROUP), lambda b, s: (b, ch(s), f_col)),
                  pl.BlockSpec((nb, c, GROUP), lambda b, s: (b, ch(s), 3))],
        out_specs=pl.BlockSpec((nb, c, GROUP), lambda b, s: (b, ch(s), 0)),
        out_shape=jax.ShapeDtypeStruct((bsz, l, GROUP), ACT),
        scratch_shapes=[pltpu.VMEM((nb, LIN_HEADS, HEAD_DV, HEAD_DV), F32)],
        compiler_params=_cparams("parallel", "arbitrary"),
        name="gla_rev" if reverse else "gla_fwd",
    )(m, lb_logits, p, p, p)


def _shift_matrices(rows):
    offs = [t - CONV_LEFT for t in range(CONV_W) if t != CONV_LEFT]
    main = np.zeros((len(offs) * rows, rows), np.float32)
    edge = np.zeros((len(offs) * SUBLANES, 2 * HALO), np.float32)
    for n, off in enumerate(offs):
        for i in range(rows):
            j = i + off
            if 0 <= j < rows:
                main[n * rows + i, j] = 1.0
            elif j < 0:
                edge[n * SUBLANES + i, HALO + j] = 1.0
            else:
                edge[n * SUBLANES + i - (rows - SUBLANES), HALO + j - rows] = 1.0
    return main, edge


def _conv_taps(sm_ref, se_ref, x, xp, xn, keep_prev, keep_next, rows):
    sh = jnp.dot(sm_ref[...], x, preferred_element_type=F32)
    halo = jnp.concatenate([xp * keep_prev.astype(xp.dtype), xn * keep_next.astype(xn.dtype)], axis=0)
    fix = jnp.dot(se_ref[...], halo, preferred_element_type=F32)
    taps, n = [], 0
    for t in range(CONV_W):
        off = t - CONV_LEFT
        if off == 0:
            taps.append(x.astype(F32))
            continue
        s = sh[n * rows:(n + 1) * rows, :]
        f = fix[n * SUBLANES:(n + 1) * SUBLANES, :]
        if off < 0:
            s = jnp.concatenate([s[0:SUBLANES] + f, s[SUBLANES:]], axis=0)
        else:
            s = jnp.concatenate([s[:rows - SUBLANES], s[rows - SUBLANES:] + f], axis=0)
        taps.append(s)
        n += 1
    return taps


def _gdn_prep_body(sm_ref, se_ref, cw_ref, q_ref, qp_ref, qn_ref, k_ref, kp_ref, kn_ref, v_ref, vp_ref, vn_ref,
                   qo_ref, ko_ref, vo_ref, *, tp, n_ctx_tiles, n_tiles):
    c = LIN_CHUNK
    nh = LIN_HEADS
    tile = pl.program_id(1)
    seg_first = jnp.logical_or(tile == 0, tile == n_ctx_tiles)
    seg_last = jnp.logical_or(tile == n_ctx_tiles - 1, tile == n_tiles - 1)
    keep_prev = jnp.where(seg_first, 0.0, 1.0)
    keep_next = jnp.where(seg_last, 0.0, 1.0)

    all_taps = [_conv_taps(sm_ref, se_ref, x[...], xp[...], xn[...], keep_prev, keep_next, tp)
                for x, xp, xn in ((q_ref, qp_ref, qn_ref), (k_ref, kp_ref, kn_ref), (v_ref, vp_ref, vn_ref))]

    def conv_silu(col):
        taps = all_taps[col]
        u = cw_ref[0:1, col * GROUP:(col + 1) * GROUP] * taps[0]
        for t in range(1, CONV_W):
            u = u + cw_ref[t:t + 1, col * GROUP:(col + 1) * GROUP] * taps[t]
        return _silu(u)

    def l2n(x):
        return x * lax.rsqrt(jnp.sum(x * x, axis=-1, keepdims=True) + EPS)

    def emit(out_ref, x, norm, scale):
        for h in range(nh):
            xh = x[:, h * HEAD_DV:(h + 1) * HEAD_DV]
            if norm:
                xh = l2n(xh) * scale
            for j in range(tp // c):
                out_ref[j, h * c:(h + 1) * c, :] = xh[j * c:(j + 1) * c, :].astype(out_ref.dtype)

    emit(qo_ref, conv_silu(0), True, HEAD_DV ** -0.5)
    emit(ko_ref, conv_silu(1), True, 1.0)
    emit(vo_ref, conv_silu(2), False, 1.0)


def _gdn_prep(p, conv_w, *, n_ctx, tp=256):
    bsz, l, _ = p.shape
    c = LIN_CHUNK
    n_tiles = l // tp
    hb = tp // HALO
    n_hblocks = l // HALO

    def triple(col):
        return [pl.BlockSpec((None, tp, GROUP), lambda b, t: (b, t, col)),
                pl.BlockSpec((None, HALO, GROUP), lambda b, t: (b, jnp.maximum(t * hb - 1, 0), col)),
                pl.BlockSpec((None, HALO, GROUP), lambda b, t: (b, jnp.minimum((t + 1) * hb, n_hblocks - 1), col))]

    out_spec = pl.BlockSpec((None, tp // c, LIN_HEADS * c, HEAD_DV), lambda b, t: (b, t, 0, 0))
    out_sds = jax.ShapeDtypeStruct((bsz, l // c, LIN_HEADS * c, HEAD_DV), ACT)
    sm, se = (jnp.asarray(m, p.dtype) for m in _shift_matrices(tp))
    const2 = lambda b, t: (0, 0)
    return pl.pallas_call(
        functools.partial(_gdn_prep_body, tp=tp, n_ctx_tiles=n_ctx // tp, n_tiles=n_tiles),
        grid=(bsz, n_tiles),
        in_specs=[pl.BlockSpec(sm.shape, const2), pl.BlockSpec(se.shape, const2), pl.BlockSpec(conv_w.shape, const2)]
                 + triple(5) + triple(6) + triple(7),
        out_specs=[out_spec] * 3, out_shape=[out_sds] * 3,
        compiler_params=_cparams("parallel", "parallel"),
        name="gdn_prep",
    )(sm, se, conv_w, *([p] * 9))


def _gdn_body(cs_ref, arow_ref, dtrow_ref, t_ref, q_ref, k_ref, v_ref, o_ref, st_scr,
              *, reverse, beta_col, g_col, nb):
    c = LIN_CHUNK
    nh = LIN_HEADS
    n = nh * c
    step = pl.program_id(1)

    @pl.when(step == 0)
    def _():
        st_scr[...] = jnp.zeros_like(st_scr)

    i2 = lax.broadcasted_iota(jnp.int32, (c, n), 0)
    lane = lax.broadcasted_iota(jnp.int32, (c, n), 1)
    j2 = lane & (c - 1)
    grp = lane >> 6
    mask = (j2 >= i2) if reverse else (j2 <= i2)
    same32 = (i2 >> 5) == (j2 >> 5)
    same16 = (i2 >> 4) == (j2 >> 4)
    eye = jnp.where(i2 == j2, 1.0, 0.0)
    grp_rows = [jnp.where((lax.broadcasted_iota(jnp.int32, (1, n), 1) >> 6) == h, 1.0, 0.0).astype(BF16)
                for h in range(nh)]
    last = 0 if reverse else c - 1

    def each(fn, *lists):
        return [fn(*args) for args in zip(*lists)]

    def head_bcast(x, col):
        return jnp.concatenate([jnp.broadcast_to(x[:, col + h:col + h + 1], (c, HEAD_DV)) for h in range(nh)], axis=0)

    def side_by_side(x):
        out = x[(nh - 1) * c:nh * c, :]
        for h in range(nh - 2, -1, -1):
            out = jnp.where(grp == h, x[h * c:(h + 1) * c, :], out)
        return out

    def block_diag(x):
        xb = x.astype(BF16)
        return jnp.concatenate([xb * grp_rows[h] for h in range(nh)], axis=0)

    def dot_sbs(x, y_bd):
        return jnp.dot(x.astype(BF16), y_bd, preferred_element_type=F32)

    rows_bb = list(range(nb))
    q = [q_ref[bb].astype(F32) for bb in rows_bb]
    k = [k_ref[bb].astype(F32) for bb in rows_bb]
    v = [v_ref[bb].astype(F32) for bb in rows_bb]

    tail = [t_ref[bb] for bb in rows_bb]
    beta = [head_bcast(_sigmoid(x), beta_col) for x in tail]
    g = [head_bcast(-jnp.exp(arow_ref[...]) * _softplus(x + dtrow_ref[...]), g_col) for x in tail]
    gc = [_dot_sel(cs_ref[...], x) for x in g]
    gc_last = [jnp.concatenate([jnp.broadcast_to(x[h * c + last:h * c + last + 1, :], (c, HEAD_DV))
                                for h in range(nh)], axis=0) for x in gc]
    diff = [side_by_side(jnp.concatenate([x, x], axis=1)) - jnp.transpose(x)[0:1, :] for x in gc]
    gam = [jnp.where(mask, jnp.exp(jnp.where(mask, x, 0.0)), 0.0) for x in diff]

    kb = each(lambda a, b: a * b, k, beta)
    kq = each(lambda a, b, kx: _dot_nt(jnp.concatenate([a, b], axis=0), kx), kb, q, k)
    mm = each(lambda x, gm: jnp.where(i2 == j2, 0.0, side_by_side(x[:n]) * gm), kq, gam)
    qk = each(lambda x, gm: side_by_side(x[n:]) * gm, kq, gam)
    egc = [jnp.exp(x) for x in gc]
    rhs = each(lambda a, b, kbx, e: jnp.concatenate([a * b, kbx * e], axis=1), v, beta, kb, egc)

    md = [jnp.where(same16, x, 0.0) for x in mm]
    m2 = each(lambda a: dot_sbs(a, block_diag(a)), md)
    t = [eye - x for x in md]
    tm = each(lambda a, b: dot_sbs(jnp.concatenate([a, b], axis=0), block_diag(b)), t, m2)
    t = each(lambda a, x: a + x[:c], t, tm)
    m4 = [x[c:] for x in tm]
    tm = each(lambda a, b: dot_sbs(jnp.concatenate([a, b], axis=0), block_diag(b)), t, m4)
    t = each(lambda a, x: a + x[:c], t, tm)
    t = each(lambda a, x: a + dot_sbs(a, block_diag(x[c:])), t, tm)
    c32 = each(lambda a, b: jnp.where(same32, a - b, 0.0), mm, md)
    ct = each(lambda a, b: dot_sbs(a, block_diag(b)), c32, t)
    t = each(lambda a, b: a - dot_sbs(a, block_diag(b)), t, ct)
    c64 = [jnp.where(same32, 0.0, x) for x in mm]
    ct = each(lambda a, b: dot_sbs(a, block_diag(b)), c64, t)
    t = each(lambda a, b: a - dot_sbs(a, block_diag(b)), t, ct)
    sol = each(lambda a, b: jnp.dot(block_diag(a), b.astype(BF16), preferred_element_type=F32), t, rhs)

    qd = each(lambda a, e: a * e, q, egc)
    kd = each(lambda a, gl, x: a * jnp.exp(gl - x), k, gc_last, gc)
    g_last = [jnp.exp(x) for x in gc_last]
    heads = list(range(nh))
    state = [[st_scr[bb, h] for h in heads] for bb in rows_bb]
    ws = [[_dot(jnp.concatenate([sol[bb][h * c:(h + 1) * c, HEAD_DV:], qd[bb][h * c:(h + 1) * c]], axis=0),
                state[bb][h]) for h in heads] for bb in rows_bb]
    v_new = [jnp.concatenate([sol[bb][h * c:(h + 1) * c, :HEAD_DV] - ws[bb][h][:c] for h in heads], axis=0)
             for bb in rows_bb]
    o_intra = each(lambda a, b: jnp.dot(block_diag(a), b.astype(BF16), preferred_element_type=F32), qk, v_new)
    for h in heads:
        rows = slice(h * c, (h + 1) * c)
        for bb in rows_bb:
            o_ref[bb, :, h * HEAD_DV:(h + 1) * HEAD_DV] = (o_intra[bb][rows] + ws[bb][h][c:]).astype(o_ref.dtype)
            st_scr[bb, h] = (state[bb][h] * g_last[bb][h * c:h * c + 1, :]
                             + _dot(kd[bb][rows].T, v_new[bb][rows]))


def _gdn_cumsum_matrix(reverse):
    c, nh = LIN_CHUNK, LIN_HEADS
    p = np.arange(c)
    tri = (p[None, :] >= p[:, None]) if reverse else (p[None, :] <= p[:, None])
    return np.kron(np.eye(nh), tri.astype(np.float32))


def _gdn(qkv, tail, a_row, dt_row, *, n_ctx, reverse, direction, nb):
    q, k, v = qkv
    bsz, n_chunks, n, _ = q.shape
    c = LIN_CHUNK
    n_ctx_chunks = n_ctx // c
    cs = _doubled(_gdn_cumsum_matrix(reverse))

    def ch(s):
        return _scan_tile(s, n_ctx_chunks, n_chunks, reverse)

    const2 = lambda b, s: (0, 0)
    stacked = pl.BlockSpec((nb, None, n, HEAD_DV), lambda b, s: (b, ch(s), 0, 0))
    return pl.pallas_call(
        functools.partial(_gdn_body, reverse=reverse, beta_col=direction * LIN_HEADS,
                          g_col=(2 + direction) * LIN_HEADS, nb=nb),
        grid=(bsz // nb, n_chunks),
        in_specs=[pl.BlockSpec(cs.shape, const2),
                  pl.BlockSpec((1, LANES), const2), pl.BlockSpec((1, LANES), const2),
                  pl.BlockSpec((nb, c, LANES), lambda b, s: (b, ch(s), 0)), stacked, stacked, stacked],
        out_specs=pl.BlockSpec((nb, c, GROUP), lambda b, s: (b, ch(s), 0)),
        out_shape=jax.ShapeDtypeStruct((bsz, n_chunks * c, GROUP), ACT),
        scratch_shapes=[pltpu.VMEM((nb, LIN_HEADS, HEAD_DV, HEAD_DV), F32)],
        compiler_params=_cparams("parallel", "arbitrary"),
        name="gdn_rev" if reverse else "gdn_fwd",
    )(cs, a_row, dt_row, tail, q, k, v)


def _post_body(x_ref, af_ref, ar_ref, ga_ref, bf_ref, br_ref, gb_ref, w_ref, ml_ref, mc_ref, o_ref,
               *, even, tm, n_ctx):
    t = pl.program_id(1)
    a = af_ref[...].astype(F32) + ar_ref[...].astype(F32)
    bsum = bf_ref[...].astype(F32) + br_ref[...].astype(F32)
    if even:
        ya = a * _gelu_tanh(ga_ref[...].astype(F32))
        yb = _head_norm(bsum, True) * _silu(gb_ref[...].astype(F32))
    else:
        ya = _head_norm(a, False) * _silu(ga_ref[...].astype(F32))
        yb = _head_norm(bsum, False) * _silu(gb_ref[...].astype(F32))
    half = w_ref.shape[0] // 2
    mix = jnp.dot(ya.astype(BF16), w_ref[0:half, :], preferred_element_type=F32)
    mix = mix + jnp.dot(yb.astype(BF16), w_ref[half:, :], preferred_element_type=F32)
    o_ref[...] = x_ref[...] + _row_select(t, tm, n_ctx, mc_ref, ml_ref, 2) * mix


def _post(x, mod, a_f, a_r, b_f, b_r, p, gate_a_col, gate_b_col, w_out, *, even, tm, n_ctx):
    bsz, l, d = x.shape
    ctx_row = bsz
    tok = lambda b, t: (b, t, 0)
    grp = pl.BlockSpec((None, tm, GROUP), tok)
    return pl.pallas_call(
        functools.partial(_post_body, even=even, tm=tm, n_ctx=n_ctx),
        grid=(bsz, l // tm),
        in_specs=[pl.BlockSpec((None, tm, d), tok), grp, grp,
                  pl.BlockSpec((None, tm, GROUP), lambda b, t: (b, t, gate_a_col)), grp, grp,
                  pl.BlockSpec((None, tm, GROUP), lambda b, t: (b, t, gate_b_col)),
                  pl.BlockSpec(w_out.shape, lambda b, t: (0, 0)),
                  pl.BlockSpec((None, 6, d), lambda b, t: (b, 0, 0)),
                  pl.BlockSpec((None, 6, d), lambda b, t: (ctx_row, 0, 0))],
        out_specs=pl.BlockSpec((None, tm, d), tok),
        out_shape=jax.ShapeDtypeStruct((bsz, l, d), F32),
        compiler_params=_cparams("parallel", "parallel"),
        name="post",
    )(x, a_f, a_r, p, b_f, b_r, p, w_out, mod, mod)


def _mlp_body(x_ref, g_ref, ml_ref, mc_ref, w1_ref, w2_ref, fg_ref, o_ref, h_scr, acc_scr, *, tm, n_ctx, final):
    t = pl.program_id(1)
    f = pl.program_id(2)
    if final:
        x_ref = x_ref.at[0]

    @pl.when(f == 0)
    def _():
        y = _rms(x_ref[...], g_ref[...])
        shift = _row_select(t, tm, n_ctx, mc_ref, ml_ref, 3)
        scale = _row_select(t, tm, n_ctx, mc_ref, ml_ref, 4)
        h_scr[...] = (y * (1.0 + scale) + shift).astype(BF16)
        acc_scr[...] = jnp.zeros_like(acc_scr)

    hid = jnp.maximum(jnp.dot(h_scr[...], w1_ref[...], preferred_element_type=F32), 0.0)
    acc_scr[...] += jnp.dot((hid * hid).astype(BF16), w2_ref[...], preferred_element_type=F32)

    @pl.when(f == pl.num_programs(2) - 1)
    def _():
        y = x_ref[...] + _row_select(t, tm, n_ctx, mc_ref, ml_ref, 5) * acc_scr[...]
        if final:
            y = _rms(y, fg_ref[...])
        o_ref[...] = y


def _mlp(x, mod, g, w1, w2, final_g, *, tm, tf, n_ctx, final):
    bsz, l, d = x.shape
    dff = w1.shape[1]
    ctx_row = bsz
    tok = lambda b, t, f: (b, t, 0)
    if final:
        l = l - n_ctx
        x_spec = pl.BlockSpec((pl.Element(1), pl.Element(tm), pl.Element(d)),
                              lambda b, t, f, off=n_ctx: (b, pl.multiple_of(off + t * tm, SUBLANES), 0))
        n_ctx = 0
    else:
        x_spec = pl.BlockSpec((None, tm, d), tok)
    return pl.pallas_call(
        functools.partial(_mlp_body, tm=tm, n_ctx=n_ctx, final=final),
        grid=(bsz, l // tm, dff // tf),
        in_specs=[x_spec,
                  pl.BlockSpec((1, d), lambda b, t, f: (0, 0)),
                  pl.BlockSpec((None, 6, d), lambda b, t, f: (b, 0, 0)),
                  pl.BlockSpec((None, 6, d), lambda b, t, f: (ctx_row, 0, 0)),
                  pl.BlockSpec((d, tf), lambda b, t, f: (0, f)),
                  pl.BlockSpec((tf, d), lambda b, t, f: (f, 0)),
                  pl.BlockSpec((1, d), lambda b, t, f: (0, 0))],
        out_specs=pl.BlockSpec((None, tm, d), tok),
        out_shape=jax.ShapeDtypeStruct((bsz, l, d), F32),
        scratch_shapes=[pltpu.VMEM((tm, d), BF16), pltpu.VMEM((tm, d), F32)],
        compiler_params=_cparams("parallel", "parallel", "arbitrary"),
        name="mlp",
    )(x, g.reshape(1, d), mod, mod, w1, w2, final_g.reshape(1, d))


def _token_tile(l):
    for div in (4, 8, 16, 17, 34):
        if l % div == 0 and (l // div) % 16 == 0 and l // div <= 1088:
            return l // div
    return l


def kernel(x, c, ctx, c_ctx, ada_w, ada_b, norm1_g, norm2_g, mix_w_out, mlp_w1, mlp_w2, ev_w_in, lru_conv_w,
           lru_conv_b, lru_wa, lru_ba, lru_wx, lru_bx, lru_lambda, ret_log_gamma, od_w_in, hg_lb_logits,
           gdn_conv_w, gdn_a_log, gdn_dt_bias, final_g):
    bsz, seq, d = x.shape
    n_ctx = ctx.shape[1]
    depth = ada_w.shape[0]
    l = n_ctx + seq
    tm = _token_tile(l)
    tf = 1024
    nb = next(n for n in (8, 4, 2, 1) if bsz % n == 0)

    mods = _mods(c, c_ctx, ada_w, ada_b)
    ret_dk = (ev_w_in.shape[2] - 4 * GROUP) // (2 * RET_HEADS)
    cos_t, sin_t = _rope_tables(n_ctx, seq, ret_dk)

    xs = jnp.concatenate([ctx, x], axis=1)
    for layer in range(depth):
        mod = mods[layer]
        w_out = mix_w_out[layer].astype(BF16)
        if layer % 2 == 0:
            e = layer // 2
            (p,) = _proj(xs, mod, norm1_g[layer], ev_w_in[e].astype(BF16), None, tm=tm, n_ctx=n_ctx)
            mix = []
            for dr, rev in enumerate((False, True)):
                mix.append((
                    _lru(p, lru_conv_w[e], lru_conv_b[e], _blockdiag_halves(lru_wa[e, dr]).astype(BF16),
                         lru_ba[e, dr], _blockdiag_halves(lru_wx[e, dr]).astype(BF16), lru_bx[e, dr],
                         lru_lambda[e, dr], n_ctx=n_ctx, reverse=rev),
                    _retention(p, cos_t, sin_t, ret_log_gamma[e, dr], n_ctx=n_ctx, reverse=rev, nb=nb)))
            gate_cols = (1, 4)
        else:
            o = layer // 2
            n_main = 9 * GROUP
            w_in = od_w_in[o]
            w_tail = jnp.pad(w_in[:, n_main:], ((0, 0), (0, LANES - (w_in.shape[1] - n_main)))).astype(BF16)
            p, tail = _proj(xs, mod, norm1_g[layer], w_in[:, :n_main].astype(BF16), w_tail, tm=tm, n_ctx=n_ctx)
            qkv = _gdn_prep(p, gdn_conv_w[o], n_ctx=n_ctx)
            mix = []
            for dr, rev in enumerate((False, True)):
                g_col = (2 + dr) * LIN_HEADS
                a_row = jnp.zeros((1, LANES), F32).at[0, g_col:g_col + LIN_HEADS].set(gdn_a_log[o, dr])
                dt_row = jnp.zeros((1, LANES), F32).at[0, g_col:g_col + LIN_HEADS].set(gdn_dt_bias[o, dr])
                mix.append((
                    _gla(p, hg_lb_logits[dr], n_ctx=n_ctx, reverse=rev, layer=o, nb=nb),
                    _gdn(qkv, tail, a_row, dt_row, n_ctx=n_ctx, reverse=rev, direction=dr, nb=nb)))
            gate_cols = (4, 8)
        xs = _post(xs, mod, mix[0][0], mix[1][0], mix[0][1], mix[1][1], p, gate_cols[0], gate_cols[1], w_out,
                   even=layer % 2 == 0, tm=tm, n_ctx=n_ctx)
        last = layer == depth - 1
        xs = _mlp(xs, mod, norm2_g[layer], mlp_w1[layer].astype(BF16), mlp_w2[layer].astype(BF16), final_g,
                  tm=_token_tile(seq) if last else tm, tf=tf, n_ctx=n_ctx, final=last)
    return xs
```

```python
import functools

import numpy as np
import jax
import jax.numpy as jnp
from jax import lax
from jax.experimental import pallas as pl
from jax.experimental.pallas import tpu as pltpu

F32 = jnp.float32
BF16 = jnp.bfloat16

EPS = 1e-6
GRID_W = 64
CONV_W = 4
CONV_LEFT = CONV_W // 2
LRU_BLOCKS = 8
LRU_C = 8.0
RET_HEADS = 4
RET_CHUNK = 256
ROPE_BASE = 10000.0
LIN_HEADS = 4
GLA_CHUNK = 128
LIN_CHUNK = 64
HEAD_DV = 128

LANES = 128
SUBLANES = 8
V7X_VMEM_BYTES = 64 * 1024 * 1024
VMEM_LIMIT = (V7X_VMEM_BYTES * 3) // 4
GROUP = 512

HALO = 2 * SUBLANES
ACT = jnp.bfloat16


def _cparams(*sem):
    return pltpu.CompilerParams(dimension_semantics=sem, vmem_limit_bytes=VMEM_LIMIT)


def _sigmoid(x):
    return 1.0 / (1.0 + jnp.exp(-x))


def _silu(x):
    return x * _sigmoid(x)


def _softplus(x):
    return jnp.maximum(x, 0.0) + jnp.log(1.0 + jnp.exp(-jnp.abs(x)))


def _gelu_tanh(x):
    return 0.5 * x * (1.0 + jnp.tanh(0.7978845608028654 * (x + 0.044715 * x * x * x)))


def _dot(a, b):
    return jnp.dot(a.astype(BF16), b.astype(BF16), preferred_element_type=F32)


def _dot_nt(a, b):
    return lax.dot_general(a.astype(BF16), b.astype(BF16), (((1,), (1,)), ((), ())),
                           preferred_element_type=F32)


def _split_bf16(x):
    hi = x.astype(BF16)
    lo = (x - hi.astype(F32)).astype(BF16)
    return hi, lo


def _dot_sel(m2, x):
    hi, lo = _split_bf16(x)
    return jnp.dot(m2, jnp.concatenate([hi, lo], axis=0), preferred_element_type=F32)


def _doubled(m):
    return jnp.asarray(np.concatenate([m, m], axis=1), BF16)


def _dot3(a, b):
    ah, al = _split_bf16(a)
    bh, bl = _split_bf16(b)
    return (jnp.dot(ah, bh, preferred_element_type=F32) + jnp.dot(ah, bl, preferred_element_type=F32)
            + jnp.dot(al, bh, preferred_element_type=F32))


def _row_select(t, tm, n_ctx, mc_ref, ml_ref, idx):
    rows = t * tm + lax.broadcasted_iota(jnp.int32, (tm, 1), 0)
    return jnp.where(rows < n_ctx, mc_ref[idx:idx + 1, :], ml_ref[idx:idx + 1, :])


def _rms(x, g):
    return x * lax.rsqrt(jnp.mean(x * x, axis=-1, keepdims=True) + EPS) * g


def _head_norm(o, center):
    outs = []
    for h in range(o.shape[-1] // HEAD_DV):
        oh = o[:, h * HEAD_DV:(h + 1) * HEAD_DV]
        if center:
            oh = oh - jnp.mean(oh, axis=-1, keepdims=True)
        outs.append(oh * lax.rsqrt(jnp.mean(oh * oh, axis=-1, keepdims=True) + EPS))
    return jnp.concatenate(outs, axis=-1)


def _scan_tile(step, n_first, n_total, reverse):
    if not reverse:
        return step
    return jnp.where(step < n_first, n_first - 1 - step, n_total - 1 - (step - n_first))


def _mods_body(s_ref, w_ref, b_ref, o_ref):
    s = _silu(s_ref[...])
    o_ref[...] = _dot3(s, w_ref[...]) + b_ref[...]


def _mods(c, c_ctx, ada_w, ada_b):
    depth, d, n6 = ada_w.shape
    bsz = c.shape[0]
    rows = -(-(bsz + 1) // SUBLANES) * SUBLANES
    s = jnp.zeros((rows, d), F32).at[:bsz].set(c).at[bsz].set(c_ctx)
    tn = 1024
    out = pl.pallas_call(
        _mods_body,
        grid=(depth, n6 // tn),
        in_specs=[pl.BlockSpec((rows, d), lambda l, n: (0, 0)),
                  pl.BlockSpec((None, d, tn), lambda l, n: (l, 0, n)),
                  pl.BlockSpec((None, 1, tn), lambda l, n: (l, 0, n))],
        out_specs=pl.BlockSpec((None, rows, tn), lambda l, n: (l, 0, n)),
        out_shape=jax.ShapeDtypeStruct((depth, rows, n6), F32),
        compiler_params=_cparams("parallel", "parallel"),
        name="mods",
    )(s, ada_w, ada_b.reshape(depth, 1, n6))
    return out.reshape(depth, rows, 6, d)


def _proj_body(*refs, tm, n_ctx, has_tail):
    if has_tail:
        x_ref, ml_ref, mc_ref, g_ref, w_ref, wt_ref, o_ref, ot_ref = refs
    else:
        x_ref, ml_ref, mc_ref, g_ref, w_ref, o_ref = refs
    t = pl.program_id(1)
    y = _rms(x_ref[...], g_ref[...])
    shift = _row_select(t, tm, n_ctx, mc_ref, ml_ref, 0)
    scale = _row_select(t, tm, n_ctx, mc_ref, ml_ref, 1)
    h = (y * (1.0 + scale) + shift).astype(BF16)
    o_ref[...] = jnp.dot(h, w_ref[...], preferred_element_type=F32).astype(o_ref.dtype)
    if has_tail:
        ot_ref[...] = jnp.dot(h, wt_ref[...], preferred_element_type=F32)


def _proj(x, mod, g, w, w_tail, *, tm, n_ctx):
    bsz, l, d = x.shape
    n_out = w.shape[1]
    ctx_row = bsz
    resident = dict(pipeline_mode=pl.Buffered(1))
    in_specs = [pl.BlockSpec((None, tm, d), lambda b, t: (b, t, 0)),
                pl.BlockSpec((None, 6, d), lambda b, t: (b, 0, 0)),
                pl.BlockSpec((None, 6, d), lambda b, t: (ctx_row, 0, 0)),
                pl.BlockSpec((1, d), lambda b, t: (0, 0)),
                pl.BlockSpec((d, n_out), lambda b, t: (0, 0), **resident)]
    out_specs = [pl.BlockSpec((None, tm, n_out), lambda b, t: (b, t, 0))]
    out_shape = [jax.ShapeDtypeStruct((bsz, l, n_out), ACT)]
    args = [x, mod, mod, g.reshape(1, d), w]
    if w_tail is not None:
        in_specs.append(pl.BlockSpec((d, LANES), lambda b, t: (0, 0), **resident))
        out_specs.append(pl.BlockSpec((None, tm, LANES), lambda b, t: (b, t, 0)))
        out_shape.append(jax.ShapeDtypeStruct((bsz, l, LANES), F32))
        args.append(w_tail)
    outs = pl.pallas_call(
        functools.partial(_proj_body, tm=tm, n_ctx=n_ctx, has_tail=w_tail is not None),
        grid=(bsz, l // tm),
        in_specs=in_specs, out_specs=out_specs, out_shape=out_shape,
        compiler_params=_cparams("parallel", "parallel"),
        name="proj",
    )(*args)
    return outs


def _lru_body(sm_ref, se_ref, x_ref, xp_ref, xn_ref, cw_ref, cb_ref, wa_ref, ba_ref, wx_ref, bx_ref, lam_ref,
              o_ref, a_scr, b_scr, h_scr, *, tl, n_ctx_tiles, n_tiles, reverse):
    step = pl.program_id(0)
    tile = _scan_tile(step, n_ctx_tiles, n_tiles, reverse)
    bsz = x_ref.shape[0]
    width = x_ref.shape[2]
    half = width // 2
    seg_first = jnp.logical_or(tile == 0, tile == n_ctx_tiles)
    seg_last = jnp.logical_or(tile == n_ctx_tiles - 1, tile == n_tiles - 1)
    keep_prev = jnp.where(seg_first, 0.0, 1.0)
    keep_next = jnp.where(seg_last, 0.0, 1.0)
    neg_c_sp = -LRU_C * _softplus(-lam_ref[...])

    @pl.when(step == 0)
    def _():
        h_scr[...] = jnp.zeros_like(h_scr)

    pair = 2 if bsz % 2 == 0 else 1

    def block_dot(ub, w_ref):
        return jnp.concatenate([jnp.dot(ub[:, :half], w_ref[0], preferred_element_type=F32),
                                jnp.dot(ub[:, half:], w_ref[1], preferred_element_type=F32)], axis=-1)

    def gates(bp, carry):
        bs = [bp * pair + n for n in range(pair)]
        taps = [_conv_taps(sm_ref, se_ref, x_ref[b], xp_ref[b], xn_ref[b], keep_prev, keep_next, tl) for b in bs]
        u = [cb_ref[...] + sum(cw_ref[k:k + 1, :] * tp[k] for k in range(CONV_W)) for tp in taps]
        ub = [x.astype(BF16) for x in u]
        r_pre = [block_dot(x, wa_ref) + ba_ref[...] for x in ub]
        i_pre = [block_dot(x, wx_ref) + bx_ref[...] for x in ub]
        a = [jnp.exp(neg_c_sp * _sigmoid(x)) for x in r_pre]
        for n, b in enumerate(bs):
            a_scr[b] = a[n]
            b_scr[b] = jnp.sqrt(1.0 - a[n] * a[n]) * (_sigmoid(i_pre[n]) * u[n])
        return carry

    lax.fori_loop(0, bsz // pair, gates, 0)

    def scan_rows(j, hs):
        idx = (tl - 1 - j) if reverse else j
        new = []
        for b in range(bsz):
            h = a_scr[b, pl.ds(idx, 1), :] * hs[b] + b_scr[b, pl.ds(idx, 1), :]
            b_scr[b, pl.ds(idx, 1), :] = h
            new.append(h)
        return tuple(new)

    hs = tuple(h_scr[b, 0:1, :] for b in range(bsz))
    hs = lax.fori_loop(0, tl, scan_rows, hs)
    for b in range(bsz):
        h_scr[b, 0:1, :] = hs[b]
    o_ref[...] = b_scr[...].astype(o_ref.dtype)


def _lru(p, conv_w, conv_b, wa, ba, wx, bx, lam, *, n_ctx, reverse, tl=256):
    bsz, l, _ = p.shape
    width = GROUP
    n_tiles = l // tl
    n_ctx_tiles = n_ctx // tl
    hb = tl // HALO
    n_hblocks = l // HALO

    def tile_of(s):
        return _scan_tile(s, n_ctx_tiles, n_tiles, reverse)

    vec = pl.BlockSpec((1, width), lambda s: (0, 0))
    mat = pl.BlockSpec((2, width // 2, width // 2), lambda s: (0, 0, 0))
    sm, se = (jnp.asarray(m, p.dtype) for m in _shift_matrices(tl))
    return pl.pallas_call(
        functools.partial(_lru_body, tl=tl, n_ctx_tiles=n_ctx_tiles, n_tiles=n_tiles, reverse=reverse),
        grid=(n_tiles,),
        in_specs=[pl.BlockSpec(sm.shape, lambda s: (0, 0)), pl.BlockSpec(se.shape, lambda s: (0, 0)),
                  pl.BlockSpec((bsz, tl, width), lambda s: (0, tile_of(s), 0)),
                  pl.BlockSpec((bsz, HALO, width), lambda s: (0, jnp.maximum(tile_of(s) * hb - 1, 0), 0)),
                  pl.BlockSpec((bsz, HALO, width),
                               lambda s: (0, jnp.minimum((tile_of(s) + 1) * hb, n_hblocks - 1), 0)),
                  pl.BlockSpec((CONV_W, width), lambda s: (0, 0)), vec, mat, vec, mat, vec, vec],
        out_specs=pl.BlockSpec((bsz, tl, width), lambda s: (0, tile_of(s), 0)),
        out_shape=jax.ShapeDtypeStruct((bsz, l, width), ACT),
        scratch_shapes=[pltpu.VMEM((bsz, tl, width), F32),
                        pltpu.VMEM((bsz, tl, width), F32),
                        pltpu.VMEM((bsz, SUBLANES, width), F32)],
        compiler_params=_cparams("arbitrary"),
        name="lru_rev" if reverse else "lru_fwd",
    )(sm, se, p, p, p, conv_w, conv_b.reshape(1, width), wa, ba.reshape(1, width), wx, bx.reshape(1, width),
      lam.reshape(1, width))


def _blockdiag_halves(w):
    nb, n, _ = w.shape
    per = nb // 2
    w = w.reshape(2, per, n, n)
    eye = jnp.eye(per, dtype=w.dtype)
    out = w[:, :, :, None, :] * eye[None, :, None, :, None]
    return out.reshape(2, per * n, per * n)


def _ret_body(*refs, reverse, dk, nb):
    if reverse:
        lg_ref, q_ref, k_ref, v_ref, cos_ref, sin_ref, acc_ref, o_ref, qd_scr, kd_scr, cd_scr, st_scr = refs
    else:
        lg_ref, lgo_ref, q_ref, k_ref, v_ref, cos_ref, sin_ref, o_ref, dec_scr, qd_scr, kd_scr, cd_scr, st_scr = refs
    c = RET_CHUNK
    step = pl.program_id(1)

    @pl.when(step == 0)
    def _():
        st_scr[...] = jnp.zeros_like(st_scr)
        row = lax.broadcasted_iota(jnp.int32, (c, HEAD_DV), 0)
        pos = ((c - 1 - row) if reverse else row).astype(F32)
        for h in range(RET_HEADS):
            lg = lg_ref[h]
            qd_scr[h] = jnp.exp(lg * (pos + 1.0))
            kd_scr[h] = jnp.exp(lg * (c - 1.0 - pos))
            cd_scr[h] = jnp.exp(jnp.full((SUBLANES, LANES), c, F32) * lg)
        if not reverse:
            i = lax.broadcasted_iota(jnp.int32, (c, c), 0)
            j = lax.broadcasted_iota(jnp.int32, (c, c), 1)
            below = jnp.maximum(i - j, 0).astype(F32)
            above = jnp.maximum(j - i, 0).astype(F32)
            for h in range(RET_HEADS):
                dec_scr[h] = (jnp.where(i >= j, jnp.exp(lg_ref[h] * below), 0.0)
                              + jnp.where(j >= i, jnp.exp(lgo_ref[h] * above), 0.0))

    def rope(x):
        width = x.shape[-1]
        lane = lax.broadcasted_iota(jnp.int32, x.shape, 1)
        first = (lane & (dk - 1)) < dk // 2
        rot = jnp.where(first, pltpu.roll(x, width - dk // 2, axis=1), pltpu.roll(x, dk // 2, axis=1))
        return x * cos_ref[...] + rot * sin_ref[...]

    q = [rope(q_ref[bb].astype(F32)) for bb in range(nb)]
    k = [rope(k_ref[bb].astype(F32)) * (dk ** -0.5) for bb in range(nb)]
    pairs = [(bb, h) for bb in range(nb) for h in range(RET_HEADS)]
    qh = [q[bb][:, h * dk:(h + 1) * dk] for bb, h in pairs]
    kh = [k[bb][:, h * dk:(h + 1) * dk] for bb, h in pairs]
    vh = [v_ref[bb, :, h * HEAD_DV:(h + 1) * HEAD_DV] for bb, h in pairs]
    state = [st_scr[bb, h] for bb, h in pairs]
    o_x = [_dot(qh[i], state[i]) * qd_scr[h] for i, (bb, h) in enumerate(pairs)]
    if reverse:
        o_in = [acc_ref[bb, :, h * HEAD_DV:(h + 1) * HEAD_DV].astype(F32) for bb, h in pairs]
    else:
        s = [_dot_nt(qh[i], kh[i]) * dec_scr[h] for i, (bb, h) in enumerate(pairs)]
        o_in = [_dot(s[i], vh[i]) for i in range(len(pairs))]
    kv = [_dot((kh[i] * kd_scr[h][:, :dk]).T, vh[i]) for i, (bb, h) in enumerate(pairs)]
    for i, (bb, h) in enumerate(pairs):
        st_scr[bb, h] = state[i] * cd_scr[h][0:1, :] + kv[i]
        o_ref[bb, :, h * HEAD_DV:(h + 1) * HEAD_DV] = (o_in[i] + o_x[i]).astype(o_ref.dtype)


def _retention(p, cos_t, sin_t, log_gamma, other, *, n_ctx, reverse, nb):
    bsz, l, _ = p.shape
    c = RET_CHUNK
    qk_w = cos_t.shape[1]
    dk = qk_w // RET_HEADS
    n_chunks = l // c
    n_ctx_chunks = n_ctx // c

    def ch(s):
        return _scan_tile(s, n_ctx_chunks, n_chunks, reverse)

    q_col = (2 * GROUP) // qk_w
    smem = pl.BlockSpec(memory_space=pltpu.SMEM)
    out_block = pl.BlockSpec((nb, c, GROUP), lambda b, s: (b, ch(s), 0))
    data_specs = [pl.BlockSpec((nb, c, qk_w), lambda b, s: (b, ch(s), q_col)),
                  pl.BlockSpec((nb, c, qk_w), lambda b, s: (b, ch(s), q_col + 1)),
                  pl.BlockSpec((nb, c, GROUP), lambda b, s: (b, ch(s), 3)),
                  pl.BlockSpec((c, qk_w), lambda b, s: (ch(s), 0)),
                  pl.BlockSpec((c, qk_w), lambda b, s: (ch(s), 0))]
    scratch = [pltpu.VMEM((RET_HEADS, c, HEAD_DV), F32),
               pltpu.VMEM((RET_HEADS, c, HEAD_DV), F32),
               pltpu.VMEM((RET_HEADS, SUBLANES, LANES), F32),
               pltpu.VMEM((nb, RET_HEADS, dk, HEAD_DV), F32)]
    if reverse:
        in_specs = [smem] + data_specs + [out_block]
        args = (log_gamma, p, p, p, cos_t, sin_t, other)
    else:
        in_specs = [smem, smem] + data_specs
        args = (log_gamma, other, p, p, p, cos_t, sin_t)
        scratch = [pltpu.VMEM((RET_HEADS, c, c), F32)] + scratch
    return pl.pallas_call(
        functools.partial(_ret_body, reverse=reverse, dk=dk, nb=nb),
        grid=(bsz // nb, n_chunks),
        in_specs=in_specs,
        out_specs=out_block,
        out_shape=jax.ShapeDtypeStruct((bsz, l, GROUP), ACT),
        scratch_shapes=scratch,
        compiler_params=_cparams("parallel", "arbitrary"),
        name="ret_rev" if reverse else "ret_fwd",
    )(*args)


def _rope_tables(n_ctx, seq, dk):
    n_freq = dk // 4
    inv = jnp.power(ROPE_BASE, -jnp.arange(n_freq, dtype=F32) / n_freq)
    rows = seq // GRID_W
    r = jnp.arange(rows, dtype=F32)
    col = jnp.arange(GRID_W, dtype=F32)
    row_ang = jnp.broadcast_to(r[:, None, None] * inv, (rows, GRID_W, n_freq))
    col_ang = jnp.broadcast_to(col[None, :, None] * inv, (rows, GRID_W, n_freq))
    ang = jnp.concatenate([row_ang, col_ang], axis=-1).reshape(rows * GRID_W, 2 * n_freq)
    cos, sin = jnp.cos(ang), jnp.sin(ang)
    cos_h = jnp.concatenate([cos, cos], axis=-1)
    sin_h = jnp.concatenate([-sin, sin], axis=-1)
    cos_t = jnp.concatenate([jnp.ones((n_ctx, dk), F32), cos_h], axis=0)
    sin_t = jnp.concatenate([jnp.zeros((n_ctx, dk), F32), sin_h], axis=0)
    return jnp.tile(cos_t, (1, RET_HEADS)), jnp.tile(sin_t, (1, RET_HEADS))


_GLA_LEVELS = tuple(GLA_CHUNK >> (i + 1) for i in range(GLA_CHUNK.bit_length() - 1))


def _gla_matrices(reverse):
    c = GLA_CHUNK
    p = np.arange(c)
    mats = [(p[None, :] <= p[:, None]), (p[None, :] > p[:, None])]
    for s in _GLA_LEVELS:
        bs = (p // s) * s
        right = ((p // s) % 2 == 1)
        m_r = (p[None, :] > bs[:, None]) & (p[None, :] <= p[:, None])
        m_l = (p[None, :] > p[:, None]) & (p[None, :] <= (bs + s)[:, None])
        mats.append(np.where(right[:, None], m_r, m_l))
    m = np.stack(mats).astype(np.float32)
    if reverse:
        m = m[:, ::-1, ::-1]
    return m.reshape(len(mats) * c, c)


def _gla_body(m_ref, lb_ref, q_ref, f_ref, v_ref, o_ref, st_scr, *, reverse, layer, nb):
    c = GLA_CHUNK
    step = pl.program_id(1)

    @pl.when(step == 0)
    def _():
        st_scr[...] = jnp.zeros_like(st_scr)

    lg = lb_ref[...]
    e = jnp.exp(lg - jnp.max(lg, axis=0, keepdims=True))
    sm = e / jnp.sum(e, axis=0, keepdims=True)
    lb = jnp.sum(sm[0:layer + 1, :], axis=0, keepdims=True) - sm[0:1, :]

    ri = lax.broadcasted_iota(jnp.int32, (c, 1), 0)
    i2 = lax.broadcasted_iota(jnp.int32, (c, c), 0)
    j2 = lax.broadcasted_iota(jnp.int32, (c, c), 1)
    if reverse:
        ri, i2, j2 = c - 1 - ri, c - 1 - i2, c - 1 - j2
    last = 0 if reverse else c - 1
    width = q_ref.shape[-1]
    nh = width // HEAD_DV

    rows_bb = list(range(nb))
    pairs = [(bb, h) for bb in rows_bb for h in range(nh)]

    def head(x, h):
        return x[:, h * HEAD_DV:(h + 1) * HEAD_DV]

    f = [lb + (1.0 - lb) * _sigmoid(f_ref[bb].astype(F32)) for bb in rows_bb]
    k = [1.0 - x for x in f]
    q = [_silu(q_ref[bb].astype(F32)) for bb in rows_bb]
    ex = [_dot_sel(m_ref[...], jnp.log(x)) for x in f]
    a_last = [jnp.exp(x[last:last + 1, :]) for x in ex]
    qb = [x.astype(BF16) for x in q]
    kb = [x.astype(BF16) for x in k]
    qa = [qb[bb] * jnp.exp(ex[bb][0:c, :].astype(BF16)) for bb in rows_bb]
    kl = [kb[bb] * jnp.exp(ex[bb][c:2 * c, :].astype(BF16)) for bb in rows_bb]
    scores = [jnp.where(i2 == j2, _dot_nt(head(qb[bb], h), head(kb[bb], h)), 0.0) for bb, h in pairs]
    for lvl, s in enumerate(_GLA_LEVELS):
        sh = s.bit_length() - 1
        right = jnp.broadcast_to(((ri >> sh) & 1).astype(F32), (c, width)).astype(BF16) > 0
        x = [jnp.where(right, qb[bb], kb[bb]) * jnp.exp(ex[bb][(2 + lvl) * c:(3 + lvl) * c, :].astype(BF16))
             for bb in rows_bb]
        bi, bj = i2 >> sh, j2 >> sh
        valid = bj == jnp.where((bi & 1) == 1, bi - 1, -1)
        gram = [_dot_nt(head(x[bb], h), head(x[bb], h)) for bb, h in pairs]
        scores = [jnp.where(valid, gm, sc) for sc, gm in zip(scores, gram)]

    state_t = [st_scr[bb, h] for bb, h in pairs]
    o_intra = [_dot(scores[i], head(v_ref[bb], h)) for i, (bb, h) in enumerate(pairs)]
    o_inter = [_dot_nt(head(qa[bb], h), state_t[i]) for i, (bb, h) in enumerate(pairs)]
    kv = [_dot(head(v_ref[bb], h).astype(F32).T, head(kl[bb], h)) for bb, h in pairs]
    for i, (bb, h) in enumerate(pairs):
        o_ref[bb, :, h * HEAD_DV:(h + 1) * HEAD_DV] = (o_intra[i] + o_inter[i]).astype(o_ref.dtype)
        st_scr[bb, h] = state_t[i] * head(a_last[bb], h) + kv[i]


def _gla(p, lb_logits, *, n_ctx, reverse, layer, nb):
    bsz, l, _ = p.shape
    c = GLA_CHUNK
    n_chunks = l // c
    n_ctx_chunks = n_ctx // c
    m = _doubled(_gla_matrices(reverse))

    def ch(s):
        return _scan_tile(s, n_ctx_chunks, n_chunks, reverse)

    f_col = 2 if reverse else 1
    return pl.pallas_call(
        functools.partial(_gla_body, reverse=reverse, layer=layer, nb=nb),
        grid=(bsz // nb, n_chunks),
        in_specs=[pl.BlockSpec(m.shape, lambda b, s: (0, 0)),
                  pl.BlockSpec(lb_logits.shape, lambda b, s: (0, 0)),
                  pl.BlockSpec((nb, c, GROUP), lambda b, s: (b, ch(s), 0)),
                  pl.BlockSpec((nb, c, GROUP), lambda b, s: (b, ch(s), f_col)),
                  pl.BlockSpec((nb, c, GROUP), lambda b, s: (b, ch(s), 3))],
        out_specs=pl.BlockSpec((nb, c, GROUP), lambda b, s: (b, ch(s), 0)),
        out_shape=jax.ShapeDtypeStruct((bsz, l, GROUP), ACT),
        scratch_shapes=[pltpu.VMEM((nb, LIN_HEADS, HEAD_DV, HEAD_DV), F32)],
        compiler_params=_cparams("parallel", "arbitrary"),
        name="gla_rev" if reverse else "gla_fwd",
    )(m, lb_logits, p, p, p)


def _shift_matrices(rows):
    offs = [t - CONV_LEFT for t in range(CONV_W) if t != CONV_LEFT]
    main = np.zeros((len(offs) * rows, rows), np.float32)
    edge = np.zeros((len(offs) * SUBLANES, 2 * HALO), np.float32)
    for n, off in enumerate(offs):
        for i in range(rows):
            j = i + off
            if 0 <= j < rows:
                main[n * rows + i, j] = 1.0
            elif j < 0:
                edge[n * SUBLANES + i, HALO + j] = 1.0
            else:
                edge[n * SUBLANES + i - (rows - SUBLANES), HALO + j - rows] = 1.0
    return main, edge


def _conv_taps(sm_ref, se_ref, x, xp, xn, keep_prev, keep_next, rows):
    sh = jnp.dot(sm_ref[...], x, preferred_element_type=F32)
    halo = jnp.concatenate([xp * keep_prev.astype(xp.dtype), xn * keep_next.astype(xn.dtype)], axis=0)
    fix = jnp.dot(se_ref[...], halo, preferred_element_type=F32)
    taps, n = [], 0
    for t in range(CONV_W):
        off = t - CONV_LEFT
        if off == 0:
            taps.append(x.astype(F32))
            continue
        s = sh[n * rows:(n + 1) * rows, :]
        f = fix[n * SUBLANES:(n + 1) * SUBLANES, :]
        if off < 0:
            s = jnp.concatenate([s[0:SUBLANES] + f, s[SUBLANES:]], axis=0)
        else:
            s = jnp.concatenate([s[:rows - SUBLANES], s[rows - SUBLANES:] + f], axis=0)
        taps.append(s)
        n += 1
    return taps


def _gdn_prep_body(sm_ref, se_ref, cw_ref, q_ref, qp_ref, qn_ref, k_ref, kp_ref, kn_ref, v_ref, vp_ref, vn_ref,
                   qo_ref, ko_ref, vo_ref, *, tp, n_ctx_tiles, n_tiles):
    c = LIN_CHUNK
    nh = LIN_HEADS
    tile = pl.program_id(1)
    seg_first = jnp.logical_or(tile == 0, tile == n_ctx_tiles)
    seg_last = jnp.logical_or(tile == n_ctx_tiles - 1, tile == n_tiles - 1)
    keep_prev = jnp.where(seg_first, 0.0, 1.0)
    keep_next = jnp.where(seg_last, 0.0, 1.0)

    all_taps = [_conv_taps(sm_ref, se_ref, x[...], xp[...], xn[...], keep_prev, keep_next, tp)
                for x, xp, xn in ((q_ref, qp_ref, qn_ref), (k_ref, kp_ref, kn_ref), (v_ref, vp_ref, vn_ref))]

    def conv_silu(col):
        taps = all_taps[col]
        u = cw_ref[0:1, col * GROUP:(col + 1) * GROUP] * taps[0]
        for t in range(1, CONV_W):
            u = u + cw_ref[t:t + 1, col * GROUP:(col + 1) * GROUP] * taps[t]
        return _silu(u)

    def l2n(x):
        return x * lax.rsqrt(jnp.sum(x * x, axis=-1, keepdims=True) + EPS)

    def emit(out_ref, x, norm, scale):
        for h in range(nh):
            xh = x[:, h * HEAD_DV:(h + 1) * HEAD_DV]
            if norm:
                xh = l2n(xh) * scale
            for j in range(tp // c):
                out_ref[j, h * c:(h + 1) * c, :] = xh[j * c:(j + 1) * c, :].astype(out_ref.dtype)

    emit(qo_ref, conv_silu(0), True, HEAD_DV ** -0.5)
    emit(ko_ref, conv_silu(1), True, 1.0)
    emit(vo_ref, conv_silu(2), False, 1.0)


def _gdn_prep(p, conv_w, *, n_ctx, tp=256):
    bsz, l, _ = p.shape
    c = LIN_CHUNK
    n_tiles = l // tp
    hb = tp // HALO
    n_hblocks = l // HALO

    def triple(col):
        return [pl.BlockSpec((None, tp, GROUP), lambda b, t: (b, t, col)),
                pl.BlockSpec((None, HALO, GROUP), lambda b, t: (b, jnp.maximum(t * hb - 1, 0), col)),
                pl.BlockSpec((None, HALO, GROUP), lambda b, t: (b, jnp.minimum((t + 1) * hb, n_hblocks - 1), col))]

    out_spec = pl.BlockSpec((None, tp // c, LIN_HEADS * c, HEAD_DV), lambda b, t: (b, t, 0, 0))
    out_sds = jax.ShapeDtypeStruct((bsz, l // c, LIN_HEADS * c, HEAD_DV), ACT)
    sm, se = (jnp.asarray(m, p.dtype) for m in _shift_matrices(tp))
    const2 = lambda b, t: (0, 0)
    return pl.pallas_call(
        functools.partial(_gdn_prep_body, tp=tp, n_ctx_tiles=n_ctx // tp, n_tiles=n_tiles),
        grid=(bsz, n_tiles),
        in_specs=[pl.BlockSpec(sm.shape, const2), pl.BlockSpec(se.shape, const2), pl.BlockSpec(conv_w.shape, const2)]
                 + triple(5) + triple(6) + triple(7),
        out_specs=[out_spec] * 3, out_shape=[out_sds] * 3,
        compiler_params=_cparams("parallel", "parallel"),
        name="gdn_prep",
    )(sm, se, conv_w, *([p] * 9))


def _gdn_body(cs_ref, arow_ref, dtrow_ref, t_ref, q_ref, k_ref, v_ref, o_ref, st_scr,
              *, reverse, beta_col, g_col, nb):
    c = LIN_CHUNK
    nh = LIN_HEADS
    n = nh * c
    step = pl.program_id(1)

    @pl.when(step == 0)
    def _():
        st_scr[...] = jnp.zeros_like(st_scr)

    i2 = lax.broadcasted_iota(jnp.int32, (c, n), 0)
    lane = lax.broadcasted_iota(jnp.int32, (c, n), 1)
    j2 = lane & (c - 1)
    grp = lane >> 6
    mask = (j2 >= i2) if reverse else (j2 <= i2)
    same32 = (i2 >> 5) == (j2 >> 5)
    same16 = (i2 >> 4) == (j2 >> 4)
    eye = jnp.where(i2 == j2, 1.0, 0.0)
    grp_rows = [jnp.where((lax.broadcasted_iota(jnp.int32, (1, n), 1) >> 6) == h, 1.0, 0.0).astype(BF16)
                for h in range(nh)]
    last = 0 if reverse else c - 1

    def each(fn, *lists):
        return [fn(*args) for args in zip(*lists)]

    def head_bcast(x, col):
        return jnp.concatenate([jnp.broadcast_to(x[:, col + h:col + h + 1], (c, HEAD_DV)) for h in range(nh)], axis=0)

    def side_by_side(x):
        out = x[(nh - 1) * c:nh * c, :]
        for h in range(nh - 2, -1, -1):
            out = jnp.where(grp == h, x[h * c:(h + 1) * c, :], out)
        return out

    def block_diag(x):
        xb = x.astype(BF16)
        return jnp.concatenate([xb * grp_rows[h] for h in range(nh)], axis=0)

    def dot_sbs(x, y_bd):
        return jnp.dot(x.astype(BF16), y_bd, preferred_element_type=F32)

    rows_bb = list(range(nb))
    q = [q_ref[bb].astype(F32) for bb in rows_bb]
    k = [k_ref[bb].astype(F32) for bb in rows_bb]
    v = [v_ref[bb].astype(F32) for bb in rows_bb]

    tail = [t_ref[bb] for bb in rows_bb]
    beta = [head_bcast(_sigmoid(x), beta_col) for x in tail]
    g = [head_bcast(-jnp.exp(arow_ref[...]) * _softplus(x + dtrow_ref[...]), g_col) for x in tail]
    gc = [_dot_sel(cs_ref[...], x) for x in g]
    gc_last = [jnp.concatenate([jnp.broadcast_to(x[h * c + last:h * c + last + 1, :], (c, HEAD_DV))
                                for h in range(nh)], axis=0) for x in gc]
    diff = [side_by_side(jnp.concatenate([x, x], axis=1)) - jnp.transpose(x)[0:1, :] for x in gc]
    gam = [jnp.where(mask, jnp.exp(jnp.where(mask, x, 0.0)), 0.0) for x in diff]

    kb = each(lambda a, b: a * b, k, beta)
    kq = each(lambda a, b, kx: _dot_nt(jnp.concatenate([a, b], axis=0), kx), kb, q, k)
    mm = each(lambda x, gm: jnp.where(i2 == j2, 0.0, side_by_side(x[:n]) * gm), kq, gam)
    qk = each(lambda x, gm: side_by_side(x[n:]) * gm, kq, gam)
    egc = [jnp.exp(x) for x in gc]
    rhs = each(lambda a, b, kbx, e: jnp.concatenate([a * b, kbx * e], axis=1), v, beta, kb, egc)

    md = [jnp.where(same16, x, 0.0) for x in mm]
    m2 = each(lambda a: dot_sbs(a, block_diag(a)), md)
    t = [eye - x for x in md]
    tm = each(lambda a, b: dot_sbs(jnp.concatenate([a, b], axis=0), block_diag(b)), t, m2)
    t = each(lambda a, x: a + x[:c], t, tm)
    m4 = [x[c:] for x in tm]
    tm = each(lambda a, b: dot_sbs(jnp.concatenate([a, b], axis=0), block_diag(b)), t, m4)
    t = each(lambda a, x: a + x[:c], t, tm)
    t = each(lambda a, x: a + dot_sbs(a, block_diag(x[c:])), t, tm)
    c32 = each(lambda a, b: jnp.where(same32, a - b, 0.0), mm, md)
    ct = each(lambda a, b: dot_sbs(a, block_diag(b)), c32, t)
    t = each(lambda a, b: a - dot_sbs(a, block_diag(b)), t, ct)
    c64 = [jnp.where(same32, 0.0, x) for x in mm]
    ct = each(lambda a, b: dot_sbs(a, block_diag(b)), c64, t)
    t = each(lambda a, b: a - dot_sbs(a, block_diag(b)), t, ct)
    sol = each(lambda a, b: jnp.dot(block_diag(a), b.astype(BF16), preferred_element_type=F32), t, rhs)

    qd = each(lambda a, e: a * e, q, egc)
    kd = each(lambda a, gl, x: a * jnp.exp(gl - x), k, gc_last, gc)
    g_last = [jnp.exp(x) for x in gc_last]
    heads = list(range(nh))
    state = [[st_scr[bb, h] for h in heads] for bb in rows_bb]
    ws = [[_dot(jnp.concatenate([sol[bb][h * c:(h + 1) * c, HEAD_DV:], qd[bb][h * c:(h + 1) * c]], axis=0),
                state[bb][h]) for h in heads] for bb in rows_bb]
    v_new = [jnp.concatenate([sol[bb][h * c:(h + 1) * c, :HEAD_DV] - ws[bb][h][:c] for h in heads], axis=0)
             for bb in rows_bb]
    o_intra = each(lambda a, b: jnp.dot(block_diag(a), b.astype(BF16), preferred_element_type=F32), qk, v_new)
    for h in heads:
        rows = slice(h * c, (h + 1) * c)
        for bb in rows_bb:
            o_ref[bb, :, h * HEAD_DV:(h + 1) * HEAD_DV] = (o_intra[bb][rows] + ws[bb][h][c:]).astype(o_ref.dtype)
            st_scr[bb, h] = (state[bb][h] * g_last[bb][h * c:h * c + 1, :]
                             + _dot(kd[bb][rows].T, v_new[bb][rows]))


def _gdn_cumsum_matrix(reverse):
    c, nh = LIN_CHUNK, LIN_HEADS
    p = np.arange(c)
    tri = (p[None, :] >= p[:, None]) if reverse else (p[None, :] <= p[:, None])
    return np.kron(np.eye(nh), tri.astype(np.float32))


def _gdn(qkv, tail, a_row, dt_row, *, n_ctx, reverse, direction, nb):
    q, k, v = qkv
    bsz, n_chunks, n, _ = q.shape
    c = LIN_CHUNK
    n_ctx_chunks = n_ctx // c
    cs = _doubled(_gdn_cumsum_matrix(reverse))

    def ch(s):
        return _scan_tile(s, n_ctx_chunks, n_chunks, reverse)

    const2 = lambda b, s: (0, 0)
    stacked = pl.BlockSpec((nb, None, n, HEAD_DV), lambda b, s: (b, ch(s), 0, 0))
    return pl.pallas_call(
        functools.partial(_gdn_body, reverse=reverse, beta_col=direction * LIN_HEADS,
                          g_col=(2 + direction) * LIN_HEADS, nb=nb),
        grid=(bsz // nb, n_chunks),
        in_specs=[pl.BlockSpec(cs.shape, const2),
                  pl.BlockSpec((1, LANES), const2), pl.BlockSpec((1, LANES), const2),
                  pl.BlockSpec((nb, c, LANES), lambda b, s: (b, ch(s), 0)), stacked, stacked, stacked],
        out_specs=pl.BlockSpec((nb, c, GROUP), lambda b, s: (b, ch(s), 0)),
        out_shape=jax.ShapeDtypeStruct((bsz, n_chunks * c, GROUP), ACT),
        scratch_shapes=[pltpu.VMEM((nb, LIN_HEADS, HEAD_DV, HEAD_DV), F32)],
        compiler_params=_cparams("parallel", "arbitrary"),
        name="gdn_rev" if reverse else "gdn_fwd",
    )(cs, a_row, dt_row, tail, q, k, v)


def _post_body(*refs, even, tm, n_ctx, n_a, n_b):
    x_ref = refs[0]
    a_refs = refs[1:1 + n_a]
    ga_ref = refs[1 + n_a]
    b_refs = refs[2 + n_a:2 + n_a + n_b]
    gb_ref, w_ref, ml_ref, mc_ref, o_ref = refs[2 + n_a + n_b:]
    t = pl.program_id(1)
    a = sum(r[...].astype(F32) for r in a_refs)
    bsum = sum(r[...].astype(F32) for r in b_refs)
    if even:
        ya = a * _gelu_tanh(ga_ref[...].astype(F32))
        yb = _head_norm(bsum, True) * _silu(gb_ref[...].astype(F32))
    else:
        ya = _head_norm(a, False) * _silu(ga_ref[...].astype(F32))
        yb = _head_norm(bsum, False) * _silu(gb_ref[...].astype(F32))
    half = w_ref.shape[0] // 2
    mix = jnp.dot(ya.astype(BF16), w_ref[0:half, :], preferred_element_type=F32)
    mix = mix + jnp.dot(yb.astype(BF16), w_ref[half:, :], preferred_element_type=F32)
    o_ref[...] = x_ref[...] + _row_select(t, tm, n_ctx, mc_ref, ml_ref, 2) * mix


def _post(x, mod, a_parts, b_parts, p, gate_a_col, gate_b_col, w_out, *, even, tm, n_ctx):
    bsz, l, d = x.shape
    ctx_row = bsz
    tok = lambda b, t: (b, t, 0)
    grp = pl.BlockSpec((None, tm, GROUP), tok)
    return pl.pallas_call(
        functools.partial(_post_body, even=even, tm=tm, n_ctx=n_ctx, n_a=len(a_parts), n_b=len(b_parts)),
        grid=(bsz, l // tm),
        in_specs=[pl.BlockSpec((None, tm, d), tok)] + [grp] * len(a_parts)
                 + [pl.BlockSpec((None, tm, GROUP), lambda b, t: (b, t, gate_a_col))] + [grp] * len(b_parts)
                 + [pl.BlockSpec((None, tm, GROUP), lambda b, t: (b, t, gate_b_col)),
                    pl.BlockSpec(w_out.shape, lambda b, t: (0, 0)),
                    pl.BlockSpec((None, 6, d), lambda b, t: (b, 0, 0)),
                    pl.BlockSpec((None, 6, d), lambda b, t: (ctx_row, 0, 0))],
        out_specs=pl.BlockSpec((None, tm, d), tok),
        out_shape=jax.ShapeDtypeStruct((bsz, l, d), F32),
        compiler_params=_cparams("parallel", "parallel"),
        name="post",
    )(x, *a_parts, p, *b_parts, p, w_out, mod, mod)


def _mlp_body(x_ref, g_ref, ml_ref, mc_ref, w1_ref, w2_ref, fg_ref, o_ref, h_scr, acc_scr, *, tm, n_ctx, final):
    t = pl.program_id(1)
    f = pl.program_id(2)
    if final:
        x_ref = x_ref.at[0]

    @pl.when(f == 0)
    def _():
        y = _rms(x_ref[...], g_ref[...])
        shift = _row_select(t, tm, n_ctx, mc_ref, ml_ref, 3)
        scale = _row_select(t, tm, n_ctx, mc_ref, ml_ref, 4)
        h_scr[...] = (y * (1.0 + scale) + shift).astype(BF16)
        acc_scr[...] = jnp.zeros_like(acc_scr)

    hid = jnp.maximum(jnp.dot(h_scr[...], w1_ref[...], preferred_element_type=F32), 0.0)
    acc_scr[...] += jnp.dot((hid * hid).astype(BF16), w2_ref[...], preferred_element_type=F32)

    @pl.when(f == pl.num_programs(2) - 1)
    def _():
        y = x_ref[...] + _row_select(t, tm, n_ctx, mc_ref, ml_ref, 5) * acc_scr[...]
        if final:
            y = _rms(y, fg_ref[...])
        o_ref[...] = y


def _mlp(x, mod, g, w1, w2, final_g, *, tm, tf, n_ctx, final):
    bsz, l, d = x.shape
    dff = w1.shape[1]
    ctx_row = bsz
    tok = lambda b, t, f: (b, t, 0)
    if final:
        l = l - n_ctx
        x_spec = pl.BlockSpec((pl.Element(1), pl.Element(tm), pl.Element(d)),
                              lambda b, t, f, off=n_ctx: (b, pl.multiple_of(off + t * tm, SUBLANES), 0))
        n_ctx = 0
    else:
        x_spec = pl.BlockSpec((None, tm, d), tok)
    return pl.pallas_call(
        functools.partial(_mlp_body, tm=tm, n_ctx=n_ctx, final=final),
        grid=(bsz, l // tm, dff // tf),
        in_specs=[x_spec,
                  pl.BlockSpec((1, d), lambda b, t, f: (0, 0)),
                  pl.BlockSpec((None, 6, d), lambda b, t, f: (b, 0, 0)),
                  pl.BlockSpec((None, 6, d), lambda b, t, f: (ctx_row, 0, 0)),
                  pl.BlockSpec((d, tf), lambda b, t, f: (0, f)),
                  pl.BlockSpec((tf, d), lambda b, t, f: (f, 0)),
                  pl.BlockSpec((1, d), lambda b, t, f: (0, 0))],
        out_specs=pl.BlockSpec((None, tm, d), tok),
        out_shape=jax.ShapeDtypeStruct((bsz, l, d), F32),
        scratch_shapes=[pltpu.VMEM((tm, d), BF16), pltpu.VMEM((tm, d), F32)],
        compiler_params=_cparams("parallel", "parallel", "arbitrary"),
        name="mlp",
    )(x, g.reshape(1, d), mod, mod, w1, w2, final_g.reshape(1, d))


def _token_tile(l):
    for div in (4, 8, 16, 17, 34):
        if l % div == 0 and (l // div) % 16 == 0 and l // div <= 1088:
            return l // div
    return l


def kernel(x, c, ctx, c_ctx, ada_w, ada_b, norm1_g, norm2_g, mix_w_out, mlp_w1, mlp_w2, ev_w_in, lru_conv_w,
           lru_conv_b, lru_wa, lru_ba, lru_wx, lru_bx, lru_lambda, ret_log_gamma, od_w_in, hg_lb_logits,
           gdn_conv_w, gdn_a_log, gdn_dt_bias, final_g):
    bsz, seq, d = x.shape
    n_ctx = ctx.shape[1]
    depth = ada_w.shape[0]
    l = n_ctx + seq
    tm = _token_tile(l)
    tf = 1024
    nb = next(n for n in (8, 4, 2, 1) if bsz % n == 0)

    mods = _mods(c, c_ctx, ada_w, ada_b)
    ret_dk = (ev_w_in.shape[2] - 4 * GROUP) // (2 * RET_HEADS)
    cos_t, sin_t = _rope_tables(n_ctx, seq, ret_dk)

    xs = jnp.concatenate([ctx, x], axis=1)
    for layer in range(depth):
        mod = mods[layer]
        w_out = mix_w_out[layer].astype(BF16)
        if layer % 2 == 0:
            e = layer // 2
            (p,) = _proj(xs, mod, norm1_g[layer], ev_w_in[e].astype(BF16), None, tm=tm, n_ctx=n_ctx)
            a_parts = tuple(
                _lru(p, lru_conv_w[e], lru_conv_b[e], _blockdiag_halves(lru_wa[e, dr]).astype(BF16),
                     lru_ba[e, dr], _blockdiag_halves(lru_wx[e, dr]).astype(BF16), lru_bx[e, dr],
                     lru_lambda[e, dr], n_ctx=n_ctx, reverse=rev) for dr, rev in enumerate((False, True)))
            ret = _retention(p, cos_t, sin_t, ret_log_gamma[e, 0], ret_log_gamma[e, 1],
                             n_ctx=n_ctx, reverse=False, nb=nb)
            ret = _retention(p, cos_t, sin_t, ret_log_gamma[e, 1], ret, n_ctx=n_ctx, reverse=True, nb=nb)
            b_parts = (ret,)
            gate_cols = (1, 4)
        else:
            o = layer // 2
            n_main = 9 * GROUP
            w_in = od_w_in[o]
            w_tail = jnp.pad(w_in[:, n_main:], ((0, 0), (0, LANES - (w_in.shape[1] - n_main)))).astype(BF16)
            p, tail = _proj(xs, mod, norm1_g[layer], w_in[:, :n_main].astype(BF16), w_tail, tm=tm, n_ctx=n_ctx)
            qkv = _gdn_prep(p, gdn_conv_w[o], n_ctx=n_ctx)
            mix = []
            for dr, rev in enumerate((False, True)):
                g_col = (2 + dr) * LIN_HEADS
                a_row = jnp.zeros((1, LANES), F32).at[0, g_col:g_col + LIN_HEADS].set(gdn_a_log[o, dr])
                dt_row = jnp.zeros((1, LANES), F32).at[0, g_col:g_col + LIN_HEADS].set(gdn_dt_bias[o, dr])
                mix.append((
                    _gla(p, hg_lb_logits[dr], n_ctx=n_ctx, reverse=rev, layer=o, nb=nb),
                    _gdn(qkv, tail, a_row, dt_row, n_ctx=n_ctx, reverse=rev, direction=dr, nb=nb)))
            a_parts, b_parts = zip(*mix)
            gate_cols = (4, 8)
        xs = _post(xs, mod, a_parts, b_parts, p, gate_cols[0], gate_cols[1], w_out,
                   even=layer % 2 == 0, tm=tm, n_ctx=n_ctx)
        last = layer == depth - 1
        xs = _mlp(xs, mod, norm2_g[layer], mlp_w1[layer].astype(BF16), mlp_w2[layer].astype(BF16), final_g,
                  tm=_token_tile(seq) if last else tm, tf=tf, n_ctx=n_ctx, final=last)
    return xs
```

```python
import functools

import numpy as np
import jax
import jax.numpy as jnp
from jax import lax
from jax.experimental import pallas as pl
from jax.experimental.pallas import tpu as pltpu

F32 = jnp.float32
BF16 = jnp.bfloat16

EPS = 1e-6
GRID_W = 64
CONV_W = 4
CONV_LEFT = CONV_W // 2
LRU_BLOCKS = 8
LRU_C = 8.0
RET_HEADS = 4
RET_CHUNK = 256
ROPE_BASE = 10000.0
LIN_HEADS = 4
GLA_CHUNK = 128
LIN_CHUNK = 64
HEAD_DV = 128

LANES = 128
SUBLANES = 8
V7X_VMEM_BYTES = 64 * 1024 * 1024
VMEM_LIMIT = (V7X_VMEM_BYTES * 3) // 4
GROUP = 512

HALO = 2 * SUBLANES
ACT = jnp.bfloat16


def _cparams(*sem):
    return pltpu.CompilerParams(dimension_semantics=sem, vmem_limit_bytes=VMEM_LIMIT)


def _sigmoid(x):
    return 1.0 / (1.0 + jnp.exp(-x))


def _silu(x):
    return x * _sigmoid(x)


def _softplus(x):
    return jnp.maximum(x, 0.0) + jnp.log(1.0 + jnp.exp(-jnp.abs(x)))


def _gelu_tanh(x):
    return 0.5 * x * (1.0 + jnp.tanh(0.7978845608028654 * (x + 0.044715 * x * x * x)))


def _dot(a, b):
    return jnp.dot(a.astype(BF16), b.astype(BF16), preferred_element_type=F32)


def _dot_nt(a, b):
    return lax.dot_general(a.astype(BF16), b.astype(BF16), (((1,), (1,)), ((), ())),
                           preferred_element_type=F32)


def _split_bf16(x):
    hi = x.astype(BF16)
    lo = (x - hi.astype(F32)).astype(BF16)
    return hi, lo


def _dot_sel(m2, x):
    hi, lo = _split_bf16(x)
    return jnp.dot(m2, jnp.concatenate([hi, lo], axis=0), preferred_element_type=F32)


def _doubled(m):
    return jnp.asarray(np.concatenate([m, m], axis=1), BF16)


def _dot3(a, b):
    ah, al = _split_bf16(a)
    bh, bl = _split_bf16(b)
    return (jnp.dot(ah, bh, preferred_element_type=F32) + jnp.dot(ah, bl, preferred_element_type=F32)
            + jnp.dot(al, bh, preferred_element_type=F32))


def _row_select(t, tm, n_ctx, mc_ref, ml_ref, idx):
    rows = t * tm + lax.broadcasted_iota(jnp.int32, (tm, 1), 0)
    return jnp.where(rows < n_ctx, mc_ref[idx:idx + 1, :], ml_ref[idx:idx + 1, :])


def _rms(x, g):
    return x * lax.rsqrt(jnp.mean(x * x, axis=-1, keepdims=True) + EPS) * g


def _head_norm(o, center):
    outs = []
    for h in range(o.shape[-1] // HEAD_DV):
        oh = o[:, h * HEAD_DV:(h + 1) * HEAD_DV]
        if center:
            oh = oh - jnp.mean(oh, axis=-1, keepdims=True)
        outs.append(oh * lax.rsqrt(jnp.mean(oh * oh, axis=-1, keepdims=True) + EPS))
    return jnp.concatenate(outs, axis=-1)


def _scan_tile(step, n_first, n_total, reverse):
    if not reverse:
        return step
    return jnp.where(step < n_first, n_first - 1 - step, n_total - 1 - (step - n_first))


def _mods_body(s_ref, w_ref, b_ref, o_ref):
    s = _silu(s_ref[...])
    o_ref[...] = _dot3(s, w_ref[...]) + b_ref[...]


def _mods(c, c_ctx, ada_w, ada_b):
    depth, d, n6 = ada_w.shape
    bsz = c.shape[0]
    rows = -(-(bsz + 1) // SUBLANES) * SUBLANES
    s = jnp.zeros((rows, d), F32).at[:bsz].set(c).at[bsz].set(c_ctx)
    tn = 1024
    out = pl.pallas_call(
        _mods_body,
        grid=(depth, n6 // tn),
        in_specs=[pl.BlockSpec((rows, d), lambda l, n: (0, 0)),
                  pl.BlockSpec((None, d, tn), lambda l, n: (l, 0, n)),
                  pl.BlockSpec((None, 1, tn), lambda l, n: (l, 0, n))],
        out_specs=pl.BlockSpec((None, rows, tn), lambda l, n: (l, 0, n)),
        out_shape=jax.ShapeDtypeStruct((depth, rows, n6), F32),
        compiler_params=_cparams("parallel", "parallel"),
        name="mods",
    )(s, ada_w, ada_b.reshape(depth, 1, n6))
    return out.reshape(depth, rows, 6, d)


def _proj_body(*refs, tm, n_ctx, has_tail):
    if has_tail:
        x_ref, ml_ref, mc_ref, g_ref, w_ref, wt_ref, o_ref, ot_ref = refs
    else:
        x_ref, ml_ref, mc_ref, g_ref, w_ref, o_ref = refs
    t = pl.program_id(1)
    y = _rms(x_ref[...], g_ref[...])
    shift = _row_select(t, tm, n_ctx, mc_ref, ml_ref, 0)
    scale = _row_select(t, tm, n_ctx, mc_ref, ml_ref, 1)
    h = (y * (1.0 + scale) + shift).astype(BF16)
    o_ref[...] = jnp.dot(h, w_ref[...], preferred_element_type=F32).astype(o_ref.dtype)
    if has_tail:
        ot_ref[...] = jnp.dot(h, wt_ref[...], preferred_element_type=F32)


def _proj(x, mod, g, w, w_tail, *, tm, n_ctx):
    bsz, l, d = x.shape
    n_out = w.shape[1]
    ctx_row = bsz
    resident = dict(pipeline_mode=pl.Buffered(1))
    in_specs = [pl.BlockSpec((None, tm, d), lambda b, t: (b, t, 0)),
                pl.BlockSpec((None, 6, d), lambda b, t: (b, 0, 0)),
                pl.BlockSpec((None, 6, d), lambda b, t: (ctx_row, 0, 0)),
                pl.BlockSpec((1, d), lambda b, t: (0, 0)),
                pl.BlockSpec((d, n_out), lambda b, t: (0, 0), **resident)]
    out_specs = [pl.BlockSpec((None, tm, n_out), lambda b, t: (b, t, 0))]
    out_shape = [jax.ShapeDtypeStruct((bsz, l, n_out), ACT)]
    args = [x, mod, mod, g.reshape(1, d), w]
    if w_tail is not None:
        in_specs.append(pl.BlockSpec((d, LANES), lambda b, t: (0, 0), **resident))
        out_specs.append(pl.BlockSpec((None, tm, LANES), lambda b, t: (b, t, 0)))
        out_shape.append(jax.ShapeDtypeStruct((bsz, l, LANES), F32))
        args.append(w_tail)
    outs = pl.pallas_call(
        functools.partial(_proj_body, tm=tm, n_ctx=n_ctx, has_tail=w_tail is not None),
        grid=(bsz, l // tm),
        in_specs=in_specs, out_specs=out_specs, out_shape=out_shape,
        compiler_params=_cparams("parallel", "parallel"),
        name="proj",
    )(*args)
    return outs


def _lru_body(sm_ref, se_ref, x_ref, xp_ref, xn_ref, cw_ref, cb_ref, wa_ref, ba_ref, wx_ref, bx_ref, lam_ref,
              o_ref, a_scr, b_scr, h_scr, *, tl, n_ctx_tiles, n_tiles, reverse):
    step = pl.program_id(0)
    tile = _scan_tile(step, n_ctx_tiles, n_tiles, reverse)
    bsz = x_ref.shape[0]
    width = x_ref.shape[2]
    half = width // 2
    seg_first = jnp.logical_or(tile == 0, tile == n_ctx_tiles)
    seg_last = jnp.logical_or(tile == n_ctx_tiles - 1, tile == n_tiles - 1)
    keep_prev = jnp.where(seg_first, 0.0, 1.0)
    keep_next = jnp.where(seg_last, 0.0, 1.0)
    neg_c_sp = -LRU_C * _softplus(-lam_ref[...])

    @pl.when(step == 0)
    def _():
        h_scr[...] = jnp.zeros_like(h_scr)

    pair = 2 if bsz % 2 == 0 else 1

    def block_dot(ub, w_ref):
        return jnp.concatenate([jnp.dot(ub[:, :half], w_ref[0], preferred_element_type=F32),
                                jnp.dot(ub[:, half:], w_ref[1], preferred_element_type=F32)], axis=-1)

    def gates(bp, carry):
        bs = [bp * pair + n for n in range(pair)]
        taps = [_conv_taps(sm_ref, se_ref, x_ref[b], xp_ref[b], xn_ref[b], keep_prev, keep_next, tl) for b in bs]
        u = [cb_ref[...] + sum(cw_ref[k:k + 1, :] * tp[k] for k in range(CONV_W)) for tp in taps]
        ub = [x.astype(BF16) for x in u]
        r_pre = [block_dot(x, wa_ref) + ba_ref[...] for x in ub]
        i_pre = [block_dot(x, wx_ref) + bx_ref[...] for x in ub]
        a = [jnp.exp(neg_c_sp * _sigmoid(x)) for x in r_pre]
        for n, b in enumerate(bs):
            a_scr[b] = a[n]
            b_scr[b] = jnp.sqrt(1.0 - a[n] * a[n]) * (_sigmoid(i_pre[n]) * u[n])
        return carry

    lax.fori_loop(0, bsz // pair, gates, 0)

    def scan_rows(j, hs):
        idx = (tl - 1 - j) if reverse else j
        new = []
        for b in range(bsz):
            h = a_scr[b, pl.ds(idx, 1), :] * hs[b] + b_scr[b, pl.ds(idx, 1), :]
            b_scr[b, pl.ds(idx, 1), :] = h
            new.append(h)
        return tuple(new)

    hs = tuple(h_scr[b, 0:1, :] for b in range(bsz))
    hs = lax.fori_loop(0, tl, scan_rows, hs)
    for b in range(bsz):
        h_scr[b, 0:1, :] = hs[b]
    o_ref[...] = b_scr[...].astype(o_ref.dtype)


def _lru(p, conv_w, conv_b, wa, ba, wx, bx, lam, *, n_ctx, reverse, tl=256):
    bsz, l, _ = p.shape
    width = GROUP
    n_tiles = l // tl
    n_ctx_tiles = n_ctx // tl
    hb = tl // HALO
    n_hblocks = l // HALO

    def tile_of(s):
        return _scan_tile(s, n_ctx_tiles, n_tiles, reverse)

    vec = pl.BlockSpec((1, width), lambda s: (0, 0))
    mat = pl.BlockSpec((2, width // 2, width // 2), lambda s: (0, 0, 0))
    sm, se = (jnp.asarray(m, p.dtype) for m in _shift_matrices(tl))
    return pl.pallas_call(
        functools.partial(_lru_body, tl=tl, n_ctx_tiles=n_ctx_tiles, n_tiles=n_tiles, reverse=reverse),
        grid=(n_tiles,),
        in_specs=[pl.BlockSpec(sm.shape, lambda s: (0, 0)), pl.BlockSpec(se.shape, lambda s: (0, 0)),
                  pl.BlockSpec((bsz, tl, width), lambda s: (0, tile_of(s), 0)),
                  pl.BlockSpec((bsz, HALO, width), lambda s: (0, jnp.maximum(tile_of(s) * hb - 1, 0), 0)),
                  pl.BlockSpec((bsz, HALO, width),
                               lambda s: (0, jnp.minimum((tile_of(s) + 1) * hb, n_hblocks - 1), 0)),
                  pl.BlockSpec((CONV_W, width), lambda s: (0, 0)), vec, mat, vec, mat, vec, vec],
        out_specs=pl.BlockSpec((bsz, tl, width), lambda s: (0, tile_of(s), 0)),
        out_shape=jax.ShapeDtypeStruct((bsz, l, width), ACT),
        scratch_shapes=[pltpu.VMEM((bsz, tl, width), F32),
                        pltpu.VMEM((bsz, tl, width), F32),
                        pltpu.VMEM((bsz, SUBLANES, width), F32)],
        compiler_params=_cparams("arbitrary"),
        name="lru_rev" if reverse else "lru_fwd",
    )(sm, se, p, p, p, conv_w, conv_b.reshape(1, width), wa, ba.reshape(1, width), wx, bx.reshape(1, width),
      lam.reshape(1, width))


def _blockdiag_halves(w):
    nb, n, _ = w.shape
    per = nb // 2
    w = w.reshape(2, per, n, n)
    eye = jnp.eye(per, dtype=w.dtype)
    out = w[:, :, :, None, :] * eye[None, :, None, :, None]
    return out.reshape(2, per * n, per * n)


def _ret_body(*refs, reverse, dk, nb):
    if reverse:
        lg_ref, q_ref, k_ref, v_ref, cos_ref, sin_ref, acc_ref, o_ref, qd_scr, kd_scr, cd_scr, st_scr = refs
    else:
        lg_ref, lgo_ref, q_ref, k_ref, v_ref, cos_ref, sin_ref, o_ref, dec_scr, qd_scr, kd_scr, cd_scr, st_scr = refs
    c = RET_CHUNK
    step = pl.program_id(1)

    @pl.when(step == 0)
    def _():
        st_scr[...] = jnp.zeros_like(st_scr)
        row = lax.broadcasted_iota(jnp.int32, (c, HEAD_DV), 0)
        pos = ((c - 1 - row) if reverse else row).astype(F32)
        for h in range(RET_HEADS):
            lg = lg_ref[h]
            qd_scr[h] = jnp.exp(lg * (pos + 1.0))
            kd_scr[h] = jnp.exp(lg * (c - 1.0 - pos))
            cd_scr[h] = jnp.exp(jnp.full((SUBLANES, LANES), c, F32) * lg)
        if not reverse:
            i = lax.broadcasted_iota(jnp.int32, (c, c), 0)
            j = lax.broadcasted_iota(jnp.int32, (c, c), 1)
            below = jnp.maximum(i - j, 0).astype(F32)
            above = jnp.maximum(j - i, 0).astype(F32)
            for h in range(RET_HEADS):
                dec_scr[h] = (jnp.where(i >= j, jnp.exp(lg_ref[h] * below), 0.0)
                              + jnp.where(j >= i, jnp.exp(lgo_ref[h] * above), 0.0))

    def rope(x):
        width = x.shape[-1]
        lane = lax.broadcasted_iota(jnp.int32, x.shape, 1)
        first = (lane & (dk - 1)) < dk // 2
        rot = jnp.where(first, pltpu.roll(x, width - dk // 2, axis=1), pltpu.roll(x, dk // 2, axis=1))
        return x * cos_ref[...] + rot * sin_ref[...]

    q = [rope(q_ref[bb].astype(F32)) for bb in range(nb)]
    k = [rope(k_ref[bb].astype(F32)) * (dk ** -0.5) for bb in range(nb)]
    pairs = [(bb, h) for bb in range(nb) for h in range(RET_HEADS)]
    qh = [q[bb][:, h * dk:(h + 1) * dk] for bb, h in pairs]
    kh = [k[bb][:, h * dk:(h + 1) * dk] for bb, h in pairs]
    vh = [v_ref[bb, :, h * HEAD_DV:(h + 1) * HEAD_DV] for bb, h in pairs]
    state = [st_scr[bb, h] for bb, h in pairs]
    o_x = [_dot(qh[i], state[i]) * qd_scr[h] for i, (bb, h) in enumerate(pairs)]
    if reverse:
        o_in = [acc_ref[bb, :, h * HEAD_DV:(h + 1) * HEAD_DV].astype(F32) for bb, h in pairs]
    else:
        s = [_dot_nt(qh[i], kh[i]) * dec_scr[h] for i, (bb, h) in enumerate(pairs)]
        o_in = [_dot(s[i], vh[i]) for i in range(len(pairs))]
    kv = [_dot((kh[i] * kd_scr[h][:, :dk]).T, vh[i]) for i, (bb, h) in enumerate(pairs)]
    for i, (bb, h) in enumerate(pairs):
        st_scr[bb, h] = state[i] * cd_scr[h][0:1, :] + kv[i]
        o_ref[bb, :, h * HEAD_DV:(h + 1) * HEAD_DV] = (o_in[i] + o_x[i]).astype(o_ref.dtype)


def _retention(p, cos_t, sin_t, log_gamma, other, *, n_ctx, reverse, nb):
    bsz, l, _ = p.shape
    c = RET_CHUNK
    qk_w = cos_t.shape[1]
    dk = qk_w // RET_HEADS
    n_chunks = l // c
    n_ctx_chunks = n_ctx // c

    def ch(s):
        return _scan_tile(s, n_ctx_chunks, n_chunks, reverse)

    q_col = (2 * GROUP) // qk_w
    smem = pl.BlockSpec(memory_space=pltpu.SMEM)
    out_block = pl.BlockSpec((nb, c, GROUP), lambda b, s: (b, ch(s), 0))
    data_specs = [pl.BlockSpec((nb, c, qk_w), lambda b, s: (b, ch(s), q_col)),
                  pl.BlockSpec((nb, c, qk_w), lambda b, s: (b, ch(s), q_col + 1)),
                  pl.BlockSpec((nb, c, GROUP), lambda b, s: (b, ch(s), 3)),
                  pl.BlockSpec((c, qk_w), lambda b, s: (ch(s), 0)),
                  pl.BlockSpec((c, qk_w), lambda b, s: (ch(s), 0))]
    scratch = [pltpu.VMEM((RET_HEADS, c, HEAD_DV), F32),
               pltpu.VMEM((RET_HEADS, c, HEAD_DV), F32),
               pltpu.VMEM((RET_HEADS, SUBLANES, LANES), F32),
               pltpu.VMEM((nb, RET_HEADS, dk, HEAD_DV), F32)]
    if reverse:
        in_specs = [smem] + data_specs + [out_block]
        args = (log_gamma, p, p, p, cos_t, sin_t, other)
    else:
        in_specs = [smem, smem] + data_specs
        args = (log_gamma, other, p, p, p, cos_t, sin_t)
        scratch = [pltpu.VMEM((RET_HEADS, c, c), F32)] + scratch
    return pl.pallas_call(
        functools.partial(_ret_body, reverse=reverse, dk=dk, nb=nb),
        grid=(bsz // nb, n_chunks),
        in_specs=in_specs,
        out_specs=out_block,
        out_shape=jax.ShapeDtypeStruct((bsz, l, GROUP), ACT),
        scratch_shapes=scratch,
        compiler_params=_cparams("parallel", "arbitrary"),
        name="ret_rev" if reverse else "ret_fwd",
    )(*args)


def _rope_tables(n_ctx, seq, dk):
    n_freq = dk // 4
    inv = jnp.power(ROPE_BASE, -jnp.arange(n_freq, dtype=F32) / n_freq)
    rows = seq // GRID_W
    r = jnp.arange(rows, dtype=F32)
    col = jnp.arange(GRID_W, dtype=F32)
    row_ang = jnp.broadcast_to(r[:, None, None] * inv, (rows, GRID_W, n_freq))
    col_ang = jnp.broadcast_to(col[None, :, None] * inv, (rows, GRID_W, n_freq))
    ang = jnp.concatenate([row_ang, col_ang], axis=-1).reshape(rows * GRID_W, 2 * n_freq)
    cos, sin = jnp.cos(ang), jnp.sin(ang)
    cos_h = jnp.concatenate([cos, cos], axis=-1)
    sin_h = jnp.concatenate([-sin, sin], axis=-1)
    cos_t = jnp.concatenate([jnp.ones((n_ctx, dk), F32), cos_h], axis=0)
    sin_t = jnp.concatenate([jnp.zeros((n_ctx, dk), F32), sin_h], axis=0)
    return jnp.tile(cos_t, (1, RET_HEADS)), jnp.tile(sin_t, (1, RET_HEADS))


_GLA_LEVELS = tuple(GLA_CHUNK >> (i + 1) for i in range(GLA_CHUNK.bit_length() - 1))


def _gla_matrices(reverse):
    c = GLA_CHUNK
    p = np.arange(c)
    mats = [(p[None, :] <= p[:, None]), (p[None, :] > p[:, None])]
    for s in _GLA_LEVELS:
        bs = (p // s) * s
        right = ((p // s) % 2 == 1)
        m_r = (p[None, :] > bs[:, None]) & (p[None, :] <= p[:, None])
        m_l = (p[None, :] > p[:, None]) & (p[None, :] <= (bs + s)[:, None])
        mats.append(np.where(right[:, None], m_r, m_l))
    m = np.stack(mats).astype(np.float32)
    if reverse:
        m = m[:, ::-1, ::-1]
    return m.reshape(len(mats) * c, c)


def _gla_body(m_ref, lb_ref, q_ref, f_ref, v_ref, o_ref, st_scr, *, reverse, layer, nb):
    c = GLA_CHUNK
    step = pl.program_id(1)

    @pl.when(step == 0)
    def _():
        st_scr[...] = jnp.zeros_like(st_scr)

    lg = lb_ref[...]
    e = jnp.exp(lg - jnp.max(lg, axis=0, keepdims=True))
    sm = e / jnp.sum(e, axis=0, keepdims=True)
    lb = jnp.sum(sm[0:layer + 1, :], axis=0, keepdims=True) - sm[0:1, :]

    ri = lax.broadcasted_iota(jnp.int32, (c, 1), 0)
    i2 = lax.broadcasted_iota(jnp.int32, (c, c), 0)
    j2 = lax.broadcasted_iota(jnp.int32, (c, c), 1)
    if reverse:
        ri, i2, j2 = c - 1 - ri, c - 1 - i2, c - 1 - j2
    last = 0 if reverse else c - 1
    width = q_ref.shape[-1]
    nh = width // HEAD_DV

    rows_bb = list(range(nb))
    pairs = [(bb, h) for bb in rows_bb for h in range(nh)]

    def head(x, h):
        return x[:, h * HEAD_DV:(h + 1) * HEAD_DV]

    f = [lb + (1.0 - lb) * _sigmoid(f_ref[bb].astype(F32)) for bb in rows_bb]
    k = [1.0 - x for x in f]
    q = [_silu(q_ref[bb].astype(F32)) for bb in rows_bb]
    ex = [_dot_sel(m_ref[...], jnp.log(x)) for x in f]
    a_last = [jnp.exp(x[last:last + 1, :]) for x in ex]
    qb = [x.astype(BF16) for x in q]
    kb = [x.astype(BF16) for x in k]
    qa = [qb[bb] * jnp.exp(ex[bb][0:c, :].astype(BF16)) for bb in rows_bb]
    kl = [kb[bb] * jnp.exp(ex[bb][c:2 * c, :].astype(BF16)) for bb in rows_bb]
    scores = [jnp.where(i2 == j2, _dot_nt(head(qb[bb], h), head(kb[bb], h)), 0.0) for bb, h in pairs]
    for lvl, s in enumerate(_GLA_LEVELS):
        sh = s.bit_length() - 1
        right = jnp.broadcast_to(((ri >> sh) & 1).astype(F32), (c, width)).astype(BF16) > 0
        x = [jnp.where(right, qb[bb], kb[bb]) * jnp.exp(ex[bb][(2 + lvl) * c:(3 + lvl) * c, :].astype(BF16))
             for bb in rows_bb]
        bi, bj = i2 >> sh, j2 >> sh
        valid = bj == jnp.where((bi & 1) == 1, bi - 1, -1)
        gram = [_dot_nt(head(x[bb], h), head(x[bb], h)) for bb, h in pairs]
        scores = [jnp.where(valid, gm, sc) for sc, gm in zip(scores, gram)]

    state_t = [st_scr[bb, h] for bb, h in pairs]
    o_intra = [_dot(scores[i], head(v_ref[bb], h)) for i, (bb, h) in enumerate(pairs)]
    o_inter = [_dot_nt(head(qa[bb], h), state_t[i]) for i, (bb, h) in enumerate(pairs)]
    kv = [_dot(head(v_ref[bb], h).astype(F32).T, head(kl[bb], h)) for bb, h in pairs]
    for i, (bb, h) in enumerate(pairs):
        o_ref[bb, :, h * HEAD_DV:(h + 1) * HEAD_DV] = (o_intra[i] + o_inter[i]).astype(o_ref.dtype)
        st_scr[bb, h] = state_t[i] * head(a_last[bb], h) + kv[i]


def _gla(p, lb_logits, *, n_ctx, reverse, layer, nb):
    bsz, l, _ = p.shape
    c = GLA_CHUNK
    n_chunks = l // c
    n_ctx_chunks = n_ctx // c
    m = _doubled(_gla_matrices(reverse))

    def ch(s):
        return _scan_tile(s, n_ctx_chunks, n_chunks, reverse)

    f_col = 2 if reverse else 1
    return pl.pallas_call(
        functools.partial(_gla_body, reverse=reverse, layer=layer, nb=nb),
        grid=(bsz // nb, n_chunks),
        in_specs=[pl.BlockSpec(m.shape, lambda b, s: (0, 0)),
                  pl.BlockSpec(lb_logits.shape, lambda b, s: (0, 0)),
                  pl.BlockSpec((nb, c, GROUP), lambda b, s: (b, ch(s), 0)),
                  pl.BlockSpec((nb, c, GROUP), lambda b, s: (b, ch(s), f_col)),
                  pl.BlockSpec((nb, c, GROUP), lambda b, s: (b, ch(s), 3))],
        out_specs=pl.BlockSpec((nb, c, GROUP), lambda b, s: (b, ch(s), 0)),
        out_shape=jax.ShapeDtypeStruct((bsz, l, GROUP), ACT),
        scratch_shapes=[pltpu.VMEM((nb, LIN_HEADS, HEAD_DV, HEAD_DV), F32)],
        compiler_params=_cparams("parallel", "arbitrary"),
        name="gla_rev" if reverse else "gla_fwd",
    )(m, lb_logits, p, p, p)


def _shift_matrices(rows):
    offs = [t - CONV_LEFT for t in range(CONV_W) if t != CONV_LEFT]
    main = np.zeros((len(offs) * rows, rows), np.float32)
    edge = np.zeros((len(offs) * SUBLANES, 2 * HALO), np.float32)
    for n, off in enumerate(offs):
        for i in range(rows):
            j = i + off
            if 0 <= j < rows:
                main[n * rows + i, j] = 1.0
            elif j < 0:
                edge[n * SUBLANES + i, HALO + j] = 1.0
            else:
                edge[n * SUBLANES + i - (rows - SUBLANES), HALO + j - rows] = 1.0
    return main, edge


def _conv_taps(sm_ref, se_ref, x, xp, xn, keep_prev, keep_next, rows):
    sh = jnp.dot(sm_ref[...], x, preferred_element_type=F32)
    halo = jnp.concatenate([xp * keep_prev.astype(xp.dtype), xn * keep_next.astype(xn.dtype)], axis=0)
    fix = jnp.dot(se_ref[...], halo, preferred_element_type=F32)
    taps, n = [], 0
    for t in range(CONV_W):
        off = t - CONV_LEFT
        if off == 0:
            taps.append(x.astype(F32))
            continue
        s = sh[n * rows:(n + 1) * rows, :]
        f = fix[n * SUBLANES:(n + 1) * SUBLANES, :]
        if off < 0:
            s = jnp.concatenate([s[0:SUBLANES] + f, s[SUBLANES:]], axis=0)
        else:
            s = jnp.concatenate([s[:rows - SUBLANES], s[rows - SUBLANES:] + f], axis=0)
        taps.append(s)
        n += 1
    return taps


def _gdn_prep_body(sm_ref, se_ref, cw_ref, q_ref, qp_ref, qn_ref, k_ref, kp_ref, kn_ref, v_ref, vp_ref, vn_ref,
                   qo_ref, ko_ref, vo_ref, *, tp, n_ctx_tiles, n_tiles):
    c = LIN_CHUNK
    nh = LIN_HEADS
    tile = pl.program_id(1)
    seg_first = jnp.logical_or(tile == 0, tile == n_ctx_tiles)
    seg_last = jnp.logical_or(tile == n_ctx_tiles - 1, tile == n_tiles - 1)
    keep_prev = jnp.where(seg_first, 0.0, 1.0)
    keep_next = jnp.where(seg_last, 0.0, 1.0)

    all_taps = [_conv_taps(sm_ref, se_ref, x[...], xp[...], xn[...], keep_prev, keep_next, tp)
                for x, xp, xn in ((q_ref, qp_ref, qn_ref), (k_ref, kp_ref, kn_ref), (v_ref, vp_ref, vn_ref))]

    def conv_silu(col):
        taps = all_taps[col]
        u = cw_ref[0:1, col * GROUP:(col + 1) * GROUP] * taps[0]
        for t in range(1, CONV_W):
            u = u + cw_ref[t:t + 1, col * GROUP:(col + 1) * GROUP] * taps[t]
        return _silu(u)

    def l2n(x):
        return x * lax.rsqrt(jnp.sum(x * x, axis=-1, keepdims=True) + EPS)

    def emit(out_ref, x, norm, scale):
        for h in range(nh):
            xh = x[:, h * HEAD_DV:(h + 1) * HEAD_DV]
            if norm:
                xh = l2n(xh) * scale
            for j in range(tp // c):
                out_ref[j, h * c:(h + 1) * c, :] = xh[j * c:(j + 1) * c, :].astype(out_ref.dtype)

    emit(qo_ref, conv_silu(0), True, HEAD_DV ** -0.5)
    emit(ko_ref, conv_silu(1), True, 1.0)
    emit(vo_ref, conv_silu(2), False, 1.0)


def _gdn_prep(p, conv_w, *, n_ctx, tp=256):
    bsz, l, _ = p.shape
    c = LIN_CHUNK
    n_tiles = l // tp
    hb = tp // HALO
    n_hblocks = l // HALO

    def triple(col):
        return [pl.BlockSpec((None, tp, GROUP), lambda b, t: (b, t, col)),
                pl.BlockSpec((None, HALO, GROUP), lambda b, t: (b, jnp.maximum(t * hb - 1, 0), col)),
                pl.BlockSpec((None, HALO, GROUP), lambda b, t: (b, jnp.minimum((t + 1) * hb, n_hblocks - 1), col))]

    out_spec = pl.BlockSpec((None, tp // c, LIN_HEADS * c, HEAD_DV), lambda b, t: (b, t, 0, 0))
    out_sds = jax.ShapeDtypeStruct((bsz, l // c, LIN_HEADS * c, HEAD_DV), ACT)
    sm, se = (jnp.asarray(m, p.dtype) for m in _shift_matrices(tp))
    const2 = lambda b, t: (0, 0)
    return pl.pallas_call(
        functools.partial(_gdn_prep_body, tp=tp, n_ctx_tiles=n_ctx // tp, n_tiles=n_tiles),
        grid=(bsz, n_tiles),
        in_specs=[pl.BlockSpec(sm.shape, const2), pl.BlockSpec(se.shape, const2), pl.BlockSpec(conv_w.shape, const2)]
                 + triple(5) + triple(6) + triple(7),
        out_specs=[out_spec] * 3, out_shape=[out_sds] * 3,
        compiler_params=_cparams("parallel", "parallel"),
        name="gdn_prep",
    )(sm, se, conv_w, *([p] * 9))


def _gdn_body(cs_ref, arow_ref, dtrow_ref, t_ref, q_ref, k_ref, v_ref, o_ref, st_scr,
              *, reverse, beta_col, g_col, nb):
    c = LIN_CHUNK
    nh = LIN_HEADS
    n = nh * c
    step = pl.program_id(1)

    @pl.when(step == 0)
    def _():
        st_scr[...] = jnp.zeros_like(st_scr)

    i2 = lax.broadcasted_iota(jnp.int32, (c, n), 0)
    lane = lax.broadcasted_iota(jnp.int32, (c, n), 1)
    j2 = lane & (c - 1)
    grp = lane >> 6
    mask = (j2 >= i2) if reverse else (j2 <= i2)
    same32 = (i2 >> 5) == (j2 >> 5)
    same16 = (i2 >> 4) == (j2 >> 4)
    eye = jnp.where(i2 == j2, 1.0, 0.0)
    grp_rows = [jnp.where((lax.broadcasted_iota(jnp.int32, (1, n), 1) >> 6) == h, 1.0, 0.0).astype(BF16)
                for h in range(nh)]
    last = 0 if reverse else c - 1

    def each(fn, *lists):
        return [fn(*args) for args in zip(*lists)]

    def head_bcast(x, col):
        return jnp.concatenate([jnp.broadcast_to(x[:, col + h:col + h + 1], (c, HEAD_DV)) for h in range(nh)], axis=0)

    def side_by_side(x):
        out = x[(nh - 1) * c:nh * c, :]
        for h in range(nh - 2, -1, -1):
            out = jnp.where(grp == h, x[h * c:(h + 1) * c, :], out)
        return out

    def block_diag(x):
        xb = x.astype(BF16)
        return jnp.concatenate([xb * grp_rows[h] for h in range(nh)], axis=0)

    def dot_sbs(x, y_bd):
        return jnp.dot(x.astype(BF16), y_bd, preferred_element_type=F32)

    rows_bb = list(range(nb))
    q = [q_ref[bb].astype(F32) for bb in rows_bb]
    k = [k_ref[bb].astype(F32) for bb in rows_bb]
    v = [v_ref[bb].astype(F32) for bb in rows_bb]

    tail = [t_ref[bb] for bb in rows_bb]
    beta = [head_bcast(_sigmoid(x), beta_col) for x in tail]
    g = [head_bcast(-jnp.exp(arow_ref[...]) * _softplus(x + dtrow_ref[...]), g_col) for x in tail]
    gc = [_dot_sel(cs_ref[...], x) for x in g]
    gc_last = [jnp.concatenate([jnp.broadcast_to(x[h * c + last:h * c + last + 1, :], (c, HEAD_DV))
                                for h in range(nh)], axis=0) for x in gc]
    diff = [side_by_side(jnp.concatenate([x, x], axis=1)) - jnp.transpose(x)[0:1, :] for x in gc]
    gam = [jnp.where(mask, jnp.exp(jnp.where(mask, x, 0.0)), 0.0) for x in diff]

    kb = each(lambda a, b: a * b, k, beta)
    kq = each(lambda a, b, kx: _dot_nt(jnp.concatenate([a, b], axis=0), kx), kb, q, k)
    mm = each(lambda x, gm: jnp.where(i2 == j2, 0.0, side_by_side(x[:n]) * gm), kq, gam)
    qk = each(lambda x, gm: side_by_side(x[n:]) * gm, kq, gam)
    egc = [jnp.exp(x) for x in gc]
    rhs = each(lambda a, b, kbx, e: jnp.concatenate([a * b, kbx * e], axis=1), v, beta, kb, egc)

    md = [jnp.where(same16, x, 0.0) for x in mm]
    m2 = each(lambda a: dot_sbs(a, block_diag(a)), md)
    t = [eye - x for x in md]
    tm = each(lambda a, b: dot_sbs(jnp.concatenate([a, b], axis=0), block_diag(b)), t, m2)
    t = each(lambda a, x: a + x[:c], t, tm)
    m4 = [x[c:] for x in tm]
    tm = each(lambda a, b: dot_sbs(jnp.concatenate([a, b], axis=0), block_diag(b)), t, m4)
    t = each(lambda a, x: a + x[:c], t, tm)
    t = each(lambda a, x: a + dot_sbs(a, block_diag(x[c:])), t, tm)
    c32 = each(lambda a, b: jnp.where(same32, a - b, 0.0), mm, md)
    ct = each(lambda a, b: dot_sbs(a, block_diag(b)), c32, t)
    t = each(lambda a, b: a - dot_sbs(a, block_diag(b)), t, ct)
    c64 = [jnp.where(same32, 0.0, x) for x in mm]
    ct = each(lambda a, b: dot_sbs(a, block_diag(b)), c64, t)
    t = each(lambda a, b: a - dot_sbs(a, block_diag(b)), t, ct)
    sol = each(lambda a, b: jnp.dot(block_diag(a), b.astype(BF16), preferred_element_type=F32), t, rhs)

    qd = each(lambda a, e: a * e, q, egc)
    kd = each(lambda a, gl, x: a * jnp.exp(gl - x), k, gc_last, gc)
    g_last = [jnp.exp(x) for x in gc_last]
    heads = list(range(nh))
    state = [[st_scr[bb, h] for h in heads] for bb in rows_bb]
    ws = [[_dot(jnp.concatenate([sol[bb][h * c:(h + 1) * c, HEAD_DV:], qd[bb][h * c:(h + 1) * c]], axis=0),
                state[bb][h]) for h in heads] for bb in rows_bb]
    v_new = [jnp.concatenate([sol[bb][h * c:(h + 1) * c, :HEAD_DV] - ws[bb][h][:c] for h in heads], axis=0)
             for bb in rows_bb]
    o_intra = each(lambda a, b: jnp.dot(block_diag(a), b.astype(BF16), preferred_element_type=F32), qk, v_new)
    for h in heads:
        rows = slice(h * c, (h + 1) * c)
        for bb in rows_bb:
            o_ref[bb, :, h * HEAD_DV:(h + 1) * HEAD_DV] = (o_intra[bb][rows] + ws[bb][h][c:]).astype(o_ref.dtype)
            st_scr[bb, h] = (state[bb][h] * g_last[bb][h * c:h * c + 1, :]
                             + _dot(kd[bb][rows].T, v_new[bb][rows]))


def _gdn_cumsum_matrix(reverse):
    c, nh = LIN_CHUNK, LIN_HEADS
    p = np.arange(c)
    tri = (p[None, :] >= p[:, None]) if reverse else (p[None, :] <= p[:, None])
    return np.kron(np.eye(nh), tri.astype(np.float32))


def _gdn(qkv, tail, a_row, dt_row, *, n_ctx, reverse, direction, nb):
    q, k, v = qkv
    bsz, n_chunks, n, _ = q.shape
    c = LIN_CHUNK
    n_ctx_chunks = n_ctx // c
    cs = _doubled(_gdn_cumsum_matrix(reverse))

    def ch(s):
        return _scan_tile(s, n_ctx_chunks, n_chunks, reverse)

    const2 = lambda b, s: (0, 0)
    stacked = pl.BlockSpec((nb, None, n, HEAD_DV), lambda b, s: (b, ch(s), 0, 0))
    return pl.pallas_call(
        functools.partial(_gdn_body, reverse=reverse, beta_col=direction * LIN_HEADS,
                          g_col=(2 + direction) * LIN_HEADS, nb=nb),
        grid=(bsz // nb, n_chunks),
        in_specs=[pl.BlockSpec(cs.shape, const2),
                  pl.BlockSpec((1, LANES), const2), pl.BlockSpec((1, LANES), const2),
                  pl.BlockSpec((nb, c, LANES), lambda b, s: (b, ch(s), 0)), stacked, stacked, stacked],
        out_specs=pl.BlockSpec((nb, c, GROUP), lambda b, s: (b, ch(s), 0)),
        out_shape=jax.ShapeDtypeStruct((bsz, n_chunks * c, GROUP), ACT),
        scratch_shapes=[pltpu.VMEM((nb, LIN_HEADS, HEAD_DV, HEAD_DV), F32)],
        compiler_params=_cparams("parallel", "arbitrary"),
        name="gdn_rev" if reverse else "gdn_fwd",
    )(cs, a_row, dt_row, tail, q, k, v)


def _post_body(*refs, even, tm, n_ctx, n_a, n_b):
    x_ref = refs[0]
    a_refs = refs[1:1 + n_a]
    ga_ref = refs[1 + n_a]
    b_refs = refs[2 + n_a:2 + n_a + n_b]
    gb_ref, w_ref, ml_ref, mc_ref, o_ref = refs[2 + n_a + n_b:]
    t = pl.program_id(1)
    a = sum(r[...].astype(F32) for r in a_refs)
    bsum = sum(r[...].astype(F32) for r in b_refs)
    if even:
        ya = a * _gelu_tanh(ga_ref[...].astype(F32))
        yb = _head_norm(bsum, True) * _silu(gb_ref[...].astype(F32))
    else:
        ya = _head_norm(a, False) * _silu(ga_ref[...].astype(F32))
        yb = _head_norm(bsum, False) * _silu(gb_ref[...].astype(F32))
    half = w_ref.shape[0] // 2
    mix = jnp.dot(ya.astype(BF16), w_ref[0:half, :], preferred_element_type=F32)
    mix = mix + jnp.dot(yb.astype(BF16), w_ref[half:, :], preferred_element_type=F32)
    o_ref[...] = x_ref[...] + _row_select(t, tm, n_ctx, mc_ref, ml_ref, 2) * mix


def _post(x, mod, a_parts, b_parts, p, gate_a_col, gate_b_col, w_out, *, even, tm, n_ctx):
    bsz, l, d = x.shape
    ctx_row = bsz
    tok = lambda b, t: (b, t, 0)
    grp = pl.BlockSpec((None, tm, GROUP), tok)
    return pl.pallas_call(
        functools.partial(_post_body, even=even, tm=tm, n_ctx=n_ctx, n_a=len(a_parts), n_b=len(b_parts)),
        grid=(bsz, l // tm),
        in_specs=[pl.BlockSpec((None, tm, d), tok)] + [grp] * len(a_parts)
                 + [pl.BlockSpec((None, tm, GROUP), lambda b, t: (b, t, gate_a_col))] + [grp] * len(b_parts)
                 + [pl.BlockSpec((None, tm, GROUP), lambda b, t: (b, t, gate_b_col)),
                    pl.BlockSpec(w_out.shape, lambda b, t: (0, 0)),
                    pl.BlockSpec((None, 6, d), lambda b, t: (b, 0, 0)),
                    pl.BlockSpec((None, 6, d), lambda b, t: (ctx_row, 0, 0))],
        out_specs=pl.BlockSpec((None, tm, d), tok),
        out_shape=jax.ShapeDtypeStruct((bsz, l, d), F32),
        compiler_params=_cparams("parallel", "parallel"),
        name="post",
    )(x, *a_parts, p, *b_parts, p, w_out, mod, mod)


def _mlp_body(x_ref, g_ref, ml_ref, mc_ref, w1_ref, w2_ref, fg_ref, o_ref, h_scr, acc_scr, *, tm, n_ctx, final):
    t = pl.program_id(1)
    f = pl.program_id(2)
    if final:
        x_ref = x_ref.at[0]

    @pl.when(f == 0)
    def _():
        y = _rms(x_ref[...], g_ref[...])
        shift = _row_select(t, tm, n_ctx, mc_ref, ml_ref, 3)
        scale = _row_select(t, tm, n_ctx, mc_ref, ml_ref, 4)
        h_scr[...] = (y * (1.0 + scale) + shift).astype(BF16)
        acc_scr[...] = jnp.zeros_like(acc_scr)

    hid = jnp.maximum(jnp.dot(h_scr[...], w1_ref[...], preferred_element_type=F32), 0.0)
    acc_scr[...] += jnp.dot((hid * hid).astype(BF16), w2_ref[...], preferred_element_type=F32)

    @pl.when(f == pl.num_programs(2) - 1)
    def _():
        y = x_ref[...] + _row_select(t, tm, n_ctx, mc_ref, ml_ref, 5) * acc_scr[...]
        if final:
            y = _rms(y, fg_ref[...])
        o_ref[...] = y


def _mlp(x, mod, g, w1, w2, final_g, *, tm, tf, n_ctx, final):
    bsz, l, d = x.shape
    dff = w1.shape[1]
    ctx_row = bsz
    tok = lambda b, t, f: (b, t, 0)
    if final:
        l = l - n_ctx
        x_spec = pl.BlockSpec((pl.Element(1), pl.Element(tm), pl.Element(d)),
                              lambda b, t, f, off=n_ctx: (b, pl.multiple_of(off + t * tm, SUBLANES), 0))
        n_ctx = 0
    else:
        x_spec = pl.BlockSpec((None, tm, d), tok)
    return pl.pallas_call(
        functools.partial(_mlp_body, tm=tm, n_ctx=n_ctx, final=final),
        grid=(bsz, l // tm, dff // tf),
        in_specs=[x_spec,
                  pl.BlockSpec((1, d), lambda b, t, f: (0, 0)),
                  pl.BlockSpec((None, 6, d), lambda b, t, f: (b, 0, 0)),
                  pl.BlockSpec((None, 6, d), lambda b, t, f: (ctx_row, 0, 0)),
                  pl.BlockSpec((d, tf), lambda b, t, f: (0, f)),
                  pl.BlockSpec((tf, d), lambda b, t, f: (f, 0)),
                  pl.BlockSpec((1, d), lambda b, t, f: (0, 0))],
        out_specs=pl.BlockSpec((None, tm, d), tok),
        out_shape=jax.ShapeDtypeStruct((bsz, l, d), F32),
        scratch_shapes=[pltpu.VMEM((tm, d), BF16), pltpu.VMEM((tm, d), F32)],
        compiler_params=_cparams("parallel", "parallel", "arbitrary"),
        name="mlp",
    )(x, g.reshape(1, d), mod, mod, w1, w2, final_g.reshape(1, d))


def _mlp_res_body(x_ref, g_ref, ml_ref, mc_ref, w1_ref, w2_ref, fg_ref, o_ref, hid_scr, *, tm, tf, n_ctx, final):
    t = pl.program_id(1)
    if final:
        x_ref = x_ref.at[0]
    y = _rms(x_ref[...], g_ref[...])
    shift = _row_select(t, tm, n_ctx, mc_ref, ml_ref, 3)
    scale = _row_select(t, tm, n_ctx, mc_ref, ml_ref, 4)
    h = (y * (1.0 + scale) + shift).astype(BF16)
    for f in range(w1_ref.shape[1] // tf):
        hid = jnp.maximum(jnp.dot(h, w1_ref[:, f * tf:(f + 1) * tf], preferred_element_type=F32), 0.0)
        hid_scr[:, f * tf:(f + 1) * tf] = (hid * hid).astype(BF16)
    acc = jnp.dot(hid_scr[...], w2_ref[...], preferred_element_type=F32)
    y = x_ref[...] + _row_select(t, tm, n_ctx, mc_ref, ml_ref, 5) * acc
    if final:
        y = _rms(y, fg_ref[...])
    o_ref[...] = y


def _mlp_res(x, mod, g, w1, w2, final_g, *, tm, tf, n_ctx, final):
    bsz, l, d = x.shape
    dff = w1.shape[1]
    ctx_row = bsz
    tok = lambda b, t: (b, t, 0)
    if final:
        l = l - n_ctx
        x_spec = pl.BlockSpec((pl.Element(1), pl.Element(tm), pl.Element(d)),
                              lambda b, t, off=n_ctx: (b, pl.multiple_of(off + t * tm, SUBLANES), 0))
        n_ctx = 0
    else:
        x_spec = pl.BlockSpec((None, tm, d), tok)
    resident = dict(pipeline_mode=pl.Buffered(1))
    return pl.pallas_call(
        functools.partial(_mlp_res_body, tm=tm, tf=tf, n_ctx=n_ctx, final=final),
        grid=(bsz, l // tm),
        in_specs=[x_spec,
                  pl.BlockSpec((1, d), lambda b, t: (0, 0)),
                  pl.BlockSpec((None, 6, d), lambda b, t: (b, 0, 0)),
                  pl.BlockSpec((None, 6, d), lambda b, t: (ctx_row, 0, 0)),
                  pl.BlockSpec((d, dff), lambda b, t: (0, 0), **resident),
                  pl.BlockSpec((dff, d), lambda b, t: (0, 0), **resident),
                  pl.BlockSpec((1, d), lambda b, t: (0, 0))],
        out_specs=pl.BlockSpec((None, tm, d), tok),
        out_shape=jax.ShapeDtypeStruct((bsz, l, d), F32),
        scratch_shapes=[pltpu.VMEM((tm, dff), BF16)],
        compiler_params=_cparams("parallel", "parallel"),
        name="mlp",
    )(x, g.reshape(1, d), mod, mod, w1, w2, final_g.reshape(1, d))


def _token_tile(l):
    for div in (4, 8, 16, 17, 34):
        if l % div == 0 and (l // div) % 16 == 0 and l // div <= 1088:
            return l // div
    return l


def kernel(x, c, ctx, c_ctx, ada_w, ada_b, norm1_g, norm2_g, mix_w_out, mlp_w1, mlp_w2, ev_w_in, lru_conv_w,
           lru_conv_b, lru_wa, lru_ba, lru_wx, lru_bx, lru_lambda, ret_log_gamma, od_w_in, hg_lb_logits,
           gdn_conv_w, gdn_a_log, gdn_dt_bias, final_g):
    bsz, seq, d = x.shape
    n_ctx = ctx.shape[1]
    depth = ada_w.shape[0]
    l = n_ctx + seq
    tm = _token_tile(l)
    tf = 1024
    nb = next(n for n in (8, 4, 2, 1) if bsz % n == 0)

    mods = _mods(c, c_ctx, ada_w, ada_b)
    ret_dk = (ev_w_in.shape[2] - 4 * GROUP) // (2 * RET_HEADS)
    cos_t, sin_t = _rope_tables(n_ctx, seq, ret_dk)

    xs = jnp.concatenate([ctx, x], axis=1)
    for layer in range(depth):
        mod = mods[layer]
        w_out = mix_w_out[layer].astype(BF16)
        if layer % 2 == 0:
            e = layer // 2
            (p,) = _proj(xs, mod, norm1_g[layer], ev_w_in[e].astype(BF16), None, tm=tm, n_ctx=n_ctx)
            a_parts = tuple(
                _lru(p, lru_conv_w[e], lru_conv_b[e], _blockdiag_halves(lru_wa[e, dr]).astype(BF16),
                     lru_ba[e, dr], _blockdiag_halves(lru_wx[e, dr]).astype(BF16), lru_bx[e, dr],
                     lru_lambda[e, dr], n_ctx=n_ctx, reverse=rev) for dr, rev in enumerate((False, True)))
            ret = _retention(p, cos_t, sin_t, ret_log_gamma[e, 0], ret_log_gamma[e, 1],
                             n_ctx=n_ctx, reverse=False, nb=nb)
            ret = _retention(p, cos_t, sin_t, ret_log_gamma[e, 1], ret, n_ctx=n_ctx, reverse=True, nb=nb)
            b_parts = (ret,)
            gate_cols = (1, 4)
        else:
            o = layer // 2
            n_main = 9 * GROUP
            w_in = od_w_in[o]
            w_tail = jnp.pad(w_in[:, n_main:], ((0, 0), (0, LANES - (w_in.shape[1] - n_main)))).astype(BF16)
            p, tail = _proj(xs, mod, norm1_g[layer], w_in[:, :n_main].astype(BF16), w_tail, tm=tm, n_ctx=n_ctx)
            qkv = _gdn_prep(p, gdn_conv_w[o], n_ctx=n_ctx)
            mix = []
            for dr, rev in enumerate((False, True)):
                g_col = (2 + dr) * LIN_HEADS
                a_row = jnp.zeros((1, LANES), F32).at[0, g_col:g_col + LIN_HEADS].set(gdn_a_log[o, dr])
                dt_row = jnp.zeros((1, LANES), F32).at[0, g_col:g_col + LIN_HEADS].set(gdn_dt_bias[o, dr])
                mix.append((
                    _gla(p, hg_lb_logits[dr], n_ctx=n_ctx, reverse=rev, layer=o, nb=nb),
                    _gdn(qkv, tail, a_row, dt_row, n_ctx=n_ctx, reverse=rev, direction=dr, nb=nb)))
            a_parts, b_parts = zip(*mix)
            gate_cols = (4, 8)
        xs = _post(xs, mod, a_parts, b_parts, p, gate_cols[0], gate_cols[1], w_out,
                   even=layer % 2 == 0, tm=tm, n_ctx=n_ctx)
        last = layer == depth - 1
        xs = _mlp_res(xs, mod, norm2_g[layer], mlp_w1[layer].astype(BF16), mlp_w2[layer].astype(BF16), final_g,
                      tm=(_token_tile(seq) if last else tm) // 2, tf=tf, n_ctx=n_ctx, final=last)
    return xs
```

```python
import functools

import numpy as np
import jax
import jax.numpy as jnp
from jax import lax
from jax.experimental import pallas as pl
from jax.experimental.pallas import tpu as pltpu

F32 = jnp.float32
BF16 = jnp.bfloat16

EPS = 1e-6
GRID_W = 64
CONV_W = 4
CONV_LEFT = CONV_W // 2
LRU_BLOCKS = 8
LRU_C = 8.0
RET_HEADS = 4
RET_CHUNK = 256
ROPE_BASE = 10000.0
LIN_HEADS = 4
GLA_CHUNK = 128
LIN_CHUNK = 64
HEAD_DV = 128

LANES = 128
SUBLANES = 8
V7X_VMEM_BYTES = 64 * 1024 * 1024
VMEM_LIMIT = (V7X_VMEM_BYTES * 3) // 4
GROUP = 512

HALO = 2 * SUBLANES
ACT = jnp.bfloat16


def _cparams(*sem):
    return pltpu.CompilerParams(dimension_semantics=sem, vmem_limit_bytes=VMEM_LIMIT)


def _sigmoid(x):
    return 1.0 / (1.0 + jnp.exp(-x))


def _silu(x):
    return x * _sigmoid(x)


def _softplus(x):
    return jnp.maximum(x, 0.0) + jnp.log(1.0 + jnp.exp(-jnp.abs(x)))


def _gelu_tanh(x):
    return 0.5 * x * (1.0 + jnp.tanh(0.7978845608028654 * (x + 0.044715 * x * x * x)))


def _dot(a, b):
    return jnp.dot(a.astype(BF16), b.astype(BF16), preferred_element_type=F32)


def _dot_nt(a, b):
    return lax.dot_general(a.astype(BF16), b.astype(BF16), (((1,), (1,)), ((), ())),
                           preferred_element_type=F32)


def _split_bf16(x):
    hi = x.astype(BF16)
    lo = (x - hi.astype(F32)).astype(BF16)
    return hi, lo


def _dot_sel(m2, x):
    hi, lo = _split_bf16(x)
    return jnp.dot(m2, jnp.concatenate([hi, lo], axis=0), preferred_element_type=F32)


def _doubled(m):
    return jnp.asarray(np.concatenate([m, m], axis=1), BF16)


def _dot3(a, b):
    ah, al = _split_bf16(a)
    bh, bl = _split_bf16(b)
    return (jnp.dot(ah, bh, preferred_element_type=F32) + jnp.dot(ah, bl, preferred_element_type=F32)
            + jnp.dot(al, bh, preferred_element_type=F32))


def _row_select(t, tm, n_ctx, mc_ref, ml_ref, idx):
    rows = t * tm + lax.broadcasted_iota(jnp.int32, (tm, 1), 0)
    return jnp.where(rows < n_ctx, mc_ref[idx:idx + 1, :], ml_ref[idx:idx + 1, :])


def _rms(x, g):
    return x * lax.rsqrt(jnp.mean(x * x, axis=-1, keepdims=True) + EPS) * g


def _head_norm(o, center):
    outs = []
    for h in range(o.shape[-1] // HEAD_DV):
        oh = o[:, h * HEAD_DV:(h + 1) * HEAD_DV]
        if center:
            oh = oh - jnp.mean(oh, axis=-1, keepdims=True)
        outs.append(oh * lax.rsqrt(jnp.mean(oh * oh, axis=-1, keepdims=True) + EPS))
    return jnp.concatenate(outs, axis=-1)


def _scan_tile(step, n_first, n_total, reverse):
    if not reverse:
        return step
    return jnp.where(step < n_first, n_first - 1 - step, n_total - 1 - (step - n_first))


def _mods_body(s_ref, w_ref, b_ref, o_ref):
    s = _silu(s_ref[...])
    o_ref[...] = _dot3(s, w_ref[...]) + b_ref[...]


def _mods(c, c_ctx, ada_w, ada_b):
    depth, d, n6 = ada_w.shape
    bsz = c.shape[0]
    rows = -(-(bsz + 1) // SUBLANES) * SUBLANES
    s = jnp.zeros((rows, d), F32).at[:bsz].set(c).at[bsz].set(c_ctx)
    tn = 1024
    out = pl.pallas_call(
        _mods_body,
        grid=(depth, n6 // tn),
        in_specs=[pl.BlockSpec((rows, d), lambda l, n: (0, 0)),
                  pl.BlockSpec((None, d, tn), lambda l, n: (l, 0, n)),
                  pl.BlockSpec((None, 1, tn), lambda l, n: (l, 0, n))],
        out_specs=pl.BlockSpec((None, rows, tn), lambda l, n: (l, 0, n)),
        out_shape=jax.ShapeDtypeStruct((depth, rows, n6), F32),
        compiler_params=_cparams("parallel", "parallel"),
        name="mods",
    )(s, ada_w, ada_b.reshape(depth, 1, n6))
    return out.reshape(depth, rows, 6, d)


def _proj_body(*refs, tm, n_ctx, has_tail):
    if has_tail:
        x_ref, ml_ref, mc_ref, g_ref, w_ref, wt_ref, o_ref, ot_ref = refs
    else:
        x_ref, ml_ref, mc_ref, g_ref, w_ref, o_ref = refs
    t = pl.program_id(1)
    y = _rms(x_ref[...], g_ref[...])
    shift = _row_select(t, tm, n_ctx, mc_ref, ml_ref, 0)
    scale = _row_select(t, tm, n_ctx, mc_ref, ml_ref, 1)
    h = (y * (1.0 + scale) + shift).astype(BF16)
    o_ref[...] = jnp.dot(h, w_ref[...], preferred_element_type=F32).astype(o_ref.dtype)
    if has_tail:
        ot_ref[...] = jnp.dot(h, wt_ref[...], preferred_element_type=F32)


def _proj(x, mod, g, w, w_tail, *, tm, n_ctx):
    bsz, l, d = x.shape
    n_out = w.shape[1]
    ctx_row = bsz
    resident = dict(pipeline_mode=pl.Buffered(1))
    in_specs = [pl.BlockSpec((None, tm, d), lambda b, t: (b, t, 0)),
                pl.BlockSpec((None, 6, d), lambda b, t: (b, 0, 0)),
                pl.BlockSpec((None, 6, d), lambda b, t: (ctx_row, 0, 0)),
                pl.BlockSpec((1, d), lambda b, t: (0, 0)),
                pl.BlockSpec((d, n_out), lambda b, t: (0, 0), **resident)]
    out_specs = [pl.BlockSpec((None, tm, n_out), lambda b, t: (b, t, 0))]
    out_shape = [jax.ShapeDtypeStruct((bsz, l, n_out), ACT)]
    args = [x, mod, mod, g.reshape(1, d), w]
    if w_tail is not None:
        in_specs.append(pl.BlockSpec((d, LANES), lambda b, t: (0, 0), **resident))
        out_specs.append(pl.BlockSpec((None, tm, LANES), lambda b, t: (b, t, 0)))
        out_shape.append(jax.ShapeDtypeStruct((bsz, l, LANES), F32))
        args.append(w_tail)
    outs = pl.pallas_call(
        functools.partial(_proj_body, tm=tm, n_ctx=n_ctx, has_tail=w_tail is not None),
        grid=(bsz, l // tm),
        in_specs=in_specs, out_specs=out_specs, out_shape=out_shape,
        compiler_params=_cparams("parallel", "parallel"),
        name="proj",
    )(*args)
    return outs


def _lru_body(sm_ref, se_ref, x_ref, xp_ref, xn_ref, cw_ref, cb_ref, wa_ref, ba_ref, wx_ref, bx_ref, lam_ref,
              o_ref, a_scr, b_scr, h_scr, *, tl, n_ctx_tiles, n_tiles, reverse):
    step = pl.program_id(0)
    tile = _scan_tile(step, n_ctx_tiles, n_tiles, reverse)
    bsz = x_ref.shape[0]
    width = x_ref.shape[2]
    half = width // 2
    seg_first = jnp.logical_or(tile == 0, tile == n_ctx_tiles)
    seg_last = jnp.logical_or(tile == n_ctx_tiles - 1, tile == n_tiles - 1)
    keep_prev = jnp.where(seg_first, 0.0, 1.0)
    keep_next = jnp.where(seg_last, 0.0, 1.0)
    neg_c_sp = -LRU_C * _softplus(-lam_ref[...])

    @pl.when(step == 0)
    def _():
        h_scr[...] = jnp.zeros_like(h_scr)

    pair = 2 if bsz % 2 == 0 else 1

    def block_dot(ub, w_ref):
        return jnp.concatenate([jnp.dot(ub[:, :half], w_ref[0], preferred_element_type=F32),
                                jnp.dot(ub[:, half:], w_ref[1], preferred_element_type=F32)], axis=-1)

    def gates(bp, carry):
        bs = [bp * pair + n for n in range(pair)]
        taps = [_conv_taps(sm_ref, se_ref, x_ref[b], xp_ref[b], xn_ref[b], keep_prev, keep_next, tl) for b in bs]
        u = [cb_ref[...] + sum(cw_ref[k:k + 1, :] * tp[k] for k in range(CONV_W)) for tp in taps]
        ub = [x.astype(BF16) for x in u]
        r_pre = [block_dot(x, wa_ref) + ba_ref[...] for x in ub]
        i_pre = [block_dot(x, wx_ref) + bx_ref[...] for x in ub]
        a = [jnp.exp(neg_c_sp * _sigmoid(x)) for x in r_pre]
        for n, b in enumerate(bs):
            a_scr[b] = a[n]
            b_scr[b] = jnp.sqrt(1.0 - a[n] * a[n]) * (_sigmoid(i_pre[n]) * u[n])
        return carry

    lax.fori_loop(0, bsz // pair, gates, 0)

    def scan_rows(j, hs):
        idx = (tl - 1 - j) if reverse else j
        new = []
        for b in range(bsz):
            h = a_scr[b, pl.ds(idx, 1), :] * hs[b] + b_scr[b, pl.ds(idx, 1), :]
            b_scr[b, pl.ds(idx, 1), :] = h
            new.append(h)
        return tuple(new)

    hs = tuple(h_scr[b, 0:1, :] for b in range(bsz))
    hs = lax.fori_loop(0, tl, scan_rows, hs)
    for b in range(bsz):
        h_scr[b, 0:1, :] = hs[b]
    o_ref[...] = b_scr[...].astype(o_ref.dtype)


def _lru(p, conv_w, conv_b, wa, ba, wx, bx, lam, *, n_ctx, reverse, tl=256):
    bsz, l, _ = p.shape
    width = GROUP
    n_tiles = l // tl
    n_ctx_tiles = n_ctx // tl
    hb = tl // HALO
    n_hblocks = l // HALO

    def tile_of(s):
        return _scan_tile(s, n_ctx_tiles, n_tiles, reverse)

    vec = pl.BlockSpec((1, width), lambda s: (0, 0))
    mat = pl.BlockSpec((2, width // 2, width // 2), lambda s: (0, 0, 0))
    sm, se = (jnp.asarray(m, p.dtype) for m in _shift_matrices(tl))
    return pl.pallas_call(
        functools.partial(_lru_body, tl=tl, n_ctx_tiles=n_ctx_tiles, n_tiles=n_tiles, reverse=reverse),
        grid=(n_tiles,),
        in_specs=[pl.BlockSpec(sm.shape, lambda s: (0, 0)), pl.BlockSpec(se.shape, lambda s: (0, 0)),
                  pl.BlockSpec((bsz, tl, width), lambda s: (0, tile_of(s), 0)),
                  pl.BlockSpec((bsz, HALO, width), lambda s: (0, jnp.maximum(tile_of(s) * hb - 1, 0), 0)),
                  pl.BlockSpec((bsz, HALO, width),
                               lambda s: (0, jnp.minimum((tile_of(s) + 1) * hb, n_hblocks - 1), 0)),
                  pl.BlockSpec((CONV_W, width), lambda s: (0, 0)), vec, mat, vec, mat, vec, vec],
        out_specs=pl.BlockSpec((bsz, tl, width), lambda s: (0, tile_of(s), 0)),
        out_shape=jax.ShapeDtypeStruct((bsz, l, width), ACT),
        scratch_shapes=[pltpu.VMEM((bsz, tl, width), F32),
                        pltpu.VMEM((bsz, tl, width), F32),
                        pltpu.VMEM((bsz, SUBLANES, width), F32)],
        compiler_params=_cparams("arbitrary"),
        name="lru_rev" if reverse else "lru_fwd",
    )(sm, se, p, p, p, conv_w, conv_b.reshape(1, width), wa, ba.reshape(1, width), wx, bx.reshape(1, width),
      lam.reshape(1, width))


def _blockdiag_halves(w):
    nb, n, _ = w.shape
    per = nb // 2
    w = w.reshape(2, per, n, n)
    eye = jnp.eye(per, dtype=w.dtype)
    out = w[:, :, :, None, :] * eye[None, :, None, :, None]
    return out.reshape(2, per * n, per * n)


def _ret_body(*refs, reverse, dk, nb):
    if reverse:
        lg_ref, q_ref, k_ref, v_ref, cos_ref, sin_ref, acc_ref, o_ref, qd_scr, kd_scr, cd_scr, st_scr = refs
    else:
        lg_ref, lgo_ref, q_ref, k_ref, v_ref, cos_ref, sin_ref, o_ref, dec_scr, qd_scr, kd_scr, cd_scr, st_scr = refs
    c = RET_CHUNK
    step = pl.program_id(1)

    @pl.when(step == 0)
    def _():
        st_scr[...] = jnp.zeros_like(st_scr)
        row = lax.broadcasted_iota(jnp.int32, (c, HEAD_DV), 0)
        pos = ((c - 1 - row) if reverse else row).astype(F32)
        for h in range(RET_HEADS):
            lg = lg_ref[h]
            qd_scr[h] = jnp.exp(lg * (pos + 1.0))
            kd_scr[h] = jnp.exp(lg * (c - 1.0 - pos))
            cd_scr[h] = jnp.exp(jnp.full((SUBLANES, LANES), c, F32) * lg)
        if not reverse:
            i = lax.broadcasted_iota(jnp.int32, (c, c), 0)
            j = lax.broadcasted_iota(jnp.int32, (c, c), 1)
            below = jnp.maximum(i - j, 0).astype(F32)
            above = jnp.maximum(j - i, 0).astype(F32)
            for h in range(RET_HEADS):
                dec_scr[h] = (jnp.where(i >= j, jnp.exp(lg_ref[h] * below), 0.0)
                              + jnp.where(j >= i, jnp.exp(lgo_ref[h] * above), 0.0))

    def rope(x):
        width = x.shape[-1]
        lane = lax.broadcasted_iota(jnp.int32, x.shape, 1)
        first = (lane & (dk - 1)) < dk // 2
        rot = jnp.where(first, pltpu.roll(x, width - dk // 2, axis=1), pltpu.roll(x, dk // 2, axis=1))
        return x * cos_ref[...] + rot * sin_ref[...]

    q = [rope(q_ref[bb].astype(F32)) for bb in range(nb)]
    k = [rope(k_ref[bb].astype(F32)) * (dk ** -0.5) for bb in range(nb)]
    pairs = [(bb, h) for bb in range(nb) for h in range(RET_HEADS)]
    qh = [q[bb][:, h * dk:(h + 1) * dk] for bb, h in pairs]
    kh = [k[bb][:, h * dk:(h + 1) * dk] for bb, h in pairs]
    vh = [v_ref[bb, :, h * HEAD_DV:(h + 1) * HEAD_DV] for bb, h in pairs]
    state = [st_scr[bb, h] for bb, h in pairs]
    o_x = [_dot(qh[i], state[i]) * qd_scr[h] for i, (bb, h) in enumerate(pairs)]
    if reverse:
        o_in = [acc_ref[bb, :, h * HEAD_DV:(h + 1) * HEAD_DV].astype(F32) for bb, h in pairs]
    else:
        s = [_dot_nt(qh[i], kh[i]) * dec_scr[h] for i, (bb, h) in enumerate(pairs)]
        o_in = [_dot(s[i], vh[i]) for i in range(len(pairs))]
    kv = [_dot((kh[i] * kd_scr[h][:, :dk]).T, vh[i]) for i, (bb, h) in enumerate(pairs)]
    for i, (bb, h) in enumerate(pairs):
        st_scr[bb, h] = state[i] * cd_scr[h][0:1, :] + kv[i]
        o_ref[bb, :, h * HEAD_DV:(h + 1) * HEAD_DV] = (o_in[i] + o_x[i]).astype(o_ref.dtype)


def _retention(p, cos_t, sin_t, log_gamma, other, *, n_ctx, reverse, nb):
    bsz, l, _ = p.shape
    c = RET_CHUNK
    qk_w = cos_t.shape[1]
    dk = qk_w // RET_HEADS
    n_chunks = l // c
    n_ctx_chunks = n_ctx // c

    def ch(s):
        return _scan_tile(s, n_ctx_chunks, n_chunks, reverse)

    q_col = (2 * GROUP) // qk_w
    smem = pl.BlockSpec(memory_space=pltpu.SMEM)
    out_block = pl.BlockSpec((nb, c, GROUP), lambda b, s: (b, ch(s), 0))
    data_specs = [pl.BlockSpec((nb, c, qk_w), lambda b, s: (b, ch(s), q_col)),
                  pl.BlockSpec((nb, c, qk_w), lambda b, s: (b, ch(s), q_col + 1)),
                  pl.BlockSpec((nb, c, GROUP), lambda b, s: (b, ch(s), 3)),
                  pl.BlockSpec((c, qk_w), lambda b, s: (ch(s), 0)),
                  pl.BlockSpec((c, qk_w), lambda b, s: (ch(s), 0))]
    scratch = [pltpu.VMEM((RET_HEADS, c, HEAD_DV), F32),
               pltpu.VMEM((RET_HEADS, c, HEAD_DV), F32),
               pltpu.VMEM((RET_HEADS, SUBLANES, LANES), F32),
               pltpu.VMEM((nb, RET_HEADS, dk, HEAD_DV), F32)]
    if reverse:
        in_specs = [smem] + data_specs + [out_block]
        args = (log_gamma, p, p, p, cos_t, sin_t, other)
    else:
        in_specs = [smem, smem] + data_specs
        args = (log_gamma, other, p, p, p, cos_t, sin_t)
        scratch = [pltpu.VMEM((RET_HEADS, c, c), F32)] + scratch
    return pl.pallas_call(
        functools.partial(_ret_body, reverse=reverse, dk=dk, nb=nb),
        grid=(bsz // nb, n_chunks),
        in_specs=in_specs,
        out_specs=out_block,
        out_shape=jax.ShapeDtypeStruct((bsz, l, GROUP), ACT),
        scratch_shapes=scratch,
        compiler_params=_cparams("parallel", "arbitrary"),
        name="ret_rev" if reverse else "ret_fwd",
    )(*args)


def _rope_tables(n_ctx, seq, dk):
    n_freq = dk // 4
    inv = jnp.power(ROPE_BASE, -jnp.arange(n_freq, dtype=F32) / n_freq)
    rows = seq // GRID_W
    r = jnp.arange(rows, dtype=F32)
    col = jnp.arange(GRID_W, dtype=F32)
    row_ang = jnp.broadcast_to(r[:, None, None] * inv, (rows, GRID_W, n_freq))
    col_ang = jnp.broadcast_to(col[None, :, None] * inv, (rows, GRID_W, n_freq))
    ang = jnp.concatenate([row_ang, col_ang], axis=-1).reshape(rows * GRID_W, 2 * n_freq)
    cos, sin = jnp.cos(ang), jnp.sin(ang)
    cos_h = jnp.concatenate([cos, cos], axis=-1)
    sin_h = jnp.concatenate([-sin, sin], axis=-1)
    cos_t = jnp.concatenate([jnp.ones((n_ctx, dk), F32), cos_h], axis=0)
    sin_t = jnp.concatenate([jnp.zeros((n_ctx, dk), F32), sin_h], axis=0)
    return jnp.tile(cos_t, (1, RET_HEADS)), jnp.tile(sin_t, (1, RET_HEADS))


_GLA_LEVELS = tuple(GLA_CHUNK >> (i + 1) for i in range(GLA_CHUNK.bit_length() - 1))


def _gla_matrices(reverse):
    c = GLA_CHUNK
    p = np.arange(c)
    mats = [(p[None, :] <= p[:, None]), (p[None, :] > p[:, None])]
    for s in _GLA_LEVELS:
        bs = (p // s) * s
        right = ((p // s) % 2 == 1)
        m_r = (p[None, :] > bs[:, None]) & (p[None, :] <= p[:, None])
        m_l = (p[None, :] > p[:, None]) & (p[None, :] <= (bs + s)[:, None])
        mats.append(np.where(right[:, None], m_r, m_l))
    m = np.stack(mats).astype(np.float32)
    if reverse:
        m = m[:, ::-1, ::-1]
    return m.reshape(len(mats) * c, c)


def _gla_body(m_ref, lb_ref, q_ref, f_ref, v_ref, o_ref, st_scr, *, reverse, layer, nb):
    c = GLA_CHUNK
    step = pl.program_id(1)

    @pl.when(step == 0)
    def _():
        st_scr[...] = jnp.zeros_like(st_scr)

    lg = lb_ref[...]
    e = jnp.exp(lg - jnp.max(lg, axis=0, keepdims=True))
    sm = e / jnp.sum(e, axis=0, keepdims=True)
    lb = jnp.sum(sm[0:layer + 1, :], axis=0, keepdims=True) - sm[0:1, :]

    ri = lax.broadcasted_iota(jnp.int32, (c, 1), 0)
    i2 = lax.broadcasted_iota(jnp.int32, (c, c), 0)
    j2 = lax.broadcasted_iota(jnp.int32, (c, c), 1)
    if reverse:
        ri, i2, j2 = c - 1 - ri, c - 1 - i2, c - 1 - j2
    last = 0 if reverse else c - 1
    width = q_ref.shape[-1]
    nh = width // HEAD_DV

    rows_bb = list(range(nb))
    pairs = [(bb, h) for bb in rows_bb for h in range(nh)]

    def head(x, h):
        return x[:, h * HEAD_DV:(h + 1) * HEAD_DV]

    f = [lb + (1.0 - lb) * _sigmoid(f_ref[bb].astype(F32)) for bb in rows_bb]
    k = [1.0 - x for x in f]
    q = [_silu(q_ref[bb].astype(F32)) for bb in rows_bb]
    ex = [_dot_sel(m_ref[...], jnp.log(x)) for x in f]
    a_last = [jnp.exp(x[last:last + 1, :]) for x in ex]
    qb = [x.astype(BF16) for x in q]
    kb = [x.astype(BF16) for x in k]
    qa = [qb[bb] * jnp.exp(ex[bb][0:c, :].astype(BF16)) for bb in rows_bb]
    kl = [kb[bb] * jnp.exp(ex[bb][c:2 * c, :].astype(BF16)) for bb in rows_bb]
    scores = [jnp.where(i2 == j2, _dot_nt(head(qb[bb], h), head(kb[bb], h)), 0.0) for bb, h in pairs]
    for lvl, s in enumerate(_GLA_LEVELS):
        sh = s.bit_length() - 1
        right = jnp.broadcast_to(((ri >> sh) & 1).astype(F32), (c, width)).astype(BF16) > 0
        x = [jnp.where(right, qb[bb], kb[bb]) * jnp.exp(ex[bb][(2 + lvl) * c:(3 + lvl) * c, :].astype(BF16))
             for bb in rows_bb]
        bi, bj = i2 >> sh, j2 >> sh
        valid = bj == jnp.where((bi & 1) == 1, bi - 1, -1)
        gram = [_dot_nt(head(x[bb], h), head(x[bb], h)) for bb, h in pairs]
        scores = [jnp.where(valid, gm, sc) for sc, gm in zip(scores, gram)]

    state_t = [st_scr[bb, h] for bb, h in pairs]
    o_intra = [_dot(scores[i], head(v_ref[bb], h)) for i, (bb, h) in enumerate(pairs)]
    o_inter = [_dot_nt(head(qa[bb], h), state_t[i]) for i, (bb, h) in enumerate(pairs)]
    kv = [_dot(head(v_ref[bb], h).astype(F32).T, head(kl[bb], h)) for bb, h in pairs]
    for i, (bb, h) in enumerate(pairs):
        o_ref[bb, :, h * HEAD_DV:(h + 1) * HEAD_DV] = (o_intra[i] + o_inter[i]).astype(o_ref.dtype)
        st_scr[bb, h] = state_t[i] * head(a_last[bb], h) + kv[i]


def _gla(p, lb_logits, *, n_ctx, reverse, layer, nb):
    bsz, l, _ = p.shape
    c = GLA_CHUNK
    n_chunks = l // c
    n_ctx_chunks = n_ctx // c
    m = _doubled(_gla_matrices(reverse))

    def ch(s):
        return _scan_tile(s, n_ctx_chunks, n_chunks, reverse)

    f_col = 2 if reverse else 1
    return pl.pallas_call(
        functools.partial(_gla_body, reverse=reverse, layer=layer, nb=nb),
        grid=(bsz // nb, n_chunks),
        in_specs=[pl.BlockSpec(m.shape, lambda b, s: (0, 0)),
                  pl.BlockSpec(lb_logits.shape, lambda b, s: (0, 0)),
                  pl.BlockSpec((nb, c, GROUP), lambda b, s: (b, ch(s), 0)),
                  pl.BlockSpec((nb, c, GROUP), lambda b, s: (b, ch(s), f_col)),
                  pl.BlockSpec((nb, c, GROUP), lambda b, s: (b, ch(s), 3))],
        out_specs=pl.BlockSpec((nb, c, GROUP), lambda b, s: (b, ch(s), 0)),
        out_shape=jax.ShapeDtypeStruct((bsz, l, GROUP), ACT),
        scratch_shapes=[pltpu.VMEM((nb, LIN_HEADS, HEAD_DV, HEAD_DV), F32)],
        compiler_params=_cparams("parallel", "arbitrary"),
        name="gla_rev" if reverse else "gla_fwd",
    )(m, lb_logits, p, p, p)


def _shift_matrices(rows):
    offs = [t - CONV_LEFT for t in range(CONV_W) if t != CONV_LEFT]
    main = np.zeros((len(offs) * rows, rows), np.float32)
    edge = np.zeros((len(offs) * SUBLANES, 2 * HALO), np.float32)
    for n, off in enumerate(offs):
        for i in range(rows):
            j = i + off
            if 0 <= j < rows:
                main[n * rows + i, j] = 1.0
            elif j < 0:
                edge[n * SUBLANES + i, HALO + j] = 1.0
            else:
                edge[n * SUBLANES + i - (rows - SUBLANES), HALO + j - rows] = 1.0
    return main, edge


def _conv_taps(sm_ref, se_ref, x, xp, xn, keep_prev, keep_next, rows):
    sh = jnp.dot(sm_ref[...], x, preferred_element_type=F32)
    halo = jnp.concatenate([xp * keep_prev.astype(xp.dtype), xn * keep_next.astype(xn.dtype)], axis=0)
    fix = jnp.dot(se_ref[...], halo, preferred_element_type=F32)
    taps, n = [], 0
    for t in range(CONV_W):
        off = t - CONV_LEFT
        if off == 0:
            taps.append(x.astype(F32))
            continue
        s = sh[n * rows:(n + 1) * rows, :]
        f = fix[n * SUBLANES:(n + 1) * SUBLANES, :]
        if off < 0:
            s = jnp.concatenate([s[0:SUBLANES] + f, s[SUBLANES:]], axis=0)
        else:
            s = jnp.concatenate([s[:rows - SUBLANES], s[rows - SUBLANES:] + f], axis=0)
        taps.append(s)
        n += 1
    return taps


def _gdn_prep_body(sm_ref, se_ref, cw_ref, q_ref, qp_ref, qn_ref, k_ref, kp_ref, kn_ref, v_ref, vp_ref, vn_ref,
                   qo_ref, ko_ref, vo_ref, *, tp, n_ctx_tiles, n_tiles):
    c = LIN_CHUNK
    nh = LIN_HEADS
    tile = pl.program_id(1)
    seg_first = jnp.logical_or(tile == 0, tile == n_ctx_tiles)
    seg_last = jnp.logical_or(tile == n_ctx_tiles - 1, tile == n_tiles - 1)
    keep_prev = jnp.where(seg_first, 0.0, 1.0)
    keep_next = jnp.where(seg_last, 0.0, 1.0)

    all_taps = [_conv_taps(sm_ref, se_ref, x[...], xp[...], xn[...], keep_prev, keep_next, tp)
                for x, xp, xn in ((q_ref, qp_ref, qn_ref), (k_ref, kp_ref, kn_ref), (v_ref, vp_ref, vn_ref))]

    def conv_silu(col):
        taps = all_taps[col]
        u = cw_ref[0:1, col * GROUP:(col + 1) * GROUP] * taps[0]
        for t in range(1, CONV_W):
            u = u + cw_ref[t:t + 1, col * GROUP:(col + 1) * GROUP] * taps[t]
        return _silu(u)

    def l2n(x):
        return x * lax.rsqrt(jnp.sum(x * x, axis=-1, keepdims=True) + EPS)

    def emit(out_ref, x, norm, scale):
        for h in range(nh):
            xh = x[:, h * HEAD_DV:(h + 1) * HEAD_DV]
            if norm:
                xh = l2n(xh) * scale
            for j in range(tp // c):
                out_ref[j, h * c:(h + 1) * c, :] = xh[j * c:(j + 1) * c, :].astype(out_ref.dtype)

    emit(qo_ref, conv_silu(0), True, HEAD_DV ** -0.5)
    emit(ko_ref, conv_silu(1), True, 1.0)
    emit(vo_ref, conv_silu(2), False, 1.0)


def _gdn_prep(p, conv_w, *, n_ctx, tp=256):
    bsz, l, _ = p.shape
    c = LIN_CHUNK
    n_tiles = l // tp
    hb = tp // HALO
    n_hblocks = l // HALO

    def triple(col):
        return [pl.BlockSpec((None, tp, GROUP), lambda b, t: (b, t, col)),
                pl.BlockSpec((None, HALO, GROUP), lambda b, t: (b, jnp.maximum(t * hb - 1, 0), col)),
                pl.BlockSpec((None, HALO, GROUP), lambda b, t: (b, jnp.minimum((t + 1) * hb, n_hblocks - 1), col))]

    out_spec = pl.BlockSpec((None, tp // c, LIN_HEADS * c, HEAD_DV), lambda b, t: (b, t, 0, 0))
    out_sds = jax.ShapeDtypeStruct((bsz, l // c, LIN_HEADS * c, HEAD_DV), ACT)
    sm, se = (jnp.asarray(m, p.dtype) for m in _shift_matrices(tp))
    const2 = lambda b, t: (0, 0)
    return pl.pallas_call(
        functools.partial(_gdn_prep_body, tp=tp, n_ctx_tiles=n_ctx // tp, n_tiles=n_tiles),
        grid=(bsz, n_tiles),
        in_specs=[pl.BlockSpec(sm.shape, const2), pl.BlockSpec(se.shape, const2), pl.BlockSpec(conv_w.shape, const2)]
                 + triple(5) + triple(6) + triple(7),
        out_specs=[out_spec] * 3, out_shape=[out_sds] * 3,
        compiler_params=_cparams("parallel", "parallel"),
        name="gdn_prep",
    )(sm, se, conv_w, *([p] * 9))


def _gdn_body(cs_ref, arow_ref, dtrow_ref, t_ref, q_ref, k_ref, v_ref, o_ref, st_scr,
              *, reverse, beta_col, g_col, nb):
    c = LIN_CHUNK
    nh = LIN_HEADS
    n = nh * c
    step = pl.program_id(1)

    @pl.when(step == 0)
    def _():
        st_scr[...] = jnp.zeros_like(st_scr)

    i2 = lax.broadcasted_iota(jnp.int32, (c, n), 0)
    lane = lax.broadcasted_iota(jnp.int32, (c, n), 1)
    j2 = lane & (c - 1)
    grp = lane >> 6
    mask = (j2 >= i2) if reverse else (j2 <= i2)
    same32 = (i2 >> 5) == (j2 >> 5)
    same16 = (i2 >> 4) == (j2 >> 4)
    eye = jnp.where(i2 == j2, 1.0, 0.0)
    grp_rows = [jnp.where((lax.broadcasted_iota(jnp.int32, (1, n), 1) >> 6) == h, 1.0, 0.0).astype(BF16)
                for h in range(nh)]
    last = 0 if reverse else c - 1

    def each(fn, *lists):
        return [fn(*args) for args in zip(*lists)]

    def head_bcast(x, col):
        return jnp.concatenate([jnp.broadcast_to(x[:, col + h:col + h + 1], (c, HEAD_DV)) for h in range(nh)], axis=0)

    def side_by_side(x):
        out = x[(nh - 1) * c:nh * c, :]
        for h in range(nh - 2, -1, -1):
            out = jnp.where(grp == h, x[h * c:(h + 1) * c, :], out)
        return out

    def block_diag(x):
        xb = x.astype(BF16)
        return jnp.concatenate([xb * grp_rows[h] for h in range(nh)], axis=0)

    def dot_sbs(x, y_bd):
        return jnp.dot(x.astype(BF16), y_bd, preferred_element_type=F32)

    rows_bb = list(range(nb))
    q = [q_ref[bb].astype(F32) for bb in rows_bb]
    k = [k_ref[bb].astype(F32) for bb in rows_bb]
    v = [v_ref[bb].astype(F32) for bb in rows_bb]

    tail = [t_ref[bb] for bb in rows_bb]
    beta = [head_bcast(_sigmoid(x), beta_col) for x in tail]
    g = [head_bcast(-jnp.exp(arow_ref[...]) * _softplus(x + dtrow_ref[...]), g_col) for x in tail]
    gc = [_dot_sel(cs_ref[...], x) for x in g]
    gc_last = [jnp.concatenate([jnp.broadcast_to(x[h * c + last:h * c + last + 1, :], (c, HEAD_DV))
                                for h in range(nh)], axis=0) for x in gc]
    diff = [side_by_side(jnp.concatenate([x, x], axis=1)) - jnp.transpose(x)[0:1, :] for x in gc]
    gam = [jnp.where(mask, jnp.exp(jnp.where(mask, x, 0.0)), 0.0) for x in diff]

    kb = each(lambda a, b: a * b, k, beta)
    kq = each(lambda a, b, kx: _dot_nt(jnp.concatenate([a, b], axis=0), kx), kb, q, k)
    mm = each(lambda x, gm: jnp.where(i2 == j2, 0.0, side_by_side(x[:n]) * gm), kq, gam)
    qk = each(lambda x, gm: side_by_side(x[n:]) * gm, kq, gam)
    egc = [jnp.exp(x) for x in gc]
    rhs = each(lambda a, b, kbx, e: jnp.concatenate([a * b, kbx * e], axis=1), v, beta, kb, egc)

    md = [jnp.where(same16, x, 0.0) for x in mm]
    m2 = each(lambda a: dot_sbs(a, block_diag(a)), md)
    t = [eye - x for x in md]
    tm = each(lambda a, b: dot_sbs(jnp.concatenate([a, b], axis=0), block_diag(b)), t, m2)
    t = each(lambda a, x: a + x[:c], t, tm)
    m4 = [x[c:] for x in tm]
    tm = each(lambda a, b: dot_sbs(jnp.concatenate([a, b], axis=0), block_diag(b)), t, m4)
    t = each(lambda a, x: a + x[:c], t, tm)
    t = each(lambda a, x: a + dot_sbs(a, block_diag(x[c:])), t, tm)
    c32 = each(lambda a, b: jnp.where(same32, a - b, 0.0), mm, md)
    ct = each(lambda a, b: dot_sbs(a, block_diag(b)), c32, t)
    t = each(lambda a, b: a - dot_sbs(a, block_diag(b)), t, ct)
    c64 = [jnp.where(same32, 0.0, x) for x in mm]
    ct = each(lambda a, b: dot_sbs(a, block_diag(b)), c64, t)
    t = each(lambda a, b: a - dot_sbs(a, block_diag(b)), t, ct)
    sol = each(lambda a, b: jnp.dot(block_diag(a), b.astype(BF16), preferred_element_type=F32), t, rhs)

    qd = each(lambda a, e: a * e, q, egc)
    kd = each(lambda a, gl, x: a * jnp.exp(gl - x), k, gc_last, gc)
    g_last = [jnp.exp(x) for x in gc_last]
    heads = list(range(nh))
    state = [[st_scr[bb, h] for h in heads] for bb in rows_bb]
    ws = [[_dot(jnp.concatenate([sol[bb][h * c:(h + 1) * c, HEAD_DV:], qd[bb][h * c:(h + 1) * c]], axis=0),
                state[bb][h]) for h in heads] for bb in rows_bb]
    v_new = [jnp.concatenate([sol[bb][h * c:(h + 1) * c, :HEAD_DV] - ws[bb][h][:c] for h in heads], axis=0)
             for bb in rows_bb]
    o_intra = each(lambda a, b: jnp.dot(block_diag(a), b.astype(BF16), preferred_element_type=F32), qk, v_new)
    for h in heads:
        rows = slice(h * c, (h + 1) * c)
        for bb in rows_bb:
            o_ref[bb, :, h * HEAD_DV:(h + 1) * HEAD_DV] = (o_intra[bb][rows] + ws[bb][h][c:]).astype(o_ref.dtype)
            st_scr[bb, h] = (state[bb][h] * g_last[bb][h * c:h * c + 1, :]
                             + _dot(kd[bb][rows].T, v_new[bb][rows]))


def _gdn_cumsum_matrix(reverse):
    c, nh = LIN_CHUNK, LIN_HEADS
    p = np.arange(c)
    tri = (p[None, :] >= p[:, None]) if reverse else (p[None, :] <= p[:, None])
    return np.kron(np.eye(nh), tri.astype(np.float32))


def _gdn(qkv, tail, a_row, dt_row, *, n_ctx, reverse, direction, nb):
    q, k, v = qkv
    bsz, n_chunks, n, _ = q.shape
    c = LIN_CHUNK
    n_ctx_chunks = n_ctx // c
    cs = _doubled(_gdn_cumsum_matrix(reverse))

    def ch(s):
        return _scan_tile(s, n_ctx_chunks, n_chunks, reverse)

    const2 = lambda b, s: (0, 0)
    stacked = pl.BlockSpec((nb, None, n, HEAD_DV), lambda b, s: (b, ch(s), 0, 0))
    return pl.pallas_call(
        functools.partial(_gdn_body, reverse=reverse, beta_col=direction * LIN_HEADS,
                          g_col=(2 + direction) * LIN_HEADS, nb=nb),
        grid=(bsz // nb, n_chunks),
        in_specs=[pl.BlockSpec(cs.shape, const2),
                  pl.BlockSpec((1, LANES), const2), pl.BlockSpec((1, LANES), const2),
                  pl.BlockSpec((nb, c, LANES), lambda b, s: (b, ch(s), 0)), stacked, stacked, stacked],
        out_specs=pl.BlockSpec((nb, c, GROUP), lambda b, s: (b, ch(s), 0)),
        out_shape=jax.ShapeDtypeStruct((bsz, n_chunks * c, GROUP), ACT),
        scratch_shapes=[pltpu.VMEM((nb, LIN_HEADS, HEAD_DV, HEAD_DV), F32)],
        compiler_params=_cparams("parallel", "arbitrary"),
        name="gdn_rev" if reverse else "gdn_fwd",
    )(cs, a_row, dt_row, tail, q, k, v)


def _post_body(*refs, even, tm, n_ctx, n_a, n_b):
    x_ref = refs[0]
    a_refs = refs[1:1 + n_a]
    ga_ref = refs[1 + n_a]
    b_refs = refs[2 + n_a:2 + n_a + n_b]
    gb_ref, w_ref, ml_ref, mc_ref, o_ref = refs[2 + n_a + n_b:]
    t = pl.program_id(1)
    a = sum(r[...].astype(F32) for r in a_refs)
    bsum = sum(r[...].astype(F32) for r in b_refs)
    if even:
        ya = a * _gelu_tanh(ga_ref[...].astype(F32))
        yb = _head_norm(bsum, True) * _silu(gb_ref[...].astype(F32))
    else:
        ya = _head_norm(a, False) * _silu(ga_ref[...].astype(F32))
        yb = _head_norm(bsum, False) * _silu(gb_ref[...].astype(F32))
    y = jnp.concatenate([ya.astype(BF16), yb.astype(BF16)], axis=-1)
    mix = jnp.dot(y, w_ref[...], preferred_element_type=F32)
    o_ref[...] = x_ref[...] + _row_select(t, tm, n_ctx, mc_ref, ml_ref, 2) * mix


def _post(x, mod, a_parts, b_parts, p, gate_a_col, gate_b_col, w_out, *, even, tm, n_ctx):
    bsz, l, d = x.shape
    ctx_row = bsz
    tok = lambda b, t: (b, t, 0)
    grp = pl.BlockSpec((None, tm, GROUP), tok)
    return pl.pallas_call(
        functools.partial(_post_body, even=even, tm=tm, n_ctx=n_ctx, n_a=len(a_parts), n_b=len(b_parts)),
        grid=(bsz, l // tm),
        in_specs=[pl.BlockSpec((None, tm, d), tok)] + [grp] * len(a_parts)
                 + [pl.BlockSpec((None, tm, GROUP), lambda b, t: (b, t, gate_a_col))] + [grp] * len(b_parts)
                 + [pl.BlockSpec((None, tm, GROUP), lambda b, t: (b, t, gate_b_col)),
                    pl.BlockSpec(w_out.shape, lambda b, t: (0, 0)),
                    pl.BlockSpec((None, 6, d), lambda b, t: (b, 0, 0)),
                    pl.BlockSpec((None, 6, d), lambda b, t: (ctx_row, 0, 0))],
        out_specs=pl.BlockSpec((None, tm, d), tok),
        out_shape=jax.ShapeDtypeStruct((bsz, l, d), F32),
        compiler_params=_cparams("parallel", "parallel"),
        name="post",
    )(x, *a_parts, p, *b_parts, p, w_out, mod, mod)


def _mlp_body(x_ref, g_ref, ml_ref, mc_ref, w1_ref, w2_ref, fg_ref, o_ref, hid_scr, *, tm, tf, n_ctx, final):
    t = pl.program_id(1)
    if final:
        x_ref = x_ref.at[0]
    y = _rms(x_ref[...], g_ref[...])
    shift = _row_select(t, tm, n_ctx, mc_ref, ml_ref, 3)
    scale = _row_select(t, tm, n_ctx, mc_ref, ml_ref, 4)
    h = (y * (1.0 + scale) + shift).astype(BF16)
    for f in range(w1_ref.shape[1] // tf):
        hid = jnp.maximum(jnp.dot(h, w1_ref[:, f * tf:(f + 1) * tf], preferred_element_type=F32), 0.0)
        hid_scr[:, f * tf:(f + 1) * tf] = (hid * hid).astype(BF16)
    acc = jnp.dot(hid_scr[...], w2_ref[...], preferred_element_type=F32)
    y = x_ref[...] + _row_select(t, tm, n_ctx, mc_ref, ml_ref, 5) * acc
    if final:
        y = _rms(y, fg_ref[...])
    o_ref[...] = y


def _mlp(x, mod, g, w1, w2, final_g, *, tm, tf, n_ctx, final):
    bsz, l, d = x.shape
    dff = w1.shape[1]
    ctx_row = bsz
    tok = lambda b, t: (b, t, 0)
    if final:
        l = l - n_ctx
        x_spec = pl.BlockSpec((pl.Element(1), pl.Element(tm), pl.Element(d)),
                              lambda b, t, off=n_ctx: (b, pl.multiple_of(off + t * tm, SUBLANES), 0))
        n_ctx = 0
    else:
        x_spec = pl.BlockSpec((None, tm, d), tok)
    resident = dict(pipeline_mode=pl.Buffered(1))
    return pl.pallas_call(
        functools.partial(_mlp_body, tm=tm, tf=tf, n_ctx=n_ctx, final=final),
        grid=(bsz, l // tm),
        in_specs=[x_spec,
                  pl.BlockSpec((1, d), lambda b, t: (0, 0)),
                  pl.BlockSpec((None, 6, d), lambda b, t: (b, 0, 0)),
                  pl.BlockSpec((None, 6, d), lambda b, t: (ctx_row, 0, 0)),
                  pl.BlockSpec((d, dff), lambda b, t: (0, 0), **resident),
                  pl.BlockSpec((dff, d), lambda b, t: (0, 0), **resident),
                  pl.BlockSpec((1, d), lambda b, t: (0, 0))],
        out_specs=pl.BlockSpec((None, tm, d), tok),
        out_shape=jax.ShapeDtypeStruct((bsz, l, d), F32),
        scratch_shapes=[pltpu.VMEM((tm, dff), BF16)],
        compiler_params=_cparams("parallel", "parallel"),
        name="mlp",
    )(x, g.reshape(1, d), mod, mod, w1, w2, final_g.reshape(1, d))


def _token_tile(l):
    for div in (4, 8, 16, 17, 34):
        if l % div == 0 and (l // div) % 16 == 0 and l // div <= 1088:
            return l // div
    return l


def kernel(x, c, ctx, c_ctx, ada_w, ada_b, norm1_g, norm2_g, mix_w_out, mlp_w1, mlp_w2, ev_w_in, lru_conv_w,
           lru_conv_b, lru_wa, lru_ba, lru_wx, lru_bx, lru_lambda, ret_log_gamma, od_w_in, hg_lb_logits,
           gdn_conv_w, gdn_a_log, gdn_dt_bias, final_g):
    bsz, seq, d = x.shape
    n_ctx = ctx.shape[1]
    depth = ada_w.shape[0]
    l = n_ctx + seq
    tm = _token_tile(l)
    tf = 1024
    nb = next(n for n in (8, 4, 2, 1) if bsz % n == 0)

    mods = _mods(c, c_ctx, ada_w, ada_b)
    ret_dk = (ev_w_in.shape[2] - 4 * GROUP) // (2 * RET_HEADS)
    cos_t, sin_t = _rope_tables(n_ctx, seq, ret_dk)

    xs = jnp.concatenate([ctx, x], axis=1)
    for layer in range(depth):
        mod = mods[layer]
        w_out = mix_w_out[layer].astype(BF16)
        if layer % 2 == 0:
            e = layer // 2
            (p,) = _proj(xs, mod, norm1_g[layer], ev_w_in[e].astype(BF16), None, tm=tm, n_ctx=n_ctx)
            a_parts = tuple(
                _lru(p, lru_conv_w[e], lru_conv_b[e], _blockdiag_halves(lru_wa[e, dr]).astype(BF16),
                     lru_ba[e, dr], _blockdiag_halves(lru_wx[e, dr]).astype(BF16), lru_bx[e, dr],
                     lru_lambda[e, dr], n_ctx=n_ctx, reverse=rev) for dr, rev in enumerate((False, True)))
            ret = _retention(p, cos_t, sin_t, ret_log_gamma[e, 0], ret_log_gamma[e, 1],
                             n_ctx=n_ctx, reverse=False, nb=nb)
            ret = _retention(p, cos_t, sin_t, ret_log_gamma[e, 1], ret, n_ctx=n_ctx, reverse=True, nb=nb)
            b_parts = (ret,)
            gate_cols = (1, 4)
        else:
            o = layer // 2
            n_main = 9 * GROUP
            w_in = od_w_in[o]
            w_tail = jnp.pad(w_in[:, n_main:], ((0, 0), (0, LANES - (w_in.shape[1] - n_main)))).astype(BF16)
            p, tail = _proj(xs, mod, norm1_g[layer], w_in[:, :n_main].astype(BF16), w_tail, tm=tm, n_ctx=n_ctx)
            qkv = _gdn_prep(p, gdn_conv_w[o], n_ctx=n_ctx)
            mix = []
            for dr, rev in enumerate((False, True)):
                g_col = (2 + dr) * LIN_HEADS
                a_row = jnp.zeros((1, LANES), F32).at[0, g_col:g_col + LIN_HEADS].set(gdn_a_log[o, dr])
                dt_row = jnp.zeros((1, LANES), F32).at[0, g_col:g_col + LIN_HEADS].set(gdn_dt_bias[o, dr])
                mix.append((
                    _gla(p, hg_lb_logits[dr], n_ctx=n_ctx, reverse=rev, layer=o, nb=nb),
                    _gdn(qkv, tail, a_row, dt_row, n_ctx=n_ctx, reverse=rev, direction=dr, nb=nb)))
            a_parts, b_parts = zip(*mix)
            gate_cols = (4, 8)
        xs = _post(xs, mod, a_parts, b_parts, p, gate_cols[0], gate_cols[1], w_out,
                   even=layer % 2 == 0, tm=tm, n_ctx=n_ctx)
        last = layer == depth - 1
        xs = _mlp(xs, mod, norm2_g[layer], mlp_w1[layer].astype(BF16), mlp_w2[layer].astype(BF16), final_g,
                  tm=(_token_tile(seq) if last else tm) // 2, tf=tf, n_ctx=n_ctx, final=last)
    return xs
```

```python
import functools

import numpy as np
import jax
import jax.numpy as jnp
from jax import lax
from jax.experimental import pallas as pl
from jax.experimental.pallas import tpu as pltpu

F32 = jnp.float32
BF16 = jnp.bfloat16

EPS = 1e-6
GRID_W = 64
CONV_W = 4
CONV_LEFT = CONV_W // 2
LRU_BLOCKS = 8
LRU_C = 8.0
RET_HEADS = 4
RET_CHUNK = 256
ROPE_BASE = 10000.0
LIN_HEADS = 4
GLA_CHUNK = 128
LIN_CHUNK = 64
HEAD_DV = 128

LANES = 128
SUBLANES = 8
V7X_VMEM_BYTES = 64 * 1024 * 1024
VMEM_LIMIT = (V7X_VMEM_BYTES * 3) // 4
GROUP = 512

HALO = 2 * SUBLANES
ACT = jnp.bfloat16


def _cparams(*sem):
    return pltpu.CompilerParams(dimension_semantics=sem, vmem_limit_bytes=VMEM_LIMIT)


def _sigmoid(x):
    return 1.0 / (1.0 + jnp.exp(-x))


def _silu(x):
    return x * _sigmoid(x)


def _softplus(x):
    return jnp.maximum(x, 0.0) + jnp.log(1.0 + jnp.exp(-jnp.abs(x)))


def _gelu_tanh(x):
    return 0.5 * x * (1.0 + jnp.tanh(0.7978845608028654 * (x + 0.044715 * x * x * x)))


def _dot(a, b):
    return jnp.dot(a.astype(BF16), b.astype(BF16), preferred_element_type=F32)


def _dot_nt(a, b):
    return lax.dot_general(a.astype(BF16), b.astype(BF16), (((1,), (1,)), ((), ())),
                           preferred_element_type=F32)


def _split_bf16(x):
    hi = x.astype(BF16)
    lo = (x - hi.astype(F32)).astype(BF16)
    return hi, lo


def _dot_sel(m2, x):
    hi, lo = _split_bf16(x)
    return jnp.dot(m2, jnp.concatenate([hi, lo], axis=0), preferred_element_type=F32)


def _doubled(m):
    return jnp.asarray(np.concatenate([m, m], axis=1), BF16)


def _dot3(a, b):
    ah, al = _split_bf16(a)
    bh, bl = _split_bf16(b)
    return (jnp.dot(ah, bh, preferred_element_type=F32) + jnp.dot(ah, bl, preferred_element_type=F32)
            + jnp.dot(al, bh, preferred_element_type=F32))


def _row_select(t, tm, n_ctx, mc_ref, ml_ref, idx):
    rows = t * tm + lax.broadcasted_iota(jnp.int32, (tm, 1), 0)
    return jnp.where(rows < n_ctx, mc_ref[idx:idx + 1, :], ml_ref[idx:idx + 1, :])


def _rms(x, g):
    return x * lax.rsqrt(jnp.mean(x * x, axis=-1, keepdims=True) + EPS) * g


def _head_norm(o, center):
    outs = []
    for h in range(o.shape[-1] // HEAD_DV):
        oh = o[:, h * HEAD_DV:(h + 1) * HEAD_DV]
        if center:
            oh = oh - jnp.mean(oh, axis=-1, keepdims=True)
        outs.append(oh * lax.rsqrt(jnp.mean(oh * oh, axis=-1, keepdims=True) + EPS))
    return jnp.concatenate(outs, axis=-1)


def _scan_tile(step, n_first, n_total, reverse):
    if not reverse:
        return step
    return jnp.where(step < n_first, n_first - 1 - step, n_total - 1 - (step - n_first))


def _mods_body(s_ref, w_ref, b_ref, o_ref):
    s = _silu(s_ref[...])
    o_ref[...] = _dot3(s, w_ref[...]) + b_ref[...]


def _mods(c, c_ctx, ada_w, ada_b):
    depth, d, n6 = ada_w.shape
    bsz = c.shape[0]
    rows = -(-(bsz + 1) // SUBLANES) * SUBLANES
    s = jnp.zeros((rows, d), F32).at[:bsz].set(c).at[bsz].set(c_ctx)
    tn = 1024
    out = pl.pallas_call(
        _mods_body,
        grid=(depth, n6 // tn),
        in_specs=[pl.BlockSpec((rows, d), lambda l, n: (0, 0)),
                  pl.BlockSpec((None, d, tn), lambda l, n: (l, 0, n)),
                  pl.BlockSpec((None, 1, tn), lambda l, n: (l, 0, n))],
        out_specs=pl.BlockSpec((None, rows, tn), lambda l, n: (l, 0, n)),
        out_shape=jax.ShapeDtypeStruct((depth, rows, n6), F32),
        compiler_params=_cparams("parallel", "parallel"),
        name="mods",
    )(s, ada_w, ada_b.reshape(depth, 1, n6))
    return out.reshape(depth, rows, 6, d)


def _proj_body(*refs, tm, n_ctx, has_tail):
    if has_tail:
        x_ref, ml_ref, mc_ref, g_ref, w_ref, wt_ref, o_ref, ot_ref = refs
    else:
        x_ref, ml_ref, mc_ref, g_ref, w_ref, o_ref = refs
    t = pl.program_id(1)
    y = _rms(x_ref[...], g_ref[...])
    shift = _row_select(t, tm, n_ctx, mc_ref, ml_ref, 0)
    scale = _row_select(t, tm, n_ctx, mc_ref, ml_ref, 1)
    h = (y * (1.0 + scale) + shift).astype(BF16)
    o_ref[...] = jnp.dot(h, w_ref[...], preferred_element_type=F32).astype(o_ref.dtype)
    if has_tail:
        ot_ref[...] = jnp.dot(h, wt_ref[...], preferred_element_type=F32)


def _proj(x, mod, g, w, w_tail, *, tm, n_ctx):
    bsz, l, d = x.shape
    n_out = w.shape[1]
    ctx_row = bsz
    resident = dict(pipeline_mode=pl.Buffered(1))
    in_specs = [pl.BlockSpec((None, tm, d), lambda b, t: (b, t, 0)),
                pl.BlockSpec((None, 6, d), lambda b, t: (b, 0, 0)),
                pl.BlockSpec((None, 6, d), lambda b, t: (ctx_row, 0, 0)),
                pl.BlockSpec((1, d), lambda b, t: (0, 0)),
                pl.BlockSpec((d, n_out), lambda b, t: (0, 0), **resident)]
    out_specs = [pl.BlockSpec((None, tm, n_out), lambda b, t: (b, t, 0))]
    out_shape = [jax.ShapeDtypeStruct((bsz, l, n_out), ACT)]
    args = [x, mod, mod, g.reshape(1, d), w]
    if w_tail is not None:
        in_specs.append(pl.BlockSpec((d, LANES), lambda b, t: (0, 0), **resident))
        out_specs.append(pl.BlockSpec((None, tm, LANES), lambda b, t: (b, t, 0)))
        out_shape.append(jax.ShapeDtypeStruct((bsz, l, LANES), F32))
        args.append(w_tail)
    outs = pl.pallas_call(
        functools.partial(_proj_body, tm=tm, n_ctx=n_ctx, has_tail=w_tail is not None),
        grid=(bsz, l // tm),
        in_specs=in_specs, out_specs=out_specs, out_shape=out_shape,
        compiler_params=_cparams("parallel", "parallel"),
        name="proj",
    )(*args)
    return outs


def _lru_body(sm_ref, se_ref, x_ref, xp_ref, xn_ref, cw_ref, cb_ref, wa_ref, ba_ref, wx_ref, bx_ref, lam_ref,
              *rest, tl, n_ctx_tiles, n_tiles, reverse):
    acc_ref = rest[0] if len(rest) == 5 else None
    o_ref, a_scr, b_scr, h_scr = rest[-4:]
    step = pl.program_id(0)
    tile = _scan_tile(step, n_ctx_tiles, n_tiles, reverse)
    bsz = x_ref.shape[0]
    width = x_ref.shape[2]
    half = width // 2
    seg_first = jnp.logical_or(tile == 0, tile == n_ctx_tiles)
    seg_last = jnp.logical_or(tile == n_ctx_tiles - 1, tile == n_tiles - 1)
    keep_prev = jnp.where(seg_first, 0.0, 1.0)
    keep_next = jnp.where(seg_last, 0.0, 1.0)
    neg_c_sp = -LRU_C * _softplus(-lam_ref[...])

    @pl.when(step == 0)
    def _():
        h_scr[...] = jnp.zeros_like(h_scr)

    pair = 2 if bsz % 2 == 0 else 1

    def block_dot(ub, w_ref):
        return jnp.concatenate([jnp.dot(ub[:, :half], w_ref[0], preferred_element_type=F32),
                                jnp.dot(ub[:, half:], w_ref[1], preferred_element_type=F32)], axis=-1)

    def gates(bp, carry):
        bs = [bp * pair + n for n in range(pair)]
        taps = [_conv_taps(sm_ref, se_ref, x_ref[b], xp_ref[b], xn_ref[b], keep_prev, keep_next, tl) for b in bs]
        u = [cb_ref[...] + sum(cw_ref[k:k + 1, :] * tp[k] for k in range(CONV_W)) for tp in taps]
        ub = [x.astype(BF16) for x in u]
        r_pre = [block_dot(x, wa_ref) + ba_ref[...] for x in ub]
        i_pre = [block_dot(x, wx_ref) + bx_ref[...] for x in ub]
        a = [jnp.exp(neg_c_sp * _sigmoid(x)) for x in r_pre]
        for n, b in enumerate(bs):
            a_scr[b] = a[n]
            b_scr[b] = jnp.sqrt(1.0 - a[n] * a[n]) * (_sigmoid(i_pre[n]) * u[n])
        return carry

    lax.fori_loop(0, bsz // pair, gates, 0)

    def scan_rows(j, hs):
        idx = (tl - 1 - j) if reverse else j
        new = []
        for b in range(bsz):
            h = a_scr[b, pl.ds(idx, 1), :] * hs[b] + b_scr[b, pl.ds(idx, 1), :]
            b_scr[b, pl.ds(idx, 1), :] = h
            new.append(h)
        return tuple(new)

    hs = tuple(h_scr[b, 0:1, :] for b in range(bsz))
    hs = lax.fori_loop(0, tl, scan_rows, hs)
    for b in range(bsz):
        h_scr[b, 0:1, :] = hs[b]
    if acc_ref is None:
        o_ref[...] = b_scr[...].astype(o_ref.dtype)
    else:
        o_ref[...] = (b_scr[...] + acc_ref[...].astype(F32)).astype(o_ref.dtype)


def _lru(p, conv_w, conv_b, wa, ba, wx, bx, lam, acc=None, *, n_ctx, reverse, tl=256):
    bsz, l, _ = p.shape
    width = GROUP
    n_tiles = l // tl
    n_ctx_tiles = n_ctx // tl
    hb = tl // HALO
    n_hblocks = l // HALO

    def tile_of(s):
        return _scan_tile(s, n_ctx_tiles, n_tiles, reverse)

    vec = pl.BlockSpec((1, width), lambda s: (0, 0))
    mat = pl.BlockSpec((2, width // 2, width // 2), lambda s: (0, 0, 0))
    sm, se = (jnp.asarray(m, p.dtype) for m in _shift_matrices(tl))
    out_block = pl.BlockSpec((bsz, tl, width), lambda s: (0, tile_of(s), 0))
    extra = () if acc is None else (acc,)
    return pl.pallas_call(
        functools.partial(_lru_body, tl=tl, n_ctx_tiles=n_ctx_tiles, n_tiles=n_tiles, reverse=reverse),
        grid=(n_tiles,),
        in_specs=[pl.BlockSpec(sm.shape, lambda s: (0, 0)), pl.BlockSpec(se.shape, lambda s: (0, 0)),
                  pl.BlockSpec((bsz, tl, width), lambda s: (0, tile_of(s), 0)),
                  pl.BlockSpec((bsz, HALO, width), lambda s: (0, jnp.maximum(tile_of(s) * hb - 1, 0), 0)),
                  pl.BlockSpec((bsz, HALO, width),
                               lambda s: (0, jnp.minimum((tile_of(s) + 1) * hb, n_hblocks - 1), 0)),
                  pl.BlockSpec((CONV_W, width), lambda s: (0, 0)), vec, mat, vec, mat, vec, vec]
                 + [out_block] * len(extra),
        out_specs=out_block,
        out_shape=jax.ShapeDtypeStruct((bsz, l, width), ACT),
        scratch_shapes=[pltpu.VMEM((bsz, tl, width), F32),
                        pltpu.VMEM((bsz, tl, width), F32),
                        pltpu.VMEM((bsz, SUBLANES, width), F32)],
        compiler_params=_cparams("arbitrary"),
        name="lru_rev" if reverse else "lru_fwd",
    )(sm, se, p, p, p, conv_w, conv_b.reshape(1, width), wa, ba.reshape(1, width), wx, bx.reshape(1, width),
      lam.reshape(1, width), *extra)


def _blockdiag_halves(w):
    nb, n, _ = w.shape
    per = nb // 2
    w = w.reshape(2, per, n, n)
    eye = jnp.eye(per, dtype=w.dtype)
    out = w[:, :, :, None, :] * eye[None, :, None, :, None]
    return out.reshape(2, per * n, per * n)


def _ret_body(*refs, reverse, dk, nb):
    if reverse:
        lg_ref, q_ref, k_ref, v_ref, cos_ref, sin_ref, acc_ref, o_ref, qd_scr, kd_scr, cd_scr, st_scr = refs
    else:
        lg_ref, lgo_ref, q_ref, k_ref, v_ref, cos_ref, sin_ref, o_ref, dec_scr, qd_scr, kd_scr, cd_scr, st_scr = refs
    c = RET_CHUNK
    step = pl.program_id(1)

    @pl.when(step == 0)
    def _():
        st_scr[...] = jnp.zeros_like(st_scr)
        row = lax.broadcasted_iota(jnp.int32, (c, HEAD_DV), 0)
        pos = ((c - 1 - row) if reverse else row).astype(F32)
        for h in range(RET_HEADS):
            lg = lg_ref[h]
            qd_scr[h] = jnp.exp(lg * (pos + 1.0))
            kd_scr[h] = jnp.exp(lg * (c - 1.0 - pos))
            cd_scr[h] = jnp.exp(jnp.full((SUBLANES, LANES), c, F32) * lg)
        if not reverse:
            i = lax.broadcasted_iota(jnp.int32, (c, c), 0)
            j = lax.broadcasted_iota(jnp.int32, (c, c), 1)
            below = jnp.maximum(i - j, 0).astype(F32)
            above = jnp.maximum(j - i, 0).astype(F32)
            for h in range(RET_HEADS):
                dec_scr[h] = (jnp.where(i >= j, jnp.exp(lg_ref[h] * below), 0.0)
                              + jnp.where(j >= i, jnp.exp(lgo_ref[h] * above), 0.0))

    def rope(x):
        width = x.shape[-1]
        lane = lax.broadcasted_iota(jnp.int32, x.shape, 1)
        first = (lane & (dk - 1)) < dk // 2
        rot = jnp.where(first, pltpu.roll(x, width - dk // 2, axis=1), pltpu.roll(x, dk // 2, axis=1))
        return x * cos_ref[...] + rot * sin_ref[...]

    q = [rope(q_ref[bb].astype(F32)) for bb in range(nb)]
    k = [rope(k_ref[bb].astype(F32)) * (dk ** -0.5) for bb in range(nb)]
    pairs = [(bb, h) for bb in range(nb) for h in range(RET_HEADS)]
    qh = [q[bb][:, h * dk:(h + 1) * dk] for bb, h in pairs]
    kh = [k[bb][:, h * dk:(h + 1) * dk] for bb, h in pairs]
    vh = [v_ref[bb, :, h * HEAD_DV:(h + 1) * HEAD_DV] for bb, h in pairs]
    state = [st_scr[bb, h] for bb, h in pairs]
    o_x = [_dot(qh[i], state[i]) * qd_scr[h] for i, (bb, h) in enumerate(pairs)]
    if reverse:
        o_in = [acc_ref[bb, :, h * HEAD_DV:(h + 1) * HEAD_DV].astype(F32) for bb, h in pairs]
    else:
        s = [_dot_nt(qh[i], kh[i]) * dec_scr[h] for i, (bb, h) in enumerate(pairs)]
        o_in = [_dot(s[i], vh[i]) for i in range(len(pairs))]
    kv = [_dot((kh[i] * kd_scr[h][:, :dk]).T, vh[i]) for i, (bb, h) in enumerate(pairs)]
    for i, (bb, h) in enumerate(pairs):
        st_scr[bb, h] = state[i] * cd_scr[h][0:1, :] + kv[i]
        o_ref[bb, :, h * HEAD_DV:(h + 1) * HEAD_DV] = (o_in[i] + o_x[i]).astype(o_ref.dtype)


def _retention(p, cos_t, sin_t, log_gamma, other, *, n_ctx, reverse, nb):
    bsz, l, _ = p.shape
    c = RET_CHUNK
    qk_w = cos_t.shape[1]
    dk = qk_w // RET_HEADS
    n_chunks = l // c
    n_ctx_chunks = n_ctx // c

    def ch(s):
        return _scan_tile(s, n_ctx_chunks, n_chunks, reverse)

    q_col = (2 * GROUP) // qk_w
    smem = pl.BlockSpec(memory_space=pltpu.SMEM)
    out_block = pl.BlockSpec((nb, c, GROUP), lambda b, s: (b, ch(s), 0))
    data_specs = [pl.BlockSpec((nb, c, qk_w), lambda b, s: (b, ch(s), q_col)),
                  pl.BlockSpec((nb, c, qk_w), lambda b, s: (b, ch(s), q_col + 1)),
                  pl.BlockSpec((nb, c, GROUP), lambda b, s: (b, ch(s), 3)),
                  pl.BlockSpec((c, qk_w), lambda b, s: (ch(s), 0)),
                  pl.BlockSpec((c, qk_w), lambda b, s: (ch(s), 0))]
    scratch = [pltpu.VMEM((RET_HEADS, c, HEAD_DV), F32),
               pltpu.VMEM((RET_HEADS, c, HEAD_DV), F32),
               pltpu.VMEM((RET_HEADS, SUBLANES, LANES), F32),
               pltpu.VMEM((nb, RET_HEADS, dk, HEAD_DV), F32)]
    if reverse:
        in_specs = [smem] + data_specs + [out_block]
        args = (log_gamma, p, p, p, cos_t, sin_t, other)
    else:
        in_specs = [smem, smem] + data_specs
        args = (log_gamma, other, p, p, p, cos_t, sin_t)
        scratch = [pltpu.VMEM((RET_HEADS, c, c), F32)] + scratch
    return pl.pallas_call(
        functools.partial(_ret_body, reverse=reverse, dk=dk, nb=nb),
        grid=(bsz // nb, n_chunks),
        in_specs=in_specs,
        out_specs=out_block,
        out_shape=jax.ShapeDtypeStruct((bsz, l, GROUP), ACT),
        scratch_shapes=scratch,
        compiler_params=_cparams("parallel", "arbitrary"),
        name="ret_rev" if reverse else "ret_fwd",
    )(*args)


def _rope_tables(n_ctx, seq, dk):
    n_freq = dk // 4
    inv = jnp.power(ROPE_BASE, -jnp.arange(n_freq, dtype=F32) / n_freq)
    rows = seq // GRID_W
    r = jnp.arange(rows, dtype=F32)
    col = jnp.arange(GRID_W, dtype=F32)
    row_ang = jnp.broadcast_to(r[:, None, None] * inv, (rows, GRID_W, n_freq))
    col_ang = jnp.broadcast_to(col[None, :, None] * inv, (rows, GRID_W, n_freq))
    ang = jnp.concatenate([row_ang, col_ang], axis=-1).reshape(rows * GRID_W, 2 * n_freq)
    cos, sin = jnp.cos(ang), jnp.sin(ang)
    cos_h = jnp.concatenate([cos, cos], axis=-1)
    sin_h = jnp.concatenate([-sin, sin], axis=-1)
    cos_t = jnp.concatenate([jnp.ones((n_ctx, dk), F32), cos_h], axis=0)
    sin_t = jnp.concatenate([jnp.zeros((n_ctx, dk), F32), sin_h], axis=0)
    return jnp.tile(cos_t, (1, RET_HEADS)), jnp.tile(sin_t, (1, RET_HEADS))


_GLA_LEVELS = tuple(GLA_CHUNK >> (i + 1) for i in range(GLA_CHUNK.bit_length() - 1))


def _gla_matrices(reverse):
    c = GLA_CHUNK
    p = np.arange(c)
    mats = [(p[None, :] <= p[:, None]), (p[None, :] > p[:, None])]
    for s in _GLA_LEVELS:
        bs = (p // s) * s
        right = ((p // s) % 2 == 1)
        m_r = (p[None, :] > bs[:, None]) & (p[None, :] <= p[:, None])
        m_l = (p[None, :] > p[:, None]) & (p[None, :] <= (bs + s)[:, None])
        mats.append(np.where(right[:, None], m_r, m_l))
    m = np.stack(mats).astype(np.float32)
    if reverse:
        m = m[:, ::-1, ::-1]
    return m.reshape(len(mats) * c, c)


def _gla_body(m_ref, lb_ref, q_ref, f_ref, v_ref, *rest, reverse, layer, nb):
    acc_ref = rest[0] if len(rest) == 3 else None
    o_ref, st_scr = rest[-2:]
    c = GLA_CHUNK
    step = pl.program_id(1)

    @pl.when(step == 0)
    def _():
        st_scr[...] = jnp.zeros_like(st_scr)

    lg = lb_ref[...]
    e = jnp.exp(lg - jnp.max(lg, axis=0, keepdims=True))
    sm = e / jnp.sum(e, axis=0, keepdims=True)
    lb = jnp.sum(sm[0:layer + 1, :], axis=0, keepdims=True) - sm[0:1, :]

    ri = lax.broadcasted_iota(jnp.int32, (c, 1), 0)
    i2 = lax.broadcasted_iota(jnp.int32, (c, c), 0)
    j2 = lax.broadcasted_iota(jnp.int32, (c, c), 1)
    if reverse:
        ri, i2, j2 = c - 1 - ri, c - 1 - i2, c - 1 - j2
    last = 0 if reverse else c - 1
    width = q_ref.shape[-1]
    nh = width // HEAD_DV

    rows_bb = list(range(nb))
    pairs = [(bb, h) for bb in rows_bb for h in range(nh)]

    def head(x, h):
        return x[:, h * HEAD_DV:(h + 1) * HEAD_DV]

    f = [lb + (1.0 - lb) * _sigmoid(f_ref[bb].astype(F32)) for bb in rows_bb]
    k = [1.0 - x for x in f]
    q = [_silu(q_ref[bb].astype(F32)) for bb in rows_bb]
    ex = [_dot_sel(m_ref[...], jnp.log(x)) for x in f]
    a_last = [jnp.exp(x[last:last + 1, :]) for x in ex]
    qb = [x.astype(BF16) for x in q]
    kb = [x.astype(BF16) for x in k]
    qa = [qb[bb] * jnp.exp(ex[bb][0:c, :].astype(BF16)) for bb in rows_bb]
    kl = [kb[bb] * jnp.exp(ex[bb][c:2 * c, :].astype(BF16)) for bb in rows_bb]
    scores = [jnp.where(i2 == j2, _dot_nt(head(qb[bb], h), head(kb[bb], h)), 0.0) for bb, h in pairs]
    for lvl, s in enumerate(_GLA_LEVELS):
        sh = s.bit_length() - 1
        right = jnp.broadcast_to(((ri >> sh) & 1).astype(F32), (c, width)).astype(BF16) > 0
        x = [jnp.where(right, qb[bb], kb[bb]) * jnp.exp(ex[bb][(2 + lvl) * c:(3 + lvl) * c, :].astype(BF16))
             for bb in rows_bb]
        bi, bj = i2 >> sh, j2 >> sh
        valid = bj == jnp.where((bi & 1) == 1, bi - 1, -1)
        gram = [_dot_nt(head(x[bb], h), head(x[bb], h)) for bb, h in pairs]
        scores = [jnp.where(valid, gm, sc) for sc, gm in zip(scores, gram)]

    state_t = [st_scr[bb, h] for bb, h in pairs]
    o_intra = [_dot(scores[i], head(v_ref[bb], h)) for i, (bb, h) in enumerate(pairs)]
    o_inter = [_dot_nt(head(qa[bb], h), state_t[i]) for i, (bb, h) in enumerate(pairs)]
    kv = [_dot(head(v_ref[bb], h).astype(F32).T, head(kl[bb], h)) for bb, h in pairs]
    for i, (bb, h) in enumerate(pairs):
        o = o_intra[i] + o_inter[i]
        if acc_ref is not None:
            o = o + acc_ref[bb, :, h * HEAD_DV:(h + 1) * HEAD_DV].astype(F32)
        o_ref[bb, :, h * HEAD_DV:(h + 1) * HEAD_DV] = o.astype(o_ref.dtype)
        st_scr[bb, h] = state_t[i] * head(a_last[bb], h) + kv[i]


def _gla(p, lb_logits, acc=None, *, n_ctx, reverse, layer, nb):
    bsz, l, _ = p.shape
    c = GLA_CHUNK
    n_chunks = l // c
    n_ctx_chunks = n_ctx // c
    m = _doubled(_gla_matrices(reverse))

    def ch(s):
        return _scan_tile(s, n_ctx_chunks, n_chunks, reverse)

    f_col = 2 if reverse else 1
    out_block = pl.BlockSpec((nb, c, GROUP), lambda b, s: (b, ch(s), 0))
    extra = () if acc is None else (acc,)
    return pl.pallas_call(
        functools.partial(_gla_body, reverse=reverse, layer=layer, nb=nb),
        grid=(bsz // nb, n_chunks),
        in_specs=[pl.BlockSpec(m.shape, lambda b, s: (0, 0)),
                  pl.BlockSpec(lb_logits.shape, lambda b, s: (0, 0)),
                  pl.BlockSpec((nb, c, GROUP), lambda b, s: (b, ch(s), 0)),
                  pl.BlockSpec((nb, c, GROUP), lambda b, s: (b, ch(s), f_col)),
                  pl.BlockSpec((nb, c, GROUP), lambda b, s: (b, ch(s), 3))] + [out_block] * len(extra),
        out_specs=out_block,
        out_shape=jax.ShapeDtypeStruct((bsz, l, GROUP), ACT),
        scratch_shapes=[pltpu.VMEM((nb, LIN_HEADS, HEAD_DV, HEAD_DV), F32)],
        compiler_params=_cparams("parallel", "arbitrary"),
        name="gla_rev" if reverse else "gla_fwd",
    )(m, lb_logits, p, p, p, *extra)


def _shift_matrices(rows):
    offs = [t - CONV_LEFT for t in range(CONV_W) if t != CONV_LEFT]
    main = np.zeros((len(offs) * rows, rows), np.float32)
    edge = np.zeros((len(offs) * SUBLANES, 2 * HALO), np.float32)
    for n, off in enumerate(offs):
        for i in range(rows):
            j = i + off
            if 0 <= j < rows:
                main[n * rows + i, j] = 1.0
            elif j < 0:
                edge[n * SUBLANES + i, HALO + j] = 1.0
            else:
                edge[n * SUBLANES + i - (rows - SUBLANES), HALO + j - rows] = 1.0
    return main, edge


def _conv_taps(sm_ref, se_ref, x, xp, xn, keep_prev, keep_next, rows):
    sh = jnp.dot(sm_ref[...], x, preferred_element_type=F32)
    halo = jnp.concatenate([xp * keep_prev.astype(xp.dtype), xn * keep_next.astype(xn.dtype)], axis=0)
    fix = jnp.dot(se_ref[...], halo, preferred_element_type=F32)
    taps, n = [], 0
    for t in range(CONV_W):
        off = t - CONV_LEFT
        if off == 0:
            taps.append(x.astype(F32))
            continue
        s = sh[n * rows:(n + 1) * rows, :]
        f = fix[n * SUBLANES:(n + 1) * SUBLANES, :]
        if off < 0:
            s = jnp.concatenate([s[0:SUBLANES] + f, s[SUBLANES:]], axis=0)
        else:
            s = jnp.concatenate([s[:rows - SUBLANES], s[rows - SUBLANES:] + f], axis=0)
        taps.append(s)
        n += 1
    return taps


def _gdn_prep_body(sm_ref, se_ref, cw_ref, q_ref, qp_ref, qn_ref, k_ref, kp_ref, kn_ref, v_ref, vp_ref, vn_ref,
                   qo_ref, ko_ref, vo_ref, *, tp, n_ctx_tiles, n_tiles):
    c = LIN_CHUNK
    nh = LIN_HEADS
    tile = pl.program_id(1)
    seg_first = jnp.logical_or(tile == 0, tile == n_ctx_tiles)
    seg_last = jnp.logical_or(tile == n_ctx_tiles - 1, tile == n_tiles - 1)
    keep_prev = jnp.where(seg_first, 0.0, 1.0)
    keep_next = jnp.where(seg_last, 0.0, 1.0)

    all_taps = [_conv_taps(sm_ref, se_ref, x[...], xp[...], xn[...], keep_prev, keep_next, tp)
                for x, xp, xn in ((q_ref, qp_ref, qn_ref), (k_ref, kp_ref, kn_ref), (v_ref, vp_ref, vn_ref))]

    def conv_silu(col):
        taps = all_taps[col]
        u = cw_ref[0:1, col * GROUP:(col + 1) * GROUP] * taps[0]
        for t in range(1, CONV_W):
            u = u + cw_ref[t:t + 1, col * GROUP:(col + 1) * GROUP] * taps[t]
        return _silu(u)

    def l2n(x):
        return x * lax.rsqrt(jnp.sum(x * x, axis=-1, keepdims=True) + EPS)

    def emit(out_ref, x, norm, scale):
        for h in range(nh):
            xh = x[:, h * HEAD_DV:(h + 1) * HEAD_DV]
            if norm:
                xh = l2n(xh) * scale
            for j in range(tp // c):
                out_ref[j, h * c:(h + 1) * c, :] = xh[j * c:(j + 1) * c, :].astype(out_ref.dtype)

    emit(qo_ref, conv_silu(0), True, HEAD_DV ** -0.5)
    emit(ko_ref, conv_silu(1), True, 1.0)
    emit(vo_ref, conv_silu(2), False, 1.0)


def _gdn_prep(p, conv_w, *, n_ctx, tp=256):
    bsz, l, _ = p.shape
    c = LIN_CHUNK
    n_tiles = l // tp
    hb = tp // HALO
    n_hblocks = l // HALO

    def triple(col):
        return [pl.BlockSpec((None, tp, GROUP), lambda b, t: (b, t, col)),
                pl.BlockSpec((None, HALO, GROUP), lambda b, t: (b, jnp.maximum(t * hb - 1, 0), col)),
                pl.BlockSpec((None, HALO, GROUP), lambda b, t: (b, jnp.minimum((t + 1) * hb, n_hblocks - 1), col))]

    out_spec = pl.BlockSpec((None, tp // c, LIN_HEADS * c, HEAD_DV), lambda b, t: (b, t, 0, 0))
    out_sds = jax.ShapeDtypeStruct((bsz, l // c, LIN_HEADS * c, HEAD_DV), ACT)
    sm, se = (jnp.asarray(m, p.dtype) for m in _shift_matrices(tp))
    const2 = lambda b, t: (0, 0)
    return pl.pallas_call(
        functools.partial(_gdn_prep_body, tp=tp, n_ctx_tiles=n_ctx // tp, n_tiles=n_tiles),
        grid=(bsz, n_tiles),
        in_specs=[pl.BlockSpec(sm.shape, const2), pl.BlockSpec(se.shape, const2), pl.BlockSpec(conv_w.shape, const2)]
                 + triple(5) + triple(6) + triple(7),
        out_specs=[out_spec] * 3, out_shape=[out_sds] * 3,
        compiler_params=_cparams("parallel", "parallel"),
        name="gdn_prep",
    )(sm, se, conv_w, *([p] * 9))


def _gdn_body(cs_ref, arow_ref, dtrow_ref, t_ref, q_ref, k_ref, v_ref, *rest, reverse, beta_col, g_col, nb):
    acc_ref = rest[0] if len(rest) == 3 else None
    o_ref, st_scr = rest[-2:]
    c = LIN_CHUNK
    nh = LIN_HEADS
    n = nh * c
    step = pl.program_id(1)

    @pl.when(step == 0)
    def _():
        st_scr[...] = jnp.zeros_like(st_scr)

    i2 = lax.broadcasted_iota(jnp.int32, (c, n), 0)
    lane = lax.broadcasted_iota(jnp.int32, (c, n), 1)
    j2 = lane & (c - 1)
    grp = lane >> 6
    mask = (j2 >= i2) if reverse else (j2 <= i2)
    same32 = (i2 >> 5) == (j2 >> 5)
    same16 = (i2 >> 4) == (j2 >> 4)
    eye = jnp.where(i2 == j2, 1.0, 0.0)
    grp_rows = [jnp.where((lax.broadcasted_iota(jnp.int32, (1, n), 1) >> 6) == h, 1.0, 0.0).astype(BF16)
                for h in range(nh)]
    last = 0 if reverse else c - 1

    def each(fn, *lists):
        return [fn(*args) for args in zip(*lists)]

    def head_bcast(x, col):
        return jnp.concatenate([jnp.broadcast_to(x[:, col + h:col + h + 1], (c, HEAD_DV)) for h in range(nh)], axis=0)

    def side_by_side(x):
        out = x[(nh - 1) * c:nh * c, :]
        for h in range(nh - 2, -1, -1):
            out = jnp.where(grp == h, x[h * c:(h + 1) * c, :], out)
        return out

    def block_diag(x):
        xb = x.astype(BF16)
        return jnp.concatenate([xb * grp_rows[h] for h in range(nh)], axis=0)

    def dot_sbs(x, y_bd):
        return jnp.dot(x.astype(BF16), y_bd, preferred_element_type=F32)

    rows_bb = list(range(nb))
    q = [q_ref[bb].astype(F32) for bb in rows_bb]
    k = [k_ref[bb].astype(F32) for bb in rows_bb]
    v = [v_ref[bb].astype(F32) for bb in rows_bb]

    tail = [t_ref[bb] for bb in rows_bb]
    beta = [head_bcast(_sigmoid(x), beta_col) for x in tail]
    g = [head_bcast(-jnp.exp(arow_ref[...]) * _softplus(x + dtrow_ref[...]), g_col) for x in tail]
    gc = [_dot_sel(cs_ref[...], x) for x in g]
    gc_last = [jnp.concatenate([jnp.broadcast_to(x[h * c + last:h * c + last + 1, :], (c, HEAD_DV))
                                for h in range(nh)], axis=0) for x in gc]
    diff = [side_by_side(jnp.concatenate([x, x], axis=1)) - jnp.transpose(x)[0:1, :] for x in gc]
    gam = [jnp.where(mask, jnp.exp(jnp.where(mask, x, 0.0)), 0.0) for x in diff]

    kb = each(lambda a, b: a * b, k, beta)
    kq = each(lambda a, b, kx: _dot_nt(jnp.concatenate([a, b], axis=0), kx), kb, q, k)
    mm = each(lambda x, gm: jnp.where(i2 == j2, 0.0, side_by_side(x[:n]) * gm), kq, gam)
    qk = each(lambda x, gm: side_by_side(x[n:]) * gm, kq, gam)
    egc = [jnp.exp(x) for x in gc]
    rhs = each(lambda a, b, kbx, e: jnp.concatenate([a * b, kbx * e], axis=1), v, beta, kb, egc)

    md = [jnp.where(same16, x, 0.0) for x in mm]
    m2 = each(lambda a: dot_sbs(a, block_diag(a)), md)
    t = [eye - x for x in md]
    tm = each(lambda a, b: dot_sbs(jnp.concatenate([a, b], axis=0), block_diag(b)), t, m2)
    t = each(lambda a, x: a + x[:c], t, tm)
    m4 = [x[c:] for x in tm]
    tm = each(lambda a, b: dot_sbs(jnp.concatenate([a, b], axis=0), block_diag(b)), t, m4)
    t = each(lambda a, x: a + x[:c], t, tm)
    t = each(lambda a, x: a + dot_sbs(a, block_diag(x[c:])), t, tm)
    c32 = each(lambda a, b: jnp.where(same32, a - b, 0.0), mm, md)
    ct = each(lambda a, b: dot_sbs(a, block_diag(b)), c32, t)
    t = each(lambda a, b: a - dot_sbs(a, block_diag(b)), t, ct)
    c64 = [jnp.where(same32, 0.0, x) for x in mm]
    ct = each(lambda a, b: dot_sbs(a, block_diag(b)), c64, t)
    t = each(lambda a, b: a - dot_sbs(a, block_diag(b)), t, ct)
    sol = each(lambda a, b: jnp.dot(block_diag(a), b.astype(BF16), preferred_element_type=F32), t, rhs)

    qd = each(lambda a, e: a * e, q, egc)
    kd = each(lambda a, gl, x: a * jnp.exp(gl - x), k, gc_last, gc)
    g_last = [jnp.exp(x) for x in gc_last]
    heads = list(range(nh))
    state = [[st_scr[bb, h] for h in heads] for bb in rows_bb]
    ws = [[_dot(jnp.concatenate([sol[bb][h * c:(h + 1) * c, HEAD_DV:], qd[bb][h * c:(h + 1) * c]], axis=0),
                state[bb][h]) for h in heads] for bb in rows_bb]
    v_new = [jnp.concatenate([sol[bb][h * c:(h + 1) * c, :HEAD_DV] - ws[bb][h][:c] for h in heads], axis=0)
             for bb in rows_bb]
    o_intra = each(lambda a, b: jnp.dot(block_diag(a), b.astype(BF16), preferred_element_type=F32), qk, v_new)
    for h in heads:
        rows = slice(h * c, (h + 1) * c)
        for bb in rows_bb:
            o = o_intra[bb][rows] + ws[bb][h][c:]
            if acc_ref is not None:
                o = o + acc_ref[bb, :, h * HEAD_DV:(h + 1) * HEAD_DV].astype(F32)
            o_ref[bb, :, h * HEAD_DV:(h + 1) * HEAD_DV] = o.astype(o_ref.dtype)
            st_scr[bb, h] = (state[bb][h] * g_last[bb][h * c:h * c + 1, :]
                             + _dot(kd[bb][rows].T, v_new[bb][rows]))


def _gdn_cumsum_matrix(reverse):
    c, nh = LIN_CHUNK, LIN_HEADS
    p = np.arange(c)
    tri = (p[None, :] >= p[:, None]) if reverse else (p[None, :] <= p[:, None])
    return np.kron(np.eye(nh), tri.astype(np.float32))


def _gdn(qkv, tail, a_row, dt_row, acc=None, *, n_ctx, reverse, direction, nb):
    q, k, v = qkv
    bsz, n_chunks, n, _ = q.shape
    c = LIN_CHUNK
    n_ctx_chunks = n_ctx // c
    cs = _doubled(_gdn_cumsum_matrix(reverse))

    def ch(s):
        return _scan_tile(s, n_ctx_chunks, n_chunks, reverse)

    const2 = lambda b, s: (0, 0)
    stacked = pl.BlockSpec((nb, None, n, HEAD_DV), lambda b, s: (b, ch(s), 0, 0))
    out_block = pl.BlockSpec((nb, c, GROUP), lambda b, s: (b, ch(s), 0))
    extra = () if acc is None else (acc,)
    return pl.pallas_call(
        functools.partial(_gdn_body, reverse=reverse, beta_col=direction * LIN_HEADS,
                          g_col=(2 + direction) * LIN_HEADS, nb=nb),
        grid=(bsz // nb, n_chunks),
        in_specs=[pl.BlockSpec(cs.shape, const2),
                  pl.BlockSpec((1, LANES), const2), pl.BlockSpec((1, LANES), const2),
                  pl.BlockSpec((nb, c, LANES), lambda b, s: (b, ch(s), 0)), stacked, stacked, stacked]
                 + [out_block] * len(extra),
        out_specs=out_block,
        out_shape=jax.ShapeDtypeStruct((bsz, n_chunks * c, GROUP), ACT),
        scratch_shapes=[pltpu.VMEM((nb, LIN_HEADS, HEAD_DV, HEAD_DV), F32)],
        compiler_params=_cparams("parallel", "arbitrary"),
        name="gdn_rev" if reverse else "gdn_fwd",
    )(cs, a_row, dt_row, tail, q, k, v, *extra)


def _post_body(*refs, even, tm, n_ctx, n_a, n_b):
    x_ref = refs[0]
    a_refs = refs[1:1 + n_a]
    ga_ref = refs[1 + n_a]
    b_refs = refs[2 + n_a:2 + n_a + n_b]
    gb_ref, w_ref, ml_ref, mc_ref, o_ref = refs[2 + n_a + n_b:]
    t = pl.program_id(1)
    a = sum(r[...].astype(F32) for r in a_refs)
    bsum = sum(r[...].astype(F32) for r in b_refs)
    if even:
        ya = a * _gelu_tanh(ga_ref[...].astype(F32))
        yb = _head_norm(bsum, True) * _silu(gb_ref[...].astype(F32))
    else:
        ya = _head_norm(a, False) * _silu(ga_ref[...].astype(F32))
        yb = _head_norm(bsum, False) * _silu(gb_ref[...].astype(F32))
    y = jnp.concatenate([ya.astype(BF16), yb.astype(BF16)], axis=-1)
    mix = jnp.dot(y, w_ref[...], preferred_element_type=F32)
    o_ref[...] = x_ref[...] + _row_select(t, tm, n_ctx, mc_ref, ml_ref, 2) * mix


def _post(x, mod, a_parts, b_parts, p, gate_a_col, gate_b_col, w_out, *, even, tm, n_ctx):
    bsz, l, d = x.shape
    ctx_row = bsz
    tok = lambda b, t: (b, t, 0)
    grp = pl.BlockSpec((None, tm, GROUP), tok)
    return pl.pallas_call(
        functools.partial(_post_body, even=even, tm=tm, n_ctx=n_ctx, n_a=len(a_parts), n_b=len(b_parts)),
        grid=(bsz, l // tm),
        in_specs=[pl.BlockSpec((None, tm, d), tok)] + [grp] * len(a_parts)
                 + [pl.BlockSpec((None, tm, GROUP), lambda b, t: (b, t, gate_a_col))] + [grp] * len(b_parts)
                 + [pl.BlockSpec((None, tm, GROUP), lambda b, t: (b, t, gate_b_col)),
                    pl.BlockSpec(w_out.shape, lambda b, t: (0, 0)),
                    pl.BlockSpec((None, 6, d), lambda b, t: (b, 0, 0)),
                    pl.BlockSpec((None, 6, d), lambda b, t: (ctx_row, 0, 0))],
        out_specs=pl.BlockSpec((None, tm, d), tok),
        out_shape=jax.ShapeDtypeStruct((bsz, l, d), F32),
        compiler_params=_cparams("parallel", "parallel"),
        name="post",
    )(x, *a_parts, p, *b_parts, p, w_out, mod, mod)


def _mlp_body(x_ref, g_ref, ml_ref, mc_ref, w1_ref, w2_ref, fg_ref, o_ref, hid_scr, *, tm, tf, n_ctx, final):
    t = pl.program_id(1)
    if final:
        x_ref = x_ref.at[0]
    y = _rms(x_ref[...], g_ref[...])
    shift = _row_select(t, tm, n_ctx, mc_ref, ml_ref, 3)
    scale = _row_select(t, tm, n_ctx, mc_ref, ml_ref, 4)
    h = (y * (1.0 + scale) + shift).astype(BF16)
    for f in range(w1_ref.shape[1] // tf):
        hid = jnp.maximum(jnp.dot(h, w1_ref[:, f * tf:(f + 1) * tf], preferred_element_type=F32), 0.0)
        hid_scr[:, f * tf:(f + 1) * tf] = (hid * hid).astype(BF16)
    acc = jnp.dot(hid_scr[...], w2_ref[...], preferred_element_type=F32)
    y = x_ref[...] + _row_select(t, tm, n_ctx, mc_ref, ml_ref, 5) * acc
    if final:
        y = _rms(y, fg_ref[...])
    o_ref[...] = y


def _mlp(x, mod, g, w1, w2, final_g, *, tm, tf, n_ctx, final):
    bsz, l, d = x.shape
    dff = w1.shape[1]
    ctx_row = bsz
    tok = lambda b, t: (b, t, 0)
    if final:
        l = l - n_ctx
        x_spec = pl.BlockSpec((pl.Element(1), pl.Element(tm), pl.Element(d)),
                              lambda b, t, off=n_ctx: (b, pl.multiple_of(off + t * tm, SUBLANES), 0))
        n_ctx = 0
    else:
        x_spec = pl.BlockSpec((None, tm, d), tok)
    resident = dict(pipeline_mode=pl.Buffered(1))
    return pl.pallas_call(
        functools.partial(_mlp_body, tm=tm, tf=tf, n_ctx=n_ctx, final=final),
        grid=(bsz, l // tm),
        in_specs=[x_spec,
                  pl.BlockSpec((1, d), lambda b, t: (0, 0)),
                  pl.BlockSpec((None, 6, d), lambda b, t: (b, 0, 0)),
                  pl.BlockSpec((None, 6, d), lambda b, t: (ctx_row, 0, 0)),
                  pl.BlockSpec((d, dff), lambda b, t: (0, 0), **resident),
                  pl.BlockSpec((dff, d), lambda b, t: (0, 0), **resident),
                  pl.BlockSpec((1, d), lambda b, t: (0, 0))],
        out_specs=pl.BlockSpec((None, tm, d), tok),
        out_shape=jax.ShapeDtypeStruct((bsz, l, d), F32),
        scratch_shapes=[pltpu.VMEM((tm, dff), BF16)],
        compiler_params=_cparams("parallel", "parallel"),
        name="mlp",
    )(x, g.reshape(1, d), mod, mod, w1, w2, final_g.reshape(1, d))


def _token_tile(l):
    for div in (4, 8, 16, 17, 34):
        if l % div == 0 and (l // div) % 16 == 0 and l // div <= 1088:
            return l // div
    return l


def kernel(x, c, ctx, c_ctx, ada_w, ada_b, norm1_g, norm2_g, mix_w_out, mlp_w1, mlp_w2, ev_w_in, lru_conv_w,
           lru_conv_b, lru_wa, lru_ba, lru_wx, lru_bx, lru_lambda, ret_log_gamma, od_w_in, hg_lb_logits,
           gdn_conv_w, gdn_a_log, gdn_dt_bias, final_g):
    bsz, seq, d = x.shape
    n_ctx = ctx.shape[1]
    depth = ada_w.shape[0]
    l = n_ctx + seq
    tm = _token_tile(l)
    tf = 1024
    nb = next(n for n in (8, 4, 2, 1) if bsz % n == 0)

    mods = _mods(c, c_ctx, ada_w, ada_b)
    ret_dk = (ev_w_in.shape[2] - 4 * GROUP) // (2 * RET_HEADS)
    cos_t, sin_t = _rope_tables(n_ctx, seq, ret_dk)

    xs = jnp.concatenate([ctx, x], axis=1)
    for layer in range(depth):
        mod = mods[layer]
        w_out = mix_w_out[layer].astype(BF16)
        if layer % 2 == 0:
            e = layer // 2
            (p,) = _proj(xs, mod, norm1_g[layer], ev_w_in[e].astype(BF16), None, tm=tm, n_ctx=n_ctx)
            lru = None
            for dr, rev in enumerate((False, True)):
                lru = _lru(p, lru_conv_w[e], lru_conv_b[e], _blockdiag_halves(lru_wa[e, dr]).astype(BF16),
                           lru_ba[e, dr], _blockdiag_halves(lru_wx[e, dr]).astype(BF16), lru_bx[e, dr],
                           lru_lambda[e, dr], lru, n_ctx=n_ctx, reverse=rev)
            ret = _retention(p, cos_t, sin_t, ret_log_gamma[e, 0], ret_log_gamma[e, 1],
                             n_ctx=n_ctx, reverse=False, nb=nb)
            ret = _retention(p, cos_t, sin_t, ret_log_gamma[e, 1], ret, n_ctx=n_ctx, reverse=True, nb=nb)
            a_parts, b_parts = (lru,), (ret,)
            gate_cols = (1, 4)
        else:
            o = layer // 2
            n_main = 9 * GROUP
            w_in = od_w_in[o]
            w_tail = jnp.pad(w_in[:, n_main:], ((0, 0), (0, LANES - (w_in.shape[1] - n_main)))).astype(BF16)
            p, tail = _proj(xs, mod, norm1_g[layer], w_in[:, :n_main].astype(BF16), w_tail, tm=tm, n_ctx=n_ctx)
            qkv = _gdn_prep(p, gdn_conv_w[o], n_ctx=n_ctx)
            gla = gdn = None
            for dr, rev in enumerate((False, True)):
                g_col = (2 + dr) * LIN_HEADS
                a_row = jnp.zeros((1, LANES), F32).at[0, g_col:g_col + LIN_HEADS].set(gdn_a_log[o, dr])
                dt_row = jnp.zeros((1, LANES), F32).at[0, g_col:g_col + LIN_HEADS].set(gdn_dt_bias[o, dr])
                gla = _gla(p, hg_lb_logits[dr], gla, n_ctx=n_ctx, reverse=rev, layer=o, nb=nb)
                gdn = _gdn(qkv, tail, a_row, dt_row, gdn, n_ctx=n_ctx, reverse=rev, direction=dr, nb=nb)
            a_parts, b_parts = (gla,), (gdn,)
            gate_cols = (4, 8)
        xs = _post(xs, mod, a_parts, b_parts, p, gate_cols[0], gate_cols[1], w_out,
                   even=layer % 2 == 0, tm=tm, n_ctx=n_ctx)
        last = layer == depth - 1
        xs = _mlp(xs, mod, norm2_g[layer], mlp_w1[layer].astype(BF16), mlp_w2[layer].astype(BF16), final_g,
                  tm=(_token_tile(seq) if last else tm) // 2, tf=tf, n_ctx=n_ctx, final=last)
    return xs
```

```python
import functools

import numpy as np
import jax
import jax.numpy as jnp
from jax import lax
from jax.experimental import pallas as pl
from jax.experimental.pallas import tpu as pltpu

F32 = jnp.float32
BF16 = jnp.bfloat16

EPS = 1e-6
GRID_W = 64
CONV_W = 4
CONV_LEFT = CONV_W // 2
LRU_BLOCKS = 8
LRU_C = 8.0
RET_HEADS = 4
RET_CHUNK = 256
ROPE_BASE = 10000.0
LIN_HEADS = 4
GLA_CHUNK = 128
LIN_CHUNK = 64
HEAD_DV = 128

LANES = 128
SUBLANES = 8
V7X_VMEM_BYTES = 64 * 1024 * 1024
VMEM_LIMIT = (V7X_VMEM_BYTES * 3) // 4
GROUP = 512

HALO = 2 * SUBLANES
ACT = jnp.bfloat16


def _cparams(*sem):
    return pltpu.CompilerParams(dimension_semantics=sem, vmem_limit_bytes=VMEM_LIMIT)


def _sigmoid(x):
    return 1.0 / (1.0 + jnp.exp(-x))


def _silu(x):
    return x * _sigmoid(x)


def _softplus(x):
    return jnp.maximum(x, 0.0) + jnp.log(1.0 + jnp.exp(-jnp.abs(x)))


def _gelu_tanh(x):
    return 0.5 * x * (1.0 + jnp.tanh(0.7978845608028654 * (x + 0.044715 * x * x * x)))


def _dot(a, b):
    return jnp.dot(a.astype(BF16), b.astype(BF16), preferred_element_type=F32)


def _dot_nt(a, b):
    return lax.dot_general(a.astype(BF16), b.astype(BF16), (((1,), (1,)), ((), ())),
                           preferred_element_type=F32)


def _split_bf16(x):
    hi = x.astype(BF16)
    lo = (x - hi.astype(F32)).astype(BF16)
    return hi, lo


def _dot_sel(m2, x):
    hi, lo = _split_bf16(x)
    return jnp.dot(m2, jnp.concatenate([hi, lo], axis=0), preferred_element_type=F32)


def _doubled(m):
    return jnp.asarray(np.concatenate([m, m], axis=1), BF16)


def _dot3(a, b):
    ah, al = _split_bf16(a)
    bh, bl = _split_bf16(b)
    return (jnp.dot(ah, bh, preferred_element_type=F32) + jnp.dot(ah, bl, preferred_element_type=F32)
            + jnp.dot(al, bh, preferred_element_type=F32))


def _row_select(t, tm, n_ctx, mc_ref, ml_ref, idx):
    rows = t * tm + lax.broadcasted_iota(jnp.int32, (tm, 1), 0)
    return jnp.where(rows < n_ctx, mc_ref[idx:idx + 1, :], ml_ref[idx:idx + 1, :])


def _rms(x, g):
    return x * lax.rsqrt(jnp.mean(x * x, axis=-1, keepdims=True) + EPS) * g


def _head_norm(o, center):
    outs = []
    for h in range(o.shape[-1] // HEAD_DV):
        oh = o[:, h * HEAD_DV:(h + 1) * HEAD_DV]
        if center:
            oh = oh - jnp.mean(oh, axis=-1, keepdims=True)
        outs.append(oh * lax.rsqrt(jnp.mean(oh * oh, axis=-1, keepdims=True) + EPS))
    return jnp.concatenate(outs, axis=-1)


def _scan_tile(step, n_first, n_total, reverse):
    if not reverse:
        return step
    return jnp.where(step < n_first, n_first - 1 - step, n_total - 1 - (step - n_first))


def _mods_body(s_ref, w_ref, b_ref, o_ref):
    s = _silu(s_ref[...])
    o_ref[...] = _dot3(s, w_ref[...]) + b_ref[...]


def _mods(c, c_ctx, ada_w, ada_b):
    depth, d, n6 = ada_w.shape
    bsz = c.shape[0]
    rows = -(-(bsz + 1) // SUBLANES) * SUBLANES
    s = jnp.zeros((rows, d), F32).at[:bsz].set(c).at[bsz].set(c_ctx)
    tn = 1024
    out = pl.pallas_call(
        _mods_body,
        grid=(depth, n6 // tn),
        in_specs=[pl.BlockSpec((rows, d), lambda l, n: (0, 0)),
                  pl.BlockSpec((None, d, tn), lambda l, n: (l, 0, n)),
                  pl.BlockSpec((None, 1, tn), lambda l, n: (l, 0, n))],
        out_specs=pl.BlockSpec((None, rows, tn), lambda l, n: (l, 0, n)),
        out_shape=jax.ShapeDtypeStruct((depth, rows, n6), F32),
        compiler_params=_cparams("parallel", "parallel"),
        name="mods",
    )(s, ada_w, ada_b.reshape(depth, 1, n6))
    return out.reshape(depth, rows, 6, d)


def _proj_body(*refs, tm, n_ctx, has_tail):
    if has_tail:
        x_ref, ml_ref, mc_ref, g_ref, w_ref, wt_ref, o_ref, ot_ref = refs
    else:
        x_ref, ml_ref, mc_ref, g_ref, w_ref, o_ref = refs
    t = pl.program_id(1)
    y = _rms(x_ref[...], g_ref[...])
    shift = _row_select(t, tm, n_ctx, mc_ref, ml_ref, 0)
    scale = _row_select(t, tm, n_ctx, mc_ref, ml_ref, 1)
    h = (y * (1.0 + scale) + shift).astype(BF16)
    o_ref[...] = jnp.dot(h, w_ref[...], preferred_element_type=F32).astype(o_ref.dtype)
    if has_tail:
        ot_ref[...] = jnp.dot(h, wt_ref[...], preferred_element_type=F32)


def _proj(x, mod, g, w, w_tail, *, tm, n_ctx):
    bsz, l, d = x.shape
    n_out = w.shape[1]
    ctx_row = bsz
    resident = dict(pipeline_mode=pl.Buffered(1))
    in_specs = [pl.BlockSpec((None, tm, d), lambda b, t: (b, t, 0)),
                pl.BlockSpec((None, 6, d), lambda b, t: (b, 0, 0)),
                pl.BlockSpec((None, 6, d), lambda b, t: (ctx_row, 0, 0)),
                pl.BlockSpec((1, d), lambda b, t: (0, 0)),
                pl.BlockSpec((d, n_out), lambda b, t: (0, 0), **resident)]
    out_specs = [pl.BlockSpec((None, tm, n_out), lambda b, t: (b, t, 0))]
    out_shape = [jax.ShapeDtypeStruct((bsz, l, n_out), ACT)]
    args = [x, mod, mod, g.reshape(1, d), w]
    if w_tail is not None:
        in_specs.append(pl.BlockSpec((d, LANES), lambda b, t: (0, 0), **resident))
        out_specs.append(pl.BlockSpec((None, tm, LANES), lambda b, t: (b, t, 0)))
        out_shape.append(jax.ShapeDtypeStruct((bsz, l, LANES), F32))
        args.append(w_tail)
    outs = pl.pallas_call(
        functools.partial(_proj_body, tm=tm, n_ctx=n_ctx, has_tail=w_tail is not None),
        grid=(bsz, l // tm),
        in_specs=in_specs, out_specs=out_specs, out_shape=out_shape,
        compiler_params=_cparams("parallel", "parallel"),
        name="proj",
    )(*args)
    return outs


def _lru_body(sm_ref, se_ref, x_ref, xp_ref, xn_ref, cw_ref, cb_ref, wa_ref, ba_ref, wx_ref, bx_ref, lam_ref,
              *rest, tl, n_ctx_tiles, n_tiles, reverse):
    acc_ref = rest[0] if len(rest) == 5 else None
    o_ref, a_scr, b_scr, h_scr = rest[-4:]
    step = pl.program_id(0)
    tile = _scan_tile(step, n_ctx_tiles, n_tiles, reverse)
    bsz = x_ref.shape[0]
    width = x_ref.shape[2]
    half = width // 2
    seg_first = jnp.logical_or(tile == 0, tile == n_ctx_tiles)
    seg_last = jnp.logical_or(tile == n_ctx_tiles - 1, tile == n_tiles - 1)
    keep_prev = jnp.where(seg_first, 0.0, 1.0)
    keep_next = jnp.where(seg_last, 0.0, 1.0)
    neg_c_sp = -LRU_C * _softplus(-lam_ref[...])

    @pl.when(step == 0)
    def _():
        h_scr[...] = jnp.zeros_like(h_scr)

    pair = 2 if bsz % 2 == 0 else 1

    def block_dot(ub, w_ref):
        return jnp.concatenate([jnp.dot(ub[:, :half], w_ref[0], preferred_element_type=F32),
                                jnp.dot(ub[:, half:], w_ref[1], preferred_element_type=F32)], axis=-1)

    def gates(bp, carry):
        items = [(bp * pair + n, hf, slice(hf * half, (hf + 1) * half)) for n in range(pair) for hf in range(2)]
        taps = [_conv_taps(sm_ref, se_ref, x_ref[b, :, ln], xp_ref[b, :, ln], xn_ref[b, :, ln],
                           keep_prev, keep_next, tl) for b, hf, ln in items]
        for (b, hf, ln), tp in zip(items, taps):
            u = cb_ref[:, ln] + sum(cw_ref[k:k + 1, ln] * tp[k] for k in range(CONV_W))
            ub = u.astype(BF16)
            r_pre = jnp.dot(ub, wa_ref[hf], preferred_element_type=F32) + ba_ref[:, ln]
            i_pre = jnp.dot(ub, wx_ref[hf], preferred_element_type=F32) + bx_ref[:, ln]
            a = jnp.exp(neg_c_sp[:, ln] * _sigmoid(r_pre))
            a_scr[b, :, ln] = a
            b_scr[b, :, ln] = jnp.sqrt(1.0 - a * a) * (_sigmoid(i_pre) * u)
        return carry

    lax.fori_loop(0, bsz // pair, gates, 0)

    def scan_rows(j, hs):
        idx = (tl - 1 - j) if reverse else j
        new = []
        for b in range(bsz):
            h = a_scr[b, pl.ds(idx, 1), :] * hs[b] + b_scr[b, pl.ds(idx, 1), :]
            b_scr[b, pl.ds(idx, 1), :] = h
            new.append(h)
        return tuple(new)

    hs = tuple(h_scr[b, 0:1, :] for b in range(bsz))
    hs = lax.fori_loop(0, tl, scan_rows, hs)
    for b in range(bsz):
        h_scr[b, 0:1, :] = hs[b]
    if acc_ref is None:
        o_ref[...] = b_scr[...].astype(o_ref.dtype)
    else:
        o_ref[...] = (b_scr[...] + acc_ref[...].astype(F32)).astype(o_ref.dtype)


def _lru(p, conv_w, conv_b, wa, ba, wx, bx, lam, acc=None, *, n_ctx, reverse, tl=256):
    bsz, l, _ = p.shape
    width = GROUP
    n_tiles = l // tl
    n_ctx_tiles = n_ctx // tl
    hb = tl // HALO
    n_hblocks = l // HALO

    def tile_of(s):
        return _scan_tile(s, n_ctx_tiles, n_tiles, reverse)

    vec = pl.BlockSpec((1, width), lambda s: (0, 0))
    mat = pl.BlockSpec((2, width // 2, width // 2), lambda s: (0, 0, 0))
    sm, se = (jnp.asarray(m, p.dtype) for m in _shift_matrices(tl))
    out_block = pl.BlockSpec((bsz, tl, width), lambda s: (0, tile_of(s), 0))
    extra = () if acc is None else (acc,)
    return pl.pallas_call(
        functools.partial(_lru_body, tl=tl, n_ctx_tiles=n_ctx_tiles, n_tiles=n_tiles, reverse=reverse),
        grid=(n_tiles,),
        in_specs=[pl.BlockSpec(sm.shape, lambda s: (0, 0)), pl.BlockSpec(se.shape, lambda s: (0, 0)),
                  pl.BlockSpec((bsz, tl, width), lambda s: (0, tile_of(s), 0)),
                  pl.BlockSpec((bsz, HALO, width), lambda s: (0, jnp.maximum(tile_of(s) * hb - 1, 0), 0)),
                  pl.BlockSpec((bsz, HALO, width),
                               lambda s: (0, jnp.minimum((tile_of(s) + 1) * hb, n_hblocks - 1), 0)),
                  pl.BlockSpec((CONV_W, width), lambda s: (0, 0)), vec, mat, vec, mat, vec, vec]
                 + [out_block] * len(extra),
        out_specs=out_block,
        out_shape=jax.ShapeDtypeStruct((bsz, l, width), ACT),
        scratch_shapes=[pltpu.VMEM((bsz, tl, width), F32),
                        pltpu.VMEM((bsz, tl, width), F32),
                        pltpu.VMEM((bsz, SUBLANES, width), F32)],
        compiler_params=_cparams("arbitrary"),
        name="lru_rev" if reverse else "lru_fwd",
    )(sm, se, p, p, p, conv_w, conv_b.reshape(1, width), wa, ba.reshape(1, width), wx, bx.reshape(1, width),
      lam.reshape(1, width), *extra)


def _blockdiag_halves(w):
    nb, n, _ = w.shape
    per = nb // 2
    w = w.reshape(2, per, n, n)
    eye = jnp.eye(per, dtype=w.dtype)
    out = w[:, :, :, None, :] * eye[None, :, None, :, None]
    return out.reshape(2, per * n, per * n)


def _ret_body(*refs, reverse, dk, nb):
    if reverse:
        lg_ref, q_ref, k_ref, v_ref, cos_ref, sin_ref, acc_ref, o_ref, qd_scr, kd_scr, cd_scr, st_scr = refs
    else:
        lg_ref, lgo_ref, q_ref, k_ref, v_ref, cos_ref, sin_ref, o_ref, dec_scr, qd_scr, kd_scr, cd_scr, st_scr = refs
    c = RET_CHUNK
    step = pl.program_id(1)

    @pl.when(step == 0)
    def _():
        st_scr[...] = jnp.zeros_like(st_scr)
        row = lax.broadcasted_iota(jnp.int32, (c, HEAD_DV), 0)
        pos = ((c - 1 - row) if reverse else row).astype(F32)
        for h in range(RET_HEADS):
            lg = lg_ref[h]
            qd_scr[h] = jnp.exp(lg * (pos + 1.0))
            kd_scr[h] = jnp.exp(lg * (c - 1.0 - pos))
            cd_scr[h] = jnp.exp(jnp.full((SUBLANES, LANES), c, F32) * lg)
        if not reverse:
            i = lax.broadcasted_iota(jnp.int32, (c, c), 0)
            j = lax.broadcasted_iota(jnp.int32, (c, c), 1)
            below = jnp.maximum(i - j, 0).astype(F32)
            above = jnp.maximum(j - i, 0).astype(F32)
            for h in range(RET_HEADS):
                dec_scr[h] = (jnp.where(i >= j, jnp.exp(lg_ref[h] * below), 0.0)
                              + jnp.where(j >= i, jnp.exp(lgo_ref[h] * above), 0.0))

    def rope(x):
        width = x.shape[-1]
        lane = lax.broadcasted_iota(jnp.int32, x.shape, 1)
        first = (lane & (dk - 1)) < dk // 2
        rot = jnp.where(first, pltpu.roll(x, width - dk // 2, axis=1), pltpu.roll(x, dk // 2, axis=1))
        return x * cos_ref[...] + rot * sin_ref[...]

    q = [rope(q_ref[bb].astype(F32)) for bb in range(nb)]
    k = [rope(k_ref[bb].astype(F32)) * (dk ** -0.5) for bb in range(nb)]
    pairs = [(bb, h) for bb in range(nb) for h in range(RET_HEADS)]
    qh = [q[bb][:, h * dk:(h + 1) * dk] for bb, h in pairs]
    kh = [k[bb][:, h * dk:(h + 1) * dk] for bb, h in pairs]
    vh = [v_ref[bb, :, h * HEAD_DV:(h + 1) * HEAD_DV] for bb, h in pairs]
    state = [st_scr[bb, h] for bb, h in pairs]
    o_x = [_dot(qh[i], state[i]) * qd_scr[h] for i, (bb, h) in enumerate(pairs)]
    if reverse:
        o_in = [acc_ref[bb, :, h * HEAD_DV:(h + 1) * HEAD_DV].astype(F32) for bb, h in pairs]
    else:
        s = [_dot_nt(qh[i], kh[i]) * dec_scr[h] for i, (bb, h) in enumerate(pairs)]
        o_in = [_dot(s[i], vh[i]) for i in range(len(pairs))]
    kv = [_dot((kh[i] * kd_scr[h][:, :dk]).T, vh[i]) for i, (bb, h) in enumerate(pairs)]
    for i, (bb, h) in enumerate(pairs):
        st_scr[bb, h] = state[i] * cd_scr[h][0:1, :] + kv[i]
        o_ref[bb, :, h * HEAD_DV:(h + 1) * HEAD_DV] = (o_in[i] + o_x[i]).astype(o_ref.dtype)


def _retention(p, cos_t, sin_t, log_gamma, other, *, n_ctx, reverse, nb):
    bsz, l, _ = p.shape
    c = RET_CHUNK
    qk_w = cos_t.shape[1]
    dk = qk_w // RET_HEADS
    n_chunks = l // c
    n_ctx_chunks = n_ctx // c

    def ch(s):
        return _scan_tile(s, n_ctx_chunks, n_chunks, reverse)

    q_col = (2 * GROUP) // qk_w
    smem = pl.BlockSpec(memory_space=pltpu.SMEM)
    out_block = pl.BlockSpec((nb, c, GROUP), lambda b, s: (b, ch(s), 0))
    data_specs = [pl.BlockSpec((nb, c, qk_w), lambda b, s: (b, ch(s), q_col)),
                  pl.BlockSpec((nb, c, qk_w), lambda b, s: (b, ch(s), q_col + 1)),
                  pl.BlockSpec((nb, c, GROUP), lambda b, s: (b, ch(s), 3)),
                  pl.BlockSpec((c, qk_w), lambda b, s: (ch(s), 0)),
                  pl.BlockSpec((c, qk_w), lambda b, s: (ch(s), 0))]
    scratch = [pltpu.VMEM((RET_HEADS, c, HEAD_DV), F32),
               pltpu.VMEM((RET_HEADS, c, HEAD_DV), F32),
               pltpu.VMEM((RET_HEADS, SUBLANES, LANES), F32),
               pltpu.VMEM((nb, RET_HEADS, dk, HEAD_DV), F32)]
    if reverse:
        in_specs = [smem] + data_specs + [out_block]
        args = (log_gamma, p, p, p, cos_t, sin_t, other)
    else:
        in_specs = [smem, smem] + data_specs
        args = (log_gamma, other, p, p, p, cos_t, sin_t)
        scratch = [pltpu.VMEM((RET_HEADS, c, c), F32)] + scratch
    return pl.pallas_call(
        functools.partial(_ret_body, reverse=reverse, dk=dk, nb=nb),
        grid=(bsz // nb, n_chunks),
        in_specs=in_specs,
        out_specs=out_block,
        out_shape=jax.ShapeDtypeStruct((bsz, l, GROUP), ACT),
        scratch_shapes=scratch,
        compiler_params=_cparams("parallel", "arbitrary"),
        name="ret_rev" if reverse else "ret_fwd",
    )(*args)


def _rope_tables(n_ctx, seq, dk):
    n_freq = dk // 4
    inv = jnp.power(ROPE_BASE, -jnp.arange(n_freq, dtype=F32) / n_freq)
    rows = seq // GRID_W
    r = jnp.arange(rows, dtype=F32)
    col = jnp.arange(GRID_W, dtype=F32)
    row_ang = jnp.broadcast_to(r[:, None, None] * inv, (rows, GRID_W, n_freq))
    col_ang = jnp.broadcast_to(col[None, :, None] * inv, (rows, GRID_W, n_freq))
    ang = jnp.concatenate([row_ang, col_ang], axis=-1).reshape(rows * GRID_W, 2 * n_freq)
    cos, sin = jnp.cos(ang), jnp.sin(ang)
    cos_h = jnp.concatenate([cos, cos], axis=-1)
    sin_h = jnp.concatenate([-sin, sin], axis=-1)
    cos_t = jnp.concatenate([jnp.ones((n_ctx, dk), F32), cos_h], axis=0)
    sin_t = jnp.concatenate([jnp.zeros((n_ctx, dk), F32), sin_h], axis=0)
    return jnp.tile(cos_t, (1, RET_HEADS)), jnp.tile(sin_t, (1, RET_HEADS))


_GLA_LEVELS = tuple(GLA_CHUNK >> (i + 1) for i in range(GLA_CHUNK.bit_length() - 1))


def _gla_matrices(reverse):
    c = GLA_CHUNK
    p = np.arange(c)
    mats = [(p[None, :] <= p[:, None]), (p[None, :] > p[:, None])]
    for s in _GLA_LEVELS:
        bs = (p // s) * s
        right = ((p // s) % 2 == 1)
        m_r = (p[None, :] > bs[:, None]) & (p[None, :] <= p[:, None])
        m_l = (p[None, :] > p[:, None]) & (p[None, :] <= (bs + s)[:, None])
        mats.append(np.where(right[:, None], m_r, m_l))
    m = np.stack(mats).astype(np.float32)
    if reverse:
        m = m[:, ::-1, ::-1]
    return m.reshape(len(mats) * c, c)


def _gla_body(m_ref, lb_ref, q_ref, f_ref, v_ref, *rest, reverse, layer, nb):
    acc_ref = rest[0] if len(rest) == 3 else None
    o_ref, st_scr = rest[-2:]
    c = GLA_CHUNK
    step = pl.program_id(1)

    @pl.when(step == 0)
    def _():
        st_scr[...] = jnp.zeros_like(st_scr)

    lg = lb_ref[...]
    e = jnp.exp(lg - jnp.max(lg, axis=0, keepdims=True))
    sm = e / jnp.sum(e, axis=0, keepdims=True)
    lb = jnp.sum(sm[0:layer + 1, :], axis=0, keepdims=True) - sm[0:1, :]

    ri = lax.broadcasted_iota(jnp.int32, (c, 1), 0)
    i2 = lax.broadcasted_iota(jnp.int32, (c, c), 0)
    j2 = lax.broadcasted_iota(jnp.int32, (c, c), 1)
    if reverse:
        ri, i2, j2 = c - 1 - ri, c - 1 - i2, c - 1 - j2
    last = 0 if reverse else c - 1
    width = q_ref.shape[-1]
    nh = width // HEAD_DV

    rows_bb = list(range(nb))
    pairs = [(bb, h) for bb in rows_bb for h in range(nh)]

    def head(x, h):
        return x[:, h * HEAD_DV:(h + 1) * HEAD_DV]

    f = [lb + (1.0 - lb) * _sigmoid(f_ref[bb].astype(F32)) for bb in rows_bb]
    k = [1.0 - x for x in f]
    q = [_silu(q_ref[bb].astype(F32)) for bb in rows_bb]
    ex = [_dot_sel(m_ref[...], jnp.log(x)) for x in f]
    a_last = [jnp.exp(x[last:last + 1, :]) for x in ex]
    qb = [x.astype(BF16) for x in q]
    kb = [x.astype(BF16) for x in k]
    qa = [qb[bb] * jnp.exp(ex[bb][0:c, :].astype(BF16)) for bb in rows_bb]
    kl = [kb[bb] * jnp.exp(ex[bb][c:2 * c, :].astype(BF16)) for bb in rows_bb]
    scores = [jnp.where(i2 == j2, _dot_nt(head(qb[bb], h), head(kb[bb], h)), 0.0) for bb, h in pairs]
    for lvl, s in enumerate(_GLA_LEVELS):
        sh = s.bit_length() - 1
        right = jnp.broadcast_to(((ri >> sh) & 1).astype(F32), (c, width)).astype(BF16) > 0
        x = [jnp.where(right, qb[bb], kb[bb]) * jnp.exp(ex[bb][(2 + lvl) * c:(3 + lvl) * c, :].astype(BF16))
             for bb in rows_bb]
        bi, bj = i2 >> sh, j2 >> sh
        valid = bj == jnp.where((bi & 1) == 1, bi - 1, -1)
        gram = [_dot_nt(head(x[bb], h), head(x[bb], h)) for bb, h in pairs]
        scores = [jnp.where(valid, gm, sc) for sc, gm in zip(scores, gram)]

    state_t = [st_scr[bb, h] for bb, h in pairs]
    o_intra = [_dot(scores[i], head(v_ref[bb], h)) for i, (bb, h) in enumerate(pairs)]
    o_inter = [_dot_nt(head(qa[bb], h), state_t[i]) for i, (bb, h) in enumerate(pairs)]
    kv = [_dot(head(v_ref[bb], h).astype(F32).T, head(kl[bb], h)) for bb, h in pairs]
    for i, (bb, h) in enumerate(pairs):
        o = o_intra[i] + o_inter[i]
        if acc_ref is not None:
            o = o + acc_ref[bb, :, h * HEAD_DV:(h + 1) * HEAD_DV].astype(F32)
        o_ref[bb, :, h * HEAD_DV:(h + 1) * HEAD_DV] = o.astype(o_ref.dtype)
        st_scr[bb, h] = state_t[i] * head(a_last[bb], h) + kv[i]


def _gla(p, lb_logits, acc=None, *, n_ctx, reverse, layer, nb):
    bsz, l, _ = p.shape
    c = GLA_CHUNK
    n_chunks = l // c
    n_ctx_chunks = n_ctx // c
    m = _doubled(_gla_matrices(reverse))

    def ch(s):
        return _scan_tile(s, n_ctx_chunks, n_chunks, reverse)

    f_col = 2 if reverse else 1
    out_block = pl.BlockSpec((nb, c, GROUP), lambda b, s: (b, ch(s), 0))
    extra = () if acc is None else (acc,)
    return pl.pallas_call(
        functools.partial(_gla_body, reverse=reverse, layer=layer, nb=nb),
        grid=(bsz // nb, n_chunks),
        in_specs=[pl.BlockSpec(m.shape, lambda b, s: (0, 0)),
                  pl.BlockSpec(lb_logits.shape, lambda b, s: (0, 0)),
                  pl.BlockSpec((nb, c, GROUP), lambda b, s: (b, ch(s), 0)),
                  pl.BlockSpec((nb, c, GROUP), lambda b, s: (b, ch(s), f_col)),
                  pl.BlockSpec((nb, c, GROUP), lambda b, s: (b, ch(s), 3))] + [out_block] * len(extra),
        out_specs=out_block,
        out_shape=jax.ShapeDtypeStruct((bsz, l, GROUP), ACT),
        scratch_shapes=[pltpu.VMEM((nb, LIN_HEADS, HEAD_DV, HEAD_DV), F32)],
        compiler_params=_cparams("parallel", "arbitrary"),
        name="gla_rev" if reverse else "gla_fwd",
    )(m, lb_logits, p, p, p, *extra)


def _shift_matrices(rows):
    offs = [t - CONV_LEFT for t in range(CONV_W) if t != CONV_LEFT]
    main = np.zeros((len(offs) * rows, rows), np.float32)
    edge = np.zeros((len(offs) * SUBLANES, 2 * HALO), np.float32)
    for n, off in enumerate(offs):
        for i in range(rows):
            j = i + off
            if 0 <= j < rows:
                main[n * rows + i, j] = 1.0
            elif j < 0:
                edge[n * SUBLANES + i, HALO + j] = 1.0
            else:
                edge[n * SUBLANES + i - (rows - SUBLANES), HALO + j - rows] = 1.0
    return main, edge


def _conv_taps(sm_ref, se_ref, x, xp, xn, keep_prev, keep_next, rows):
    sh = jnp.dot(sm_ref[...], x, preferred_element_type=F32)
    halo = jnp.concatenate([xp * keep_prev.astype(xp.dtype), xn * keep_next.astype(xn.dtype)], axis=0)
    fix = jnp.dot(se_ref[...], halo, preferred_element_type=F32)
    taps, n = [], 0
    for t in range(CONV_W):
        off = t - CONV_LEFT
        if off == 0:
            taps.append(x.astype(F32))
            continue
        s = sh[n * rows:(n + 1) * rows, :]
        f = fix[n * SUBLANES:(n + 1) * SUBLANES, :]
        if off < 0:
            s = jnp.concatenate([s[0:SUBLANES] + f, s[SUBLANES:]], axis=0)
        else:
            s = jnp.concatenate([s[:rows - SUBLANES], s[rows - SUBLANES:] + f], axis=0)
        taps.append(s)
        n += 1
    return taps


def _gdn_prep_body(sm_ref, se_ref, cw_ref, q_ref, qp_ref, qn_ref, k_ref, kp_ref, kn_ref, v_ref, vp_ref, vn_ref,
                   qo_ref, ko_ref, vo_ref, *, tp, n_ctx_tiles, n_tiles):
    c = LIN_CHUNK
    nh = LIN_HEADS
    tile = pl.program_id(1)
    seg_first = jnp.logical_or(tile == 0, tile == n_ctx_tiles)
    seg_last = jnp.logical_or(tile == n_ctx_tiles - 1, tile == n_tiles - 1)
    keep_prev = jnp.where(seg_first, 0.0, 1.0)
    keep_next = jnp.where(seg_last, 0.0, 1.0)

    all_taps = [_conv_taps(sm_ref, se_ref, x[...], xp[...], xn[...], keep_prev, keep_next, tp)
                for x, xp, xn in ((q_ref, qp_ref, qn_ref), (k_ref, kp_ref, kn_ref), (v_ref, vp_ref, vn_ref))]

    def conv_silu(col):
        taps = all_taps[col]
        u = cw_ref[0:1, col * GROUP:(col + 1) * GROUP] * taps[0]
        for t in range(1, CONV_W):
            u = u + cw_ref[t:t + 1, col * GROUP:(col + 1) * GROUP] * taps[t]
        return _silu(u)

    def l2n(x):
        return x * lax.rsqrt(jnp.sum(x * x, axis=-1, keepdims=True) + EPS)

    def emit(out_ref, x, norm, scale):
        for h in range(nh):
            xh = x[:, h * HEAD_DV:(h + 1) * HEAD_DV]
            if norm:
                xh = l2n(xh) * scale
            for j in range(tp // c):
                out_ref[j, h * c:(h + 1) * c, :] = xh[j * c:(j + 1) * c, :].astype(out_ref.dtype)

    emit(qo_ref, conv_silu(0), True, HEAD_DV ** -0.5)
    emit(ko_ref, conv_silu(1), True, 1.0)
    emit(vo_ref, conv_silu(2), False, 1.0)


def _gdn_prep(p, conv_w, *, n_ctx, tp=256):
    bsz, l, _ = p.shape
    c = LIN_CHUNK
    n_tiles = l // tp
    hb = tp // HALO
    n_hblocks = l // HALO

    def triple(col):
        return [pl.BlockSpec((None, tp, GROUP), lambda b, t: (b, t, col)),
                pl.BlockSpec((None, HALO, GROUP), lambda b, t: (b, jnp.maximum(t * hb - 1, 0), col)),
                pl.BlockSpec((None, HALO, GROUP), lambda b, t: (b, jnp.minimum((t + 1) * hb, n_hblocks - 1), col))]

    out_spec = pl.BlockSpec((None, tp // c, LIN_HEADS * c, HEAD_DV), lambda b, t: (b, t, 0, 0))
    out_sds = jax.ShapeDtypeStruct((bsz, l // c, LIN_HEADS * c, HEAD_DV), ACT)
    sm, se = (jnp.asarray(m, p.dtype) for m in _shift_matrices(tp))
    const2 = lambda b, t: (0, 0)
    return pl.pallas_call(
        functools.partial(_gdn_prep_body, tp=tp, n_ctx_tiles=n_ctx // tp, n_tiles=n_tiles),
        grid=(bsz, n_tiles),
        in_specs=[pl.BlockSpec(sm.shape, const2), pl.BlockSpec(se.shape, const2), pl.BlockSpec(conv_w.shape, const2)]
                 + triple(5) + triple(6) + triple(7),
        out_specs=[out_spec] * 3, out_shape=[out_sds] * 3,
        compiler_params=_cparams("parallel", "parallel"),
        name="gdn_prep",
    )(sm, se, conv_w, *([p] * 9))


def _gdn_body(cs_ref, arow_ref, dtrow_ref, t_ref, q_ref, k_ref, v_ref, *rest, reverse, beta_col, g_col, nb):
    acc_ref = rest[0] if len(rest) == 3 else None
    o_ref, st_scr = rest[-2:]
    c = LIN_CHUNK
    nh = LIN_HEADS
    n = nh * c
    step = pl.program_id(1)

    @pl.when(step == 0)
    def _():
        st_scr[...] = jnp.zeros_like(st_scr)

    i2 = lax.broadcasted_iota(jnp.int32, (c, n), 0)
    lane = lax.broadcasted_iota(jnp.int32, (c, n), 1)
    j2 = lane & (c - 1)
    grp = lane >> 6
    mask = (j2 >= i2) if reverse else (j2 <= i2)
    same32 = (i2 >> 5) == (j2 >> 5)
    same16 = (i2 >> 4) == (j2 >> 4)
    eye = jnp.where(i2 == j2, 1.0, 0.0)
    grp_rows = [jnp.where((lax.broadcasted_iota(jnp.int32, (1, n), 1) >> 6) == h, 1.0, 0.0).astype(BF16)
                for h in range(nh)]
    last = 0 if reverse else c - 1

    def each(fn, *lists):
        return [fn(*args) for args in zip(*lists)]

    def head_bcast(x, col):
        return jnp.concatenate([jnp.broadcast_to(x[:, col + h:col + h + 1], (c, HEAD_DV)) for h in range(nh)], axis=0)

    def side_by_side(x):
        out = x[(nh - 1) * c:nh * c, :]
        for h in range(nh - 2, -1, -1):
            out = jnp.where(grp == h, x[h * c:(h + 1) * c, :], out)
        return out

    def block_diag(x):
        xb = x.astype(BF16)
        return jnp.concatenate([xb * grp_rows[h] for h in range(nh)], axis=0)

    def dot_sbs(x, y_bd):
        return jnp.dot(x.astype(BF16), y_bd, preferred_element_type=F32)

    rows_bb = list(range(nb))
    q = [q_ref[bb].astype(F32) for bb in rows_bb]
    k = [k_ref[bb].astype(F32) for bb in rows_bb]
    v = [v_ref[bb].astype(F32) for bb in rows_bb]

    tail = [t_ref[bb] for bb in rows_bb]
    beta = [head_bcast(_sigmoid(x), beta_col) for x in tail]
    g = [head_bcast(-jnp.exp(arow_ref[...]) * _softplus(x + dtrow_ref[...]), g_col) for x in tail]
    gc = [_dot_sel(cs_ref[...], x) for x in g]
    gc_last = [jnp.concatenate([jnp.broadcast_to(x[h * c + last:h * c + last + 1, :], (c, HEAD_DV))
                                for h in range(nh)], axis=0) for x in gc]
    diff = [side_by_side(jnp.concatenate([x, x], axis=1)) - jnp.transpose(x)[0:1, :] for x in gc]
    gam = [jnp.where(mask, jnp.exp(jnp.where(mask, x, 0.0)), 0.0) for x in diff]

    kb = each(lambda a, b: a * b, k, beta)
    kq = each(lambda a, b, kx: _dot_nt(jnp.concatenate([a, b], axis=0), kx), kb, q, k)
    mm = each(lambda x, gm: jnp.where(i2 == j2, 0.0, side_by_side(x[:n]) * gm), kq, gam)
    qk = each(lambda x, gm: side_by_side(x[n:]) * gm, kq, gam)
    egc = [jnp.exp(x) for x in gc]
    rhs = each(lambda a, b, kbx, e: jnp.concatenate([a * b, kbx * e], axis=1), v, beta, kb, egc)

    md = [jnp.where(same16, x, 0.0) for x in mm]
    m2 = each(lambda a: dot_sbs(a, block_diag(a)), md)
    t = [eye - x for x in md]
    tm = each(lambda a, b: dot_sbs(jnp.concatenate([a, b], axis=0), block_diag(b)), t, m2)
    t = each(lambda a, x: a + x[:c], t, tm)
    m4 = [x[c:] for x in tm]
    tm = each(lambda a, b: dot_sbs(jnp.concatenate([a, b], axis=0), block_diag(b)), t, m4)
    t = each(lambda a, x: a + x[:c], t, tm)
    t = each(lambda a, x: a + dot_sbs(a, block_diag(x[c:])), t, tm)
    c32 = each(lambda a, b: jnp.where(same32, a - b, 0.0), mm, md)
    ct = each(lambda a, b: dot_sbs(a, block_diag(b)), c32, t)
    t = each(lambda a, b: a - dot_sbs(a, block_diag(b)), t, ct)
    c64 = [jnp.where(same32, 0.0, x) for x in mm]
    ct = each(lambda a, b: dot_sbs(a, block_diag(b)), c64, t)
    t = each(lambda a, b: a - dot_sbs(a, block_diag(b)), t, ct)
    sol = each(lambda a, b: jnp.dot(block_diag(a), b.astype(BF16), preferred_element_type=F32), t, rhs)

    qd = each(lambda a, e: a * e, q, egc)
    kd = each(lambda a, gl, x: a * jnp.exp(gl - x), k, gc_last, gc)
    g_last = [jnp.exp(x) for x in gc_last]
    heads = list(range(nh))
    state = [[st_scr[bb, h] for h in heads] for bb in rows_bb]
    ws = [[_dot(jnp.concatenate([sol[bb][h * c:(h + 1) * c, HEAD_DV:], qd[bb][h * c:(h + 1) * c]], axis=0),
                state[bb][h]) for h in heads] for bb in rows_bb]
    v_new = [jnp.concatenate([sol[bb][h * c:(h + 1) * c, :HEAD_DV] - ws[bb][h][:c] for h in heads], axis=0)
             for bb in rows_bb]
    o_intra = each(lambda a, b: jnp.dot(block_diag(a), b.astype(BF16), preferred_element_type=F32), qk, v_new)
    for h in heads:
        rows = slice(h * c, (h + 1) * c)
        for bb in rows_bb:
            o = o_intra[bb][rows] + ws[bb][h][c:]
            if acc_ref is not None:
                o = o + acc_ref[bb, :, h * HEAD_DV:(h + 1) * HEAD_DV].astype(F32)
            o_ref[bb, :, h * HEAD_DV:(h + 1) * HEAD_DV] = o.astype(o_ref.dtype)
            st_scr[bb, h] = (state[bb][h] * g_last[bb][h * c:h * c + 1, :]
                             + _dot(kd[bb][rows].T, v_new[bb][rows]))


def _gdn_cumsum_matrix(reverse):
    c, nh = LIN_CHUNK, LIN_HEADS
    p = np.arange(c)
    tri = (p[None, :] >= p[:, None]) if reverse else (p[None, :] <= p[:, None])
    return np.kron(np.eye(nh), tri.astype(np.float32))


def _gdn(qkv, tail, a_row, dt_row, acc=None, *, n_ctx, reverse, direction, nb):
    q, k, v = qkv
    bsz, n_chunks, n, _ = q.shape
    c = LIN_CHUNK
    n_ctx_chunks = n_ctx // c
    cs = _doubled(_gdn_cumsum_matrix(reverse))

    def ch(s):
        return _scan_tile(s, n_ctx_chunks, n_chunks, reverse)

    const2 = lambda b, s: (0, 0)
    stacked = pl.BlockSpec((nb, None, n, HEAD_DV), lambda b, s: (b, ch(s), 0, 0))
    out_block = pl.BlockSpec((nb, c, GROUP), lambda b, s: (b, ch(s), 0))
    extra = () if acc is None else (acc,)
    return pl.pallas_call(
        functools.partial(_gdn_body, reverse=reverse, beta_col=direction * LIN_HEADS,
                          g_col=(2 + direction) * LIN_HEADS, nb=nb),
        grid=(bsz // nb, n_chunks),
        in_specs=[pl.BlockSpec(cs.shape, const2),
                  pl.BlockSpec((1, LANES), const2), pl.BlockSpec((1, LANES), const2),
                  pl.BlockSpec((nb, c, LANES), lambda b, s: (b, ch(s), 0)), stacked, stacked, stacked]
                 + [out_block] * len(extra),
        out_specs=out_block,
        out_shape=jax.ShapeDtypeStruct((bsz, n_chunks * c, GROUP), ACT),
        scratch_shapes=[pltpu.VMEM((nb, LIN_HEADS, HEAD_DV, HEAD_DV), F32)],
        compiler_params=_cparams("parallel", "arbitrary"),
        name="gdn_rev" if reverse else "gdn_fwd",
    )(cs, a_row, dt_row, tail, q, k, v, *extra)


def _post_body(*refs, even, tm, n_ctx, n_a, n_b):
    x_ref = refs[0]
    a_refs = refs[1:1 + n_a]
    ga_ref = refs[1 + n_a]
    b_refs = refs[2 + n_a:2 + n_a + n_b]
    gb_ref, w_ref, ml_ref, mc_ref, o_ref = refs[2 + n_a + n_b:]
    t = pl.program_id(1)
    a = sum(r[...].astype(F32) for r in a_refs)
    bsum = sum(r[...].astype(F32) for r in b_refs)
    if even:
        ya = a * _gelu_tanh(ga_ref[...].astype(F32))
        yb = _head_norm(bsum, True) * _silu(gb_ref[...].astype(F32))
    else:
        ya = _head_norm(a, False) * _silu(ga_ref[...].astype(F32))
        yb = _head_norm(bsum, False) * _silu(gb_ref[...].astype(F32))
    y = jnp.concatenate([ya.astype(BF16), yb.astype(BF16)], axis=-1)
    mix = jnp.dot(y, w_ref[...], preferred_element_type=F32)
    o_ref[...] = x_ref[...] + _row_select(t, tm, n_ctx, mc_ref, ml_ref, 2) * mix


def _post(x, mod, a_parts, b_parts, p, gate_a_col, gate_b_col, w_out, *, even, tm, n_ctx):
    bsz, l, d = x.shape
    ctx_row = bsz
    tok = lambda b, t: (b, t, 0)
    grp = pl.BlockSpec((None, tm, GROUP), tok)
    return pl.pallas_call(
        functools.partial(_post_body, even=even, tm=tm, n_ctx=n_ctx, n_a=len(a_parts), n_b=len(b_parts)),
        grid=(bsz, l // tm),
        in_specs=[pl.BlockSpec((None, tm, d), tok)] + [grp] * len(a_parts)
                 + [pl.BlockSpec((None, tm, GROUP), lambda b, t: (b, t, gate_a_col))] + [grp] * len(b_parts)
                 + [pl.BlockSpec((None, tm, GROUP), lambda b, t: (b, t, gate_b_col)),
                    pl.BlockSpec(w_out.shape, lambda b, t: (0, 0)),
                    pl.BlockSpec((None, 6, d), lambda b, t: (b, 0, 0)),
                    pl.BlockSpec((None, 6, d), lambda b, t: (ctx_row, 0, 0))],
        out_specs=pl.BlockSpec((None, tm, d), tok),
        out_shape=jax.ShapeDtypeStruct((bsz, l, d), F32),
        compiler_params=_cparams("parallel", "parallel"),
        name="post",
    )(x, *a_parts, p, *b_parts, p, w_out, mod, mod)


def _mlp_body(x_ref, g_ref, ml_ref, mc_ref, w1_ref, w2_ref, fg_ref, o_ref, hid_scr, *, tm, tf, n_ctx, final):
    t = pl.program_id(1)
    if final:
        x_ref = x_ref.at[0]
    y = _rms(x_ref[...], g_ref[...])
    shift = _row_select(t, tm, n_ctx, mc_ref, ml_ref, 3)
    scale = _row_select(t, tm, n_ctx, mc_ref, ml_ref, 4)
    h = (y * (1.0 + scale) + shift).astype(BF16)
    for f in range(w1_ref.shape[1] // tf):
        hid = jnp.maximum(jnp.dot(h, w1_ref[:, f * tf:(f + 1) * tf], preferred_element_type=F32), 0.0)
        hid_scr[:, f * tf:(f + 1) * tf] = (hid * hid).astype(BF16)
    acc = jnp.dot(hid_scr[...], w2_ref[...], preferred_element_type=F32)
    y = x_ref[...] + _row_select(t, tm, n_ctx, mc_ref, ml_ref, 5) * acc
    if final:
        y = _rms(y, fg_ref[...])
    o_ref[...] = y


def _mlp(x, mod, g, w1, w2, final_g, *, tm, tf, n_ctx, final):
    bsz, l, d = x.shape
    dff = w1.shape[1]
    ctx_row = bsz
    tok = lambda b, t: (b, t, 0)
    if final:
        l = l - n_ctx
        x_spec = pl.BlockSpec((pl.Element(1), pl.Element(tm), pl.Element(d)),
                              lambda b, t, off=n_ctx: (b, pl.multiple_of(off + t * tm, SUBLANES), 0))
        n_ctx = 0
    else:
        x_spec = pl.BlockSpec((None, tm, d), tok)
    resident = dict(pipeline_mode=pl.Buffered(1))
    return pl.pallas_call(
        functools.partial(_mlp_body, tm=tm, tf=tf, n_ctx=n_ctx, final=final),
        grid=(bsz, l // tm),
        in_specs=[x_spec,
                  pl.BlockSpec((1, d), lambda b, t: (0, 0)),
                  pl.BlockSpec((None, 6, d), lambda b, t: (b, 0, 0)),
                  pl.BlockSpec((None, 6, d), lambda b, t: (ctx_row, 0, 0)),
                  pl.BlockSpec((d, dff), lambda b, t: (0, 0), **resident),
                  pl.BlockSpec((dff, d), lambda b, t: (0, 0), **resident),
                  pl.BlockSpec((1, d), lambda b, t: (0, 0))],
        out_specs=pl.BlockSpec((None, tm, d), tok),
        out_shape=jax.ShapeDtypeStruct((bsz, l, d), F32),
        scratch_shapes=[pltpu.VMEM((tm, dff), BF16)],
        compiler_params=_cparams("parallel", "parallel"),
        name="mlp",
    )(x, g.reshape(1, d), mod, mod, w1, w2, final_g.reshape(1, d))


def _token_tile(l):
    for div in (4, 8, 16, 17, 34):
        if l % div == 0 and (l // div) % 16 == 0 and l // div <= 1088:
            return l // div
    return l


def kernel(x, c, ctx, c_ctx, ada_w, ada_b, norm1_g, norm2_g, mix_w_out, mlp_w1, mlp_w2, ev_w_in, lru_conv_w,
           lru_conv_b, lru_wa, lru_ba, lru_wx, lru_bx, lru_lambda, ret_log_gamma, od_w_in, hg_lb_logits,
           gdn_conv_w, gdn_a_log, gdn_dt_bias, final_g):
    bsz, seq, d = x.shape
    n_ctx = ctx.shape[1]
    depth = ada_w.shape[0]
    l = n_ctx + seq
    tm = _token_tile(l)
    tf = 1024
    nb = next(n for n in (8, 4, 2, 1) if bsz % n == 0)

    mods = _mods(c, c_ctx, ada_w, ada_b)
    ret_dk = (ev_w_in.shape[2] - 4 * GROUP) // (2 * RET_HEADS)
    cos_t, sin_t = _rope_tables(n_ctx, seq, ret_dk)

    xs = jnp.concatenate([ctx, x], axis=1)
    for layer in range(depth):
        mod = mods[layer]
        w_out = mix_w_out[layer].astype(BF16)
        if layer % 2 == 0:
            e = layer // 2
            (p,) = _proj(xs, mod, norm1_g[layer], ev_w_in[e].astype(BF16), None, tm=tm, n_ctx=n_ctx)
            lru = None
            for dr, rev in enumerate((False, True)):
                lru = _lru(p, lru_conv_w[e], lru_conv_b[e], _blockdiag_halves(lru_wa[e, dr]).astype(BF16),
                           lru_ba[e, dr], _blockdiag_halves(lru_wx[e, dr]).astype(BF16), lru_bx[e, dr],
                           lru_lambda[e, dr], lru, n_ctx=n_ctx, reverse=rev)
            ret = _retention(p, cos_t, sin_t, ret_log_gamma[e, 0], ret_log_gamma[e, 1],
                             n_ctx=n_ctx, reverse=False, nb=nb)
            ret = _retention(p, cos_t, sin_t, ret_log_gamma[e, 1], ret, n_ctx=n_ctx, reverse=True, nb=nb)
            a_parts, b_parts = (lru,), (ret,)
            gate_cols = (1, 4)
        else:
            o = layer // 2
            n_main = 9 * GROUP
            w_in = od_w_in[o]
            w_tail = jnp.pad(w_in[:, n_main:], ((0, 0), (0, LANES - (w_in.shape[1] - n_main)))).astype(BF16)
            p, tail = _proj(xs, mod, norm1_g[layer], w_in[:, :n_main].astype(BF16), w_tail, tm=tm, n_ctx=n_ctx)
            qkv = _gdn_prep(p, gdn_conv_w[o], n_ctx=n_ctx)
            gla = gdn = None
            for dr, rev in enumerate((False, True)):
                g_col = (2 + dr) * LIN_HEADS
                a_row = jnp.zeros((1, LANES), F32).at[0, g_col:g_col + LIN_HEADS].set(gdn_a_log[o, dr])
                dt_row = jnp.zeros((1, LANES), F32).at[0, g_col:g_col + LIN_HEADS].set(gdn_dt_bias[o, dr])
                gla = _gla(p, hg_lb_logits[dr], gla, n_ctx=n_ctx, reverse=rev, layer=o, nb=nb)
                gdn = _gdn(qkv, tail, a_row, dt_row, gdn, n_ctx=n_ctx, reverse=rev, direction=dr, nb=nb)
            a_parts, b_parts = (gla,), (gdn,)
            gate_cols = (4, 8)
        xs = _post(xs, mod, a_parts, b_parts, p, gate_cols[0], gate_cols[1], w_out,
                   even=layer % 2 == 0, tm=tm, n_ctx=n_ctx)
        last = layer == depth - 1
        xs = _mlp(xs, mod, norm2_g[layer], mlp_w1[layer].astype(BF16), mlp_w2[layer].astype(BF16), final_g,
                  tm=(_token_tile(seq) if last else tm) // 2, tf=tf, n_ctx=n_ctx, final=last)
    return xs
```
